```python
import math
import jax, jax.numpy as jnp
from jax import lax
import numpy as np

D_MODEL = 1024
BATCH = 2
SEQ = 16384
DEPTH = 2

N_HEADS_ATTN = 8
HEAD_DIM = 64
N_KV_GROUPS = 2
HEADS_PER_GROUP = N_HEADS_ATTN // N_KV_GROUPS
D_ATTN = N_HEADS_ATTN * HEAD_DIM
D_KV = N_KV_GROUPS * HEAD_DIM
D_CONV = D_MODEL - D_ATTN
CONV_WIDTH = 3
CMP_BLOCK = 32
CMP_STRIDE = 16
CMP_OVERLAP = CMP_BLOCK // CMP_STRIDE
CMP_HIDDEN = 256
SEL_BLOCK = 64
SEL_RATIO = SEL_BLOCK // CMP_STRIDE
N_SELECT = 16
WINDOW = 512
Q_BLOCK = 128
ROPE_THETA = 500000.0
ROPE_DIM = HEAD_DIM // 4
D_FF = 2816
N_GATES = 3 * N_HEADS_ATTN
D_IN_PROJ = D_ATTN + 6 * D_KV + N_GATES + 3 * D_CONV
EPS = 1e-6
NEG_INF = -1e30
FORCE_SCORE = 1e9
MAX_POS_OFFSET = 1024
SCALE = 1.0 / math.sqrt(HEAD_DIM)

kernel_name = "hybrid_nsa_shortconv_macaron"


def rms_norm(x, g):
    xf = x.astype(jnp.float32)
    y = xf * lax.rsqrt(jnp.mean(xf * xf, axis=-1, keepdims=True) + EPS)
    return (y * g.astype(jnp.float32)).astype(x.dtype)


def swiglu(x, w_gate, w_up, w_down):
    return (jax.nn.silu(x @ w_gate) * (x @ w_up)) @ w_down


def partial_rope(x, positions):
    half = ROPE_DIM // 2
    inv_freq = ROPE_THETA ** (-jnp.arange(half, dtype=jnp.float32) * 2.0 / ROPE_DIM)
    ang = positions.astype(jnp.float32)[..., None] * inv_freq
    cos = jnp.cos(ang)[:, :, None, :]
    sin = jnp.sin(ang)[:, :, None, :]
    xf = x.astype(jnp.float32)
    x1, x2, rest = xf[..., :half], xf[..., half:ROPE_DIM], xf[..., ROPE_DIM:]
    out = jnp.concatenate([x1 * cos - x2 * sin, x2 * cos + x1 * sin, rest], axis=-1)
    return out.astype(x.dtype)


def compress_blocks(k, pe, w1, w2):
    B, S, G, dk = k.shape
    n_chunks = S // CMP_STRIDE
    n_cmp = n_chunks - CMP_OVERLAP + 1
    chunks = k.reshape(B, n_chunks, CMP_STRIDE, G, dk)
    blocks = jnp.concatenate([chunks[:, j:j + n_cmp] for j in range(CMP_OVERLAP)], axis=2)
    blocks = blocks + pe[None, None, :, None, :]
    flat = blocks.transpose(0, 1, 3, 2, 4).reshape(B, n_cmp, G, CMP_BLOCK * dk)
    return jax.nn.gelu(flat @ w1) @ w2


def nsa_group(q, k_cmp, v_cmp, k_slc, v_slc, k_win, v_win, gates, positions,
              pe_k, w1_k, w2_k, pe_v, w1_v, w2_v):
    B, S = q.shape[:2]
    H, G, HPG, dk = N_HEADS_ATTN, N_KV_GROUPS, HEADS_PER_GROUP, HEAD_DIM
    q_raw = q.reshape(B, S, H, dk)
    q_rot = partial_rope(q_raw, positions)
    k_slc = partial_rope(k_slc.reshape(B, S, G, dk), positions)
    v_slc = v_slc.reshape(B, S, G, dk)
    k_win = partial_rope(k_win.reshape(B, S, G, dk), positions)
    v_win = v_win.reshape(B, S, G, dk)
    kc = compress_blocks(k_cmp.reshape(B, S, G, dk), pe_k, w1_k, w2_k)
    vc = compress_blocks(v_cmp.reshape(B, S, G, dk), pe_v, w1_v, w2_v)
    n_cmp = kc.shape[1]
    cmp_end = jnp.arange(n_cmp) * CMP_STRIDE + CMP_BLOCK - 1
    n_sel = S // SEL_BLOCK
    k_top = min(N_SELECT, n_sel)
    kb = k_slc.reshape(B, n_sel, SEL_BLOCK, G, dk).transpose(0, 3, 1, 2, 4)
    vb = v_slc.reshape(B, n_sel, SEL_BLOCK, G, dk).transpose(0, 3, 1, 2, 4)
    kw = jnp.pad(k_win, ((0, 0), (WINDOW, 0), (0, 0), (0, 0)))
    vw = jnp.pad(v_win, ((0, 0), (WINDOW, 0), (0, 0), (0, 0)))
    g = jax.nn.sigmoid(gates.astype(jnp.float32)).reshape(B, S, H, 3)
    agg_w = [float(c) for c in np.convolve(np.ones(SEL_RATIO), np.ones(CMP_OVERLAP))]
    pad_front = CMP_OVERLAP - 1
    pad_back = SEL_RATIO * n_sel + len(agg_w) - 1 - SEL_RATIO - n_cmp - pad_front + 1
    b_idx = jnp.arange(B)[:, None, None, None]
    g_idx = jnp.arange(G)[None, :, None, None]
    blk_j = jnp.arange(n_sel)

    def one_block(qb):
        s0 = qb * Q_BLOCK
        t = s0 + jnp.arange(Q_BLOCK)
        qr = lax.dynamic_slice_in_dim(q_raw, s0, Q_BLOCK, axis=1).reshape(B, Q_BLOCK, G, HPG, dk)
        qs = lax.dynamic_slice_in_dim(q_rot, s0, Q_BLOCK, axis=1).reshape(B, Q_BLOCK, G, HPG, dk)
        gb = lax.dynamic_slice_in_dim(g, s0, Q_BLOCK, axis=1)
        sc = jnp.einsum('bqghd,bngd->bghqn', qr, kc).astype(jnp.float32) * SCALE
        valid_c = cmp_end[None, :] <= t[:, None]
        pc = jax.nn.softmax(jnp.where(valid_c, sc, NEG_INF), axis=-1) * valid_c.astype(jnp.float32)
        o_cmp = jnp.einsum('bghqn,bngd->bqghd', pc.astype(vc.dtype), vc)
        imp = jnp.pad(pc.sum(axis=2), ((0, 0), (0, 0), (0, 0), (pad_front, pad_back)))
        p_slc = agg_w[0] * imp[..., 0:SEL_RATIO * n_sel:SEL_RATIO]
        for o in range(1, len(agg_w)):
            p_slc = p_slc + agg_w[o] * imp[..., o:o + SEL_RATIO * n_sel:SEL_RATIO]
        cur = (t // SEL_BLOCK)[:, None]
        valid_b = blk_j[None, :] * SEL_BLOCK <= t[:, None]
        forced = (blk_j[None, :] == 0) | (blk_j[None, :] == cur) | (blk_j[None, :] == cur - 1)
        score = jnp.where(valid_b, jnp.where(forced, FORCE_SCORE, p_slc), NEG_INF)
        _, idx = lax.top_k(score, k_top)
        kg = kb[b_idx, g_idx, idx].reshape(B, G, Q_BLOCK, k_top * SEL_BLOCK, dk)
        vg = vb[b_idx, g_idx, idx].reshape(B, G, Q_BLOCK, k_top * SEL_BLOCK, dk)
        tok = idx[..., None] * SEL_BLOCK + jnp.arange(SEL_BLOCK)
        mask_s = (tok <= t[None, None, :, None, None]).reshape(B, G, Q_BLOCK, k_top * SEL_BLOCK)
        ss = jnp.einsum('bqghd,bgqkd->bghqk', qs, kg).astype(jnp.float32) * SCALE
        ps = jax.nn.softmax(jnp.where(mask_s[:, :, None], ss, NEG_INF), axis=-1)
        o_slc = jnp.einsum('bghqk,bgqkd->bqghd', ps.astype(vg.dtype), vg)
        kwb = lax.dynamic_slice_in_dim(kw, s0, WINDOW + Q_BLOCK, axis=1)
        vwb = lax.dynamic_slice_in_dim(vw, s0, WINDOW + Q_BLOCK, axis=1)
        kpos = s0 - WINDOW + jnp.arange(WINDOW + Q_BLOCK)
        diff = t[:, None] - kpos[None, :]
        mask_w = (diff >= 0) & (diff < WINDOW) & (kpos[None, :] >= 0)
        sw = jnp.einsum('bqghd,bkgd->bghqk', qs, kwb).astype(jnp.float32) * SCALE
        pw = jax.nn.softmax(jnp.where(mask_w, sw, NEG_INF), axis=-1)
        o_win = jnp.einsum('bghqk,bkgd->bqghd', pw.astype(vwb.dtype), vwb)
        o = (gb[..., 0:1] * o_cmp.reshape(B, Q_BLOCK, H, dk).astype(jnp.float32)
             + gb[..., 1:2] * o_slc.reshape(B, Q_BLOCK, H, dk).astype(jnp.float32)
             + gb[..., 2:3] * o_win.reshape(B, Q_BLOCK, H, dk).astype(jnp.float32))
        return o.astype(q.dtype).reshape(B, Q_BLOCK, H * dk)

    out = lax.map(one_block, jnp.arange(S // Q_BLOCK))
    return out.transpose(1, 0, 2, 3).reshape(B, S, D_ATTN)


def short_conv_group(bg, cg, xc, w_conv):
    u = cg * xc
    y = lax.conv_general_dilated(u, w_conv[:, None, :].astype(u.dtype), window_strides=(1,),
                                 padding=[(CONV_WIDTH - 1, 0)],
                                 dimension_numbers=('NWC', 'WIO', 'NWC'),
                                 feature_group_count=D_CONV)
    return bg * y


def hybrid_mixer(h, positions, w_in, pe_k, w1_k, w2_k, pe_v, w1_v, w2_v,
                 w_conv, attn_out_norm, conv_out_norm, w_out):
    z = h @ w_in
    sizes = [D_ATTN] + [D_KV] * 6 + [N_GATES] + [D_CONV] * 3
    cuts = [int(c) for c in np.cumsum(sizes)[:-1]]
    q, kc, vc, ks, vs, kw, vw, gates, bg, cg, xc = jnp.split(z, cuts, axis=-1)
    attn = nsa_group(q, kc, vc, ks, vs, kw, vw, gates, positions, pe_k, w1_k, w2_k, pe_v, w1_v, w2_v)
    conv = short_conv_group(bg, cg, xc, w_conv)
    merged = jnp.concatenate([rms_norm(attn, attn_out_norm), rms_norm(conv, conv_out_norm)], axis=-1)
    return merged @ w_out


def setup_inputs(seed: int = 0) -> dict:
    key = jax.random.key(seed)
    ks = jax.random.split(key, 26)
    L = DEPTH

    def nrm(k, shape, scale):
        return jax.random.normal(k, shape, jnp.float32) * scale

    def gain(k, n):
        return 1.0 + 0.05 * jax.random.normal(k, (L, n), jnp.float32)

    x = nrm(ks[0], (BATCH, SEQ, D_MODEL), 1.0)
    positions = (jnp.arange(SEQ, dtype=jnp.int32)[None, :]
                 + jax.random.randint(ks[1], (BATCH, 1), 0, MAX_POS_OFFSET, dtype=jnp.int32))
    return {
        "x": x,
        "positions": positions,
        "ffn1_norm_pre": gain(ks[2], D_MODEL),
        "ffn1_w_gate": nrm(ks[3], (L, D_MODEL, D_FF), D_MODEL ** -0.5),
        "ffn1_w_up": nrm(ks[4], (L, D_MODEL, D_FF), D_MODEL ** -0.5),
        "ffn1_w_down": nrm(ks[5], (L, D_FF, D_MODEL), D_FF ** -0.5),
        "ffn1_norm_post": gain(ks[6], D_MODEL),
        "mix_norm_pre": gain(ks[7], D_MODEL),
        "w_in": nrm(ks[8], (L, D_MODEL, D_IN_PROJ), D_MODEL ** -0.5),
        "cmp_pe_k": nrm(ks[9], (L, CMP_BLOCK, HEAD_DIM), 0.02),
        "cmp_w1_k": nrm(ks[10], (L, CMP_BLOCK * HEAD_DIM, CMP_HIDDEN), (CMP_BLOCK * HEAD_DIM) ** -0.5),
        "cmp_w2_k": nrm(ks[11], (L, CMP_HIDDEN, HEAD_DIM), CMP_HIDDEN ** -0.5),
        "cmp_pe_v": nrm(ks[12], (L, CMP_BLOCK, HEAD_DIM), 0.02),
        "cmp_w1_v": nrm(ks[13], (L, CMP_BLOCK * HEAD_DIM, CMP_HIDDEN), (CMP_BLOCK * HEAD_DIM) ** -0.5),
        "cmp_w2_v": nrm(ks[14], (L, CMP_HIDDEN, HEAD_DIM), CMP_HIDDEN ** -0.5),
        "conv_w": nrm(ks[15], (L, CONV_WIDTH, D_CONV), CONV_WIDTH ** -0.5),
        "attn_out_norm": gain(ks[16], D_ATTN),
        "conv_out_norm": gain(ks[17], D_CONV),
        "w_out": nrm(ks[18], (L, D_MODEL, D_MODEL), D_MODEL ** -0.5),
        "mix_norm_post": gain(ks[19], D_MODEL),
        "ffn2_norm_pre": gain(ks[20], D_MODEL),
        "ffn2_w_gate": nrm(ks[21], (L, D_MODEL, D_FF), D_MODEL ** -0.5),
        "ffn2_w_up": nrm(ks[22], (L, D_MODEL, D_FF), D_MODEL ** -0.5),
        "ffn2_w_down": nrm(ks[23], (L, D_FF, D_MODEL), D_FF ** -0.5),
        "ffn2_norm_post": gain(ks[24], D_MODEL),
    }


def reference(x, positions, ffn1_norm_pre, ffn1_w_gate, ffn1_w_up, ffn1_w_down, ffn1_norm_post,
              mix_norm_pre, w_in, cmp_pe_k, cmp_w1_k, cmp_w2_k, cmp_pe_v, cmp_w1_v, cmp_w2_v,
              conv_w, attn_out_norm, conv_out_norm, w_out, mix_norm_post,
              ffn2_norm_pre, ffn2_w_gate, ffn2_w_up, ffn2_w_down, ffn2_norm_post):
    for l in range(DEPTH):
        h = swiglu(rms_norm(x, ffn1_norm_pre[l]), ffn1_w_gate[l], ffn1_w_up[l], ffn1_w_down[l])
        x = x + 0.5 * rms_norm(h, ffn1_norm_post[l])
        h = hybrid_mixer(rms_norm(x, mix_norm_pre[l]), positions, w_in[l],
                         cmp_pe_k[l], cmp_w1_k[l], cmp_w2_k[l], cmp_pe_v[l], cmp_w1_v[l], cmp_w2_v[l],
                         conv_w[l], attn_out_norm[l], conv_out_norm[l], w_out[l])
        x = x + rms_norm(h, mix_norm_post[l])
        h = swiglu(rms_norm(x, ffn2_norm_pre[l]), ffn2_w_gate[l], ffn2_w_up[l], ffn2_w_down[l])
        x = x + 0.5 * rms_norm(h, ffn2_norm_post[l])
    return x
```

```python
import functools
import math

import numpy as np
import jax
import jax.numpy as jnp
from jax import lax
from jax.experimental import pallas as pl
from jax.experimental.pallas import tpu as pltpu

D_MODEL = 1024
N_HEADS = 8
HEAD_DIM = 64
N_GROUPS = 2
HPG = N_HEADS // N_GROUPS
D_ATTN = N_HEADS * HEAD_DIM
D_KV = N_GROUPS * HEAD_DIM
D_CONV = D_MODEL - D_ATTN
CONV_WIDTH = 3
CMP_BLOCK = 32
CMP_STRIDE = 16
CMP_HIDDEN = 256
SEL_BLOCK = 64
SEL_SHIFT = 6
SEL_RATIO = SEL_BLOCK // CMP_STRIDE
N_SELECT = 16
WINDOW = 512
ROPE_THETA = 500000.0
ROPE_DIM = HEAD_DIM // 4
ROPE_HALF = ROPE_DIM // 2
D_FF = 2816
N_GATES = 3 * N_HEADS
EPS = 1e-6
NEG_INF = -1e30
FORCE_SCORE = 1e9
SCALE = 1.0 / math.sqrt(HEAD_DIM)

LANES = 128
GATE_COLS = N_GROUPS * LANES
VMEM_LIMIT = 56 * 1024 * 1024
FF_CHUNKS = ((0, 768), (768, 1536), (1536, 2304), (2304, 2816))

F32 = jnp.float32
BF16 = jnp.bfloat16


def _const_spec(shape):
    nd = len(shape)
    return pl.BlockSpec(shape, lambda *_: (0,) * nd, pipeline_mode=pl.Buffered(1))


def _rms(x, g):
    ms = jnp.mean(x * x, axis=-1, keepdims=True)
    return x * lax.rsqrt(ms + EPS) * g


def _ffn(x, g_pre, wg_ref, wu_ref, wd_ref, g_post):
    h = _rms(x, g_pre).astype(BF16)
    d = None
    for c0, c1 in FF_CHUNKS:
        gate = jnp.dot(h, wg_ref[:, c0:c1], preferred_element_type=F32)
        up = jnp.dot(h, wu_ref[:, c0:c1], preferred_element_type=F32)
        a = (gate * jax.nn.sigmoid(gate) * up).astype(BF16)
        part = jnp.dot(a, wd_ref[c0:c1, :], preferred_element_type=F32)
        d = part if d is None else d + part
    return x + 0.5 * _rms(d, g_post)


def _ffn_kernel(x_ref, gpre_ref, wg_ref, wu_ref, wd_ref, gpost_ref, o_ref):
    o_ref[...] = _ffn(x_ref[...], gpre_ref[...], wg_ref, wu_ref, wd_ref, gpost_ref[...])


def _ffn_call(x2d, g_pre, wg, wu, wd, g_post, tm):
    t = x2d.shape[0]
    row = pl.BlockSpec((tm, D_MODEL), lambda i: (i, 0))
    return pl.pallas_call(
        _ffn_kernel,
        grid=(t // tm,),
        in_specs=[row, _const_spec((1, D_MODEL)), _const_spec(wg.shape), _const_spec(wu.shape),
                  _const_spec(wd.shape), _const_spec((1, D_MODEL))],
        out_specs=row,
        out_shape=jax.ShapeDtypeStruct(x2d.shape, F32),
        compiler_params=pltpu.CompilerParams(dimension_semantics=("arbitrary",),
                                             vmem_limit_bytes=VMEM_LIMIT),
        name="ffn1",
    )(x2d, g_pre, wg, wu, wd, g_post)


_C_Q = 0
_C_KC = _C_Q + D_ATTN
_C_VC = _C_KC + D_KV
_C_KS = _C_VC + D_KV
_C_VS = _C_KS + D_KV
_C_KW = _C_VS + D_KV
_C_VW = _C_KW + D_KV
_C_GATE = _C_VW + D_KV
_C_BG = _C_GATE + GATE_COLS
_C_CG = _C_BG + D_CONV
_C_XC = _C_CG + D_CONV
_C_END = _C_XC + D_CONV


def _inproj_kernel(nbh, x_ref, pos_ref, g_ref, w_ref, invf_ref, convw_ref, convg_ref,
                   qraw_ref, qrot_ref, kc_ref, vc_ref, ksaug_ref, vs_ref, kw_ref, vw_ref,
                   gates_ref, convn_ref, ubuf_ref):
    i = pl.program_id(1)
    tm = x_ref.shape[1]
    h = _rms(x_ref[0], g_ref[...]).astype(BF16)

    def proj(c0, c1):
        return jnp.dot(h, w_ref[:, c0:c1], preferred_element_type=F32)

    ang = pos_ref[0].astype(F32) * invf_ref[...]
    cos = jnp.cos(ang)
    sin = jnp.sin(ang)
    lane = lax.broadcasted_iota(jnp.int32, (1, LANES), 1) & (HEAD_DIM - 1)
    s_lo = jnp.where(lane < ROPE_HALF, -sin, 0.0)
    s_hi = jnp.where(lane >= ROPE_HALF, sin, 0.0)

    def rope(z):
        return (z * cos + pltpu.roll(z, LANES - ROPE_HALF, 1) * s_lo
                + pltpu.roll(z, ROPE_HALF, 1) * s_hi)

    for c in range(D_ATTN // LANES):
        zq = proj(_C_Q + c * LANES, _C_Q + (c + 1) * LANES)
        qraw_ref[0, :, c * LANES:(c + 1) * LANES] = zq.astype(BF16)
        qrot_ref[0, :, c * LANES:(c + 1) * LANES] = rope(zq).astype(BF16)

    kc_ref[0] = proj(_C_KC, _C_VC)
    vc_ref[0] = proj(_C_VC, _C_KS)

    ks = rope(proj(_C_KS, _C_VS)).astype(BF16)
    vs = proj(_C_VS, _C_KW).astype(BF16)
    kw = rope(proj(_C_KW, _C_VW)).astype(BF16)
    vw = proj(_C_VW, _C_GATE).astype(BF16)
    row_blk = ((i * tm + lax.broadcasted_iota(jnp.int32, (tm, nbh), 0)) >> SEL_SHIFT) & (nbh - 1)
    onehot = jnp.where(row_blk == lax.broadcasted_iota(jnp.int32, (tm, nbh), 1), 1.0, 0.0).astype(BF16)
    for g in range(N_GROUPS):
        sl = slice(g * HEAD_DIM, (g + 1) * HEAD_DIM)
        ksaug_ref[0, g, :, 0:nbh] = onehot
        ksaug_ref[0, g, :, nbh:nbh + HEAD_DIM] = ks[:, sl]
        vs_ref[0, g] = vs[:, sl]
        kw_ref[0, g] = kw[:, sl]
        vw_ref[0, g] = vw[:, sl]

    gates_ref[0] = jax.nn.sigmoid(proj(_C_GATE, _C_BG))

    u = proj(_C_CG, _C_XC) * proj(_C_XC, _C_END)

    @pl.when(i == 0)
    def _():
        ubuf_ref[0:8, :] = jnp.zeros((8, D_CONV), F32)

    @pl.when(i > 0)
    def _():
        ubuf_ref[0:8, :] = ubuf_ref[tm:tm + 8, :]

    ubuf_ref[8:tm + 8, :] = u
    w = convw_ref[...]
    y = (w[2:3, :] * u + w[1:2, :] * ubuf_ref[7:tm + 7, :] + w[0:1, :] * ubuf_ref[6:tm + 6, :])
    conv = proj(_C_BG, _C_CG) * y
    convn_ref[0] = _rms(conv, convg_ref[...]).astype(BF16)


def _inproj_call(x, pos3, g_pre, w_in, invf, conv_w, conv_g, tm, nbh):
    b, s, _ = x.shape
    grid = (b, s // tm)
    row = lambda width: pl.BlockSpec((1, tm, width), lambda bi, i: (bi, i, 0))
    grp = lambda width: pl.BlockSpec((1, N_GROUPS, tm, width), lambda bi, i: (bi, 0, i, 0))
    kern = functools.partial(_inproj_kernel, nbh)
    return pl.pallas_call(
        kern,
        grid=grid,
        in_specs=[row(D_MODEL), row(1), _const_spec((1, D_MODEL)), _const_spec(w_in.shape),
                  _const_spec((1, LANES)), _const_spec((CONV_WIDTH, D_CONV)), _const_spec((1, D_CONV))],
        out_specs=[row(D_ATTN), row(D_ATTN), row(D_KV), row(D_KV), grp(nbh + HEAD_DIM), grp(HEAD_DIM),
                   grp(HEAD_DIM), grp(HEAD_DIM), row(GATE_COLS), row(D_CONV)],
        out_shape=[
            jax.ShapeDtypeStruct((b, s, D_ATTN), BF16),
            jax.ShapeDtypeStruct((b, s, D_ATTN), BF16),
            jax.ShapeDtypeStruct((b, s, D_KV), F32),
            jax.ShapeDtypeStruct((b, s, D_KV), F32),
            jax.ShapeDtypeStruct((b, N_GROUPS, s, nbh + HEAD_DIM), BF16),
            jax.ShapeDtypeStruct((b, N_GROUPS, s, HEAD_DIM), BF16),
            jax.ShapeDtypeStruct((b, N_GROUPS, s, HEAD_DIM), BF16),
            jax.ShapeDtypeStruct((b, N_GROUPS, s, HEAD_DIM), BF16),
            jax.ShapeDtypeStruct((b, s, GATE_COLS), F32),
            jax.ShapeDtypeStruct((b, s, D_CONV), BF16),
        ],
        scratch_shapes=[pltpu.VMEM((tm + 8, D_CONV), F32)],
        compiler_params=pltpu.CompilerParams(dimension_semantics=("arbitrary", "arbitrary"),
                                             vmem_limit_bytes=VMEM_LIMIT),
        name="inproj",
    )(x, pos3, g_pre, w_in, invf, conv_w, conv_g)


def _compress_kernel(kc_ref, kcn_ref, vc_ref, vcn_ref, pek_ref, w1k_ref, w2k_ref,
                     pev_ref, w1v_ref, w2v_ref, ko_ref, vo_ref):
    tc = kc_ref.shape[1]
    hid_w = N_GROUPS * CMP_HIDDEN
    last = lax.broadcasted_iota(jnp.int32, (tc, 1), 0) == tc - 1

    def one(x_ref, xn_ref, pe_ref, w1_ref, w2_ref, o_ref):
        x = x_ref[0]
        top = jnp.dot((x + pe_ref[0:1, :]).astype(BF16), w1_ref[0], preferred_element_type=F32)
        xb = (x + pe_ref[1:2, :]).astype(BF16)
        bot = jnp.dot(xb, w1_ref[1], preferred_element_type=F32)
        xnb = (xn_ref[0] + pe_ref[1:2, :]).astype(BF16)
        botn = jnp.dot(xnb, w1_ref[1], preferred_element_type=F32)
        shifted = jnp.where(last, botn[0:1, :], pltpu.roll(bot, tc - 1, 0))
        hid = jax.nn.gelu(top + shifted).astype(BF16)
        out = jnp.dot(hid, w2_ref[...], preferred_element_type=F32)
        for g in range(N_GROUPS):
            o_ref[0, g] = out[:, g * HEAD_DIM:(g + 1) * HEAD_DIM].astype(BF16)

    one(kc_ref, kcn_ref, pek_ref, w1k_ref, w2k_ref, ko_ref)
    one(vc_ref, vcn_ref, pev_ref, w1v_ref, w2v_ref, vo_ref)


def _compress_call(kc_in, vc_in, pek, w1k, w2k, pev, w1v, w2v, tc):
    b, nch, width = kc_in.shape
    nt = nch // tc
    last8 = nch // 8 - 1
    cur = pl.BlockSpec((1, tc, width), lambda bi, i: (bi, i, 0))
    nxt = pl.BlockSpec((1, 8, width), lambda bi, i: (bi, jnp.minimum((i + 1) * (tc // 8), last8), 0))
    out = pl.BlockSpec((1, N_GROUPS, tc, HEAD_DIM), lambda bi, i: (bi, 0, i, 0))
    oshape = jax.ShapeDtypeStruct((b, N_GROUPS, nch, HEAD_DIM), BF16)
    return pl.pallas_call(
        _compress_kernel,
        grid=(b, nt),
        in_specs=[cur, nxt, cur, nxt, _const_spec(pek.shape), _const_spec(w1k.shape), _const_spec(w2k.shape),
                  _const_spec(pev.shape), _const_spec(w1v.shape), _const_spec(w2v.shape)],
        out_specs=[out, out],
        out_shape=[oshape, oshape],
        compiler_params=pltpu.CompilerParams(dimension_semantics=("arbitrary", "arbitrary"),
                                             vmem_limit_bytes=VMEM_LIMIT),
        name="compress",
    )(kc_in, kc_in, vc_in, vc_in, pek, w1k, w2k, pev, w1v, w2v)


def _split3(x):
    hi = x.astype(BF16)
    r = x - hi.astype(F32)
    mid = r.astype(BF16)
    lo = (r - mid.astype(F32)).astype(BF16)
    return hi, mid, lo


def _cmp_kernel(k_top, q_ref, kc_ref, vc_ref, gates_ref, agg_ref, o_ref, bias_ref):
    qt = pl.program_id(2)
    tq = q_ref.shape[1]
    nch = kc_ref.shape[2]
    nsel = bias_ref.shape[3]
    t = qt * tq + lax.broadcasted_iota(jnp.int32, (tq, 1), 0)
    n_idx = lax.broadcasted_iota(jnp.int32, (1, nch), 1)
    valid_c = (n_idx * CMP_STRIDE + (CMP_BLOCK - 1) <= t) & (n_idx < nch - 1)
    kc = kc_ref[0, 0]
    vc = vc_ref[0, 0]
    q = q_ref[0]
    gates = gates_ref[0]
    imp = jnp.zeros((tq, nch), F32)
    outs = []
    for h in range(HPG):
        qh = q[:, h * HEAD_DIM:(h + 1) * HEAD_DIM]
        s = lax.dot_general(qh, kc, (((1,), (1,)), ((), ())), preferred_element_type=F32) * SCALE
        s = jnp.where(valid_c, s, NEG_INF)
        m = jnp.max(s, axis=-1, keepdims=True)
        e = jnp.where(valid_c, jnp.exp(s - m), 0.0)
        l = jnp.sum(e, axis=-1, keepdims=True)
        p = e * jnp.where(l > 0.0, 1.0 / l, 0.0)
        imp = imp + p
        oh = jnp.dot(p.astype(BF16), vc, preferred_element_type=F32)
        outs.append(oh * gates[:, h:h + 1])
    o_ref[0] = jnp.concatenate(outs, axis=-1)

    agg = agg_ref[...]
    p_slc = sum(jnp.dot(part, agg, preferred_element_type=F32) for part in _split3(imp))

    j_idx = lax.broadcasted_iota(jnp.int32, (1, nsel), 1)
    cur = t >> SEL_SHIFT
    valid_b = j_idx <= cur
    forced = (j_idx == 0) | (j_idx == cur) | (j_idx == cur - 1)
    score = jnp.where(valid_b, jnp.where(forced, FORCE_SCORE, p_slc), NEG_INF)
    j_f = j_idx.astype(F32)
    bias = jnp.full((tq, nsel), NEG_INF, F32)
    for _ in range(k_top):
        m = jnp.max(score, axis=-1, keepdims=True)
        first = jnp.min(jnp.where(score == m, j_f, float(nsel)), axis=-1, keepdims=True)
        hit = j_f == first
        bias = jnp.where(hit, 0.0, bias)
        score = jnp.where(hit, -jnp.inf, score)
    bias_ref[0, 0] = bias.astype(BF16)


def _cmp_call(q_raw, kcmp, vcmp, gates, agg, tq):
    b, s, _ = q_raw.shape
    nch = kcmp.shape[2]
    nsel = agg.shape[1]
    k_top = min(N_SELECT, nsel)
    gw = HPG * HEAD_DIM
    qspec = pl.BlockSpec((1, tq, gw), lambda bi, g, i: (bi, i, g))
    kvspec = pl.BlockSpec((1, 1, nch, HEAD_DIM), lambda bi, g, i: (bi, g, 0, 0))
    gspec = pl.BlockSpec((1, tq, LANES), lambda bi, g, i: (bi, i, g))
    return pl.pallas_call(
        functools.partial(_cmp_kernel, k_top),
        grid=(b, N_GROUPS, s // tq),
        in_specs=[qspec, kvspec, kvspec, gspec, _const_spec(agg.shape)],
        out_specs=[qspec, pl.BlockSpec((1, 1, tq, nsel), lambda bi, g, i: (bi, g, i, 0))],
        out_shape=[jax.ShapeDtypeStruct((b, s, D_ATTN), F32),
                   jax.ShapeDtypeStruct((b, N_GROUPS, s, nsel), BF16)],
        compiler_params=pltpu.CompilerParams(dimension_semantics=("arbitrary",) * 3,
                                             vmem_limit_bytes=VMEM_LIMIT),
        name="cmp_attn_topk",
    )(q_raw, kcmp, vcmp, gates, agg)


def _slc_kernel(nbh, q_ref, bias_ref, kaug_ref, v_ref, gates_ref, o_ref, qaug_ref):
    qt = pl.program_id(2)
    tq = q_ref.shape[1]
    tk = tq
    m_rows = HPG * tq
    n_half = qaug_ref.shape[0]
    tiles_per_half = nbh * SEL_BLOCK // tk

    q = q_ref[0]
    bias = bias_ref[0, 0]
    for hf in range(n_half):
        for h in range(HPG):
            rows = slice(h * tq, (h + 1) * tq)
            qaug_ref[hf, rows, 0:nbh] = bias[:, hf * nbh:(hf + 1) * nbh]
            qaug_ref[hf, rows, nbh:nbh + HEAD_DIM] = q[:, h * HEAD_DIM:(h + 1) * HEAD_DIM] * SCALE

    def tile(kt, carry, causal):
        m, l, acc = carry
        start = pl.multiple_of(kt * tk, tk)
        ka = kaug_ref[0, 0, pl.ds(start, tk), :]
        v = v_ref[0, 0, pl.ds(start, tk), :]
        qa = qaug_ref[kt // tiles_per_half]
        s = lax.dot_general(qa, ka, (((1,), (1,)), ((), ())), preferred_element_type=F32)
        if causal:
            r = lax.broadcasted_iota(jnp.int32, (m_rows, tk), 0) & (tq - 1)
            c = lax.broadcasted_iota(jnp.int32, (m_rows, tk), 1)
            s = jnp.where(c <= r, s, NEG_INF)
        m_new = jnp.maximum(m, jnp.max(s, axis=-1, keepdims=True))
        alpha = jnp.exp(m - m_new)
        p = jnp.exp(s - m_new)
        l = alpha * l + jnp.sum(p, axis=-1, keepdims=True)
        acc = alpha * acc + jnp.dot(p.astype(BF16), v, preferred_element_type=F32)
        return m_new, l, acc

    init = (jnp.full((m_rows, 1), NEG_INF, F32), jnp.zeros((m_rows, 1), F32),
            jnp.zeros((m_rows, HEAD_DIM), F32))
    carry = lax.fori_loop(0, qt, lambda kt, c: tile(kt, c, False), init)
    _, l, acc = tile(qt, carry, True)
    o = acc / l
    gates = gates_ref[0]
    o_ref[0] = jnp.concatenate(
        [o[h * tq:(h + 1) * tq] * gates[:, HPG + h:HPG + h + 1] for h in range(HPG)], axis=-1)


def _slc_call(q_rot, bias, kaug, v, gates, tq, nbh):
    b, s, _ = q_rot.shape
    nsel = bias.shape[3]
    gw = HPG * HEAD_DIM
    aug = nbh + HEAD_DIM
    qspec = pl.BlockSpec((1, tq, gw), lambda bi, g, i: (bi, i, g))
    return pl.pallas_call(
        functools.partial(_slc_kernel, nbh),
        grid=(b, N_GROUPS, s // tq),
        in_specs=[qspec,
                  pl.BlockSpec((1, 1, tq, nsel), lambda bi, g, i: (bi, g, i, 0)),
                  pl.BlockSpec((1, 1, s, aug), lambda bi, g, i: (bi, g, 0, 0)),
                  pl.BlockSpec((1, 1, s, HEAD_DIM), lambda bi, g, i: (bi, g, 0, 0)),
                  pl.BlockSpec((1, tq, LANES), lambda bi, g, i: (bi, i, g))],
        out_specs=qspec,
        out_shape=jax.ShapeDtypeStruct((b, s, D_ATTN), F32),
        scratch_shapes=[pltpu.VMEM((nsel // nbh, HPG * tq, aug), BF16)],
        compiler_params=pltpu.CompilerParams(dimension_semantics=("arbitrary",) * 3,
                                             vmem_limit_bytes=VMEM_LIMIT),
        name="slc_attn",
    )(q_rot, bias, kaug, v, gates)


def _win_kernel(q_ref, k_ref, v_ref, gates_ref, o_ref):
    qt = pl.program_id(2)
    tq = q_ref.shape[1]
    span = WINDOW + tq
    m_rows = HPG * tq
    t0 = qt * tq
    start = pl.multiple_of(jnp.maximum(t0 - WINDOW, 0), tq)
    k = k_ref[0, 0, pl.ds(start, span), :]
    v = v_ref[0, 0, pl.ds(start, span), :]
    q = q_ref[0]
    qs = jnp.concatenate([q[:, h * HEAD_DIM:(h + 1) * HEAD_DIM] for h in range(HPG)], axis=0) * SCALE
    s = lax.dot_general(qs, k, (((1,), (1,)), ((), ())), preferred_element_type=F32)
    t = t0 + (lax.broadcasted_iota(jnp.int32, (m_rows, span), 0) & (tq - 1))
    diff = t - (start + lax.broadcasted_iota(jnp.int32, (m_rows, span), 1))
    s = jnp.where((diff >= 0) & (diff < WINDOW), s, NEG_INF)
    m = jnp.max(s, axis=-1, keepdims=True)
    p = jnp.exp(s - m)
    l = jnp.sum(p, axis=-1, keepdims=True)
    o = jnp.dot(p.astype(BF16), v, preferred_element_type=F32) / l
    gates = gates_ref[0]
    o_ref[0] = jnp.concatenate(
        [o[h * tq:(h + 1) * tq] * gates[:, 2 * HPG + h:2 * HPG + h + 1] for h in range(HPG)], axis=-1)


def _win_call(q_rot, k, v, gates, tq):
    b, s, _ = q_rot.shape
    gw = HPG * HEAD_DIM
    qspec = pl.BlockSpec((1, tq, gw), lambda bi, g, i: (bi, i, g))
    kvspec = pl.BlockSpec((1, 1, s, HEAD_DIM), lambda bi, g, i: (bi, g, 0, 0))
    return pl.pallas_call(
        _win_kernel,
        grid=(b, N_GROUPS, s // tq),
        in_specs=[qspec, kvspec, kvspec, pl.BlockSpec((1, tq, LANES), lambda bi, g, i: (bi, i, g))],
        out_specs=qspec,
        out_shape=jax.ShapeDtypeStruct((b, s, D_ATTN), F32),
        compiler_params=pltpu.CompilerParams(dimension_semantics=("arbitrary",) * 3,
                                             vmem_limit_bytes=VMEM_LIMIT),
        name="win_attn",
    )(q_rot, k, v, gates)


def _out_kernel(x_ref, oc_ref, os_ref, ow_ref, convn_ref, ga_ref, wout_ref, gpost_ref,
                gpre2_ref, wg_ref, wu_ref, wd_ref, gpost2_ref, o_ref):
    attn = oc_ref[...] + os_ref[...] + ow_ref[...]
    an = _rms(attn, ga_ref[...]).astype(BF16)
    h = (jnp.dot(an, wout_ref[0:D_ATTN, :], preferred_element_type=F32)
         + jnp.dot(convn_ref[...], wout_ref[D_ATTN:D_MODEL, :], preferred_element_type=F32))
    x1 = x_ref[...] + _rms(h, gpost_ref[...])
    o_ref[...] = _ffn(x1, gpre2_ref[...], wg_ref, wu_ref, wd_ref, gpost2_ref[...])


def _out_call(x2d, oc, osl, ow, convn, ga, wout, gpost, gpre2, wg, wu, wd, gpost2, tm):
    t = x2d.shape[0]
    row = lambda width: pl.BlockSpec((tm, width), lambda i: (i, 0))
    return pl.pallas_call(
        _out_kernel,
        grid=(t // tm,),
        in_specs=[row(D_MODEL), row(D_ATTN), row(D_ATTN), row(D_ATTN), row(D_CONV),
                  _const_spec((1, D_ATTN)), _const_spec(wout.shape), _const_spec((1, D_MODEL)),
                  _const_spec((1, D_MODEL)), _const_spec(wg.shape), _const_spec(wu.shape),
                  _const_spec(wd.shape), _const_spec((1, D_MODEL))],
        out_specs=row(D_MODEL),
        out_shape=jax.ShapeDtypeStruct(x2d.shape, F32),
        compiler_params=pltpu.CompilerParams(dimension_semantics=("arbitrary",),
                                             vmem_limit_bytes=VMEM_LIMIT),
        name="outproj_ffn2",
    )(x2d, oc, osl, ow, convn, ga, wout, gpost, gpre2, wg, wu, wd, gpost2)


def _prep_w_in(w_in):
    sizes = [D_ATTN] + [D_KV] * 6 + [N_GATES] + [D_CONV] * 3
    cuts = np.cumsum([0] + sizes)
    q, kc, vc, ks, vs, kw, vw, gt, bg, cg, xc = [w_in[:, cuts[i]:cuts[i + 1]] for i in range(len(sizes))]
    gt = gt.reshape(D_MODEL, N_GROUPS, HPG, 3).transpose(0, 1, 3, 2).reshape(D_MODEL, N_GROUPS, 3 * HPG)
    gt = jnp.pad(gt, ((0, 0), (0, 0), (0, LANES - 3 * HPG))).reshape(D_MODEL, GATE_COLS)
    return jnp.concatenate([q, kc, vc, ks, vs, kw, vw, gt, bg, cg, xc], axis=1).astype(BF16)


def _prep_compress(pe, w1, w2):
    half = CMP_BLOCK // 2
    eye = jnp.eye(N_GROUPS, dtype=F32)
    w1r = w1.reshape(2, half, HEAD_DIM, CMP_HIDDEN)
    w1big = jnp.einsum("ptdc,gh->ptgdhc", w1r, eye).reshape(2, half * D_KV, N_GROUPS * CMP_HIDDEN)
    w2big = jnp.einsum("cd,gh->gchd", w2, eye).reshape(N_GROUPS * CMP_HIDDEN, D_KV)
    pe_rows = jnp.broadcast_to(pe.reshape(2, half, 1, HEAD_DIM), (2, half, N_GROUPS, HEAD_DIM))
    return pe_rows.reshape(2, half * D_KV), w1big.astype(BF16), w2big.astype(BF16)


def _agg_matrix(nch, nsel):
    agg_w = np.convolve(np.ones(SEL_RATIO), np.ones(CMP_BLOCK // CMP_STRIDE))
    a = np.zeros((nch, nsel), np.float32)
    for j in range(nsel):
        for o, wgt in enumerate(agg_w):
            c = SEL_RATIO * j + o - (CMP_BLOCK // CMP_STRIDE - 1)
            if 0 <= c < nch - 1:
                a[c, j] = wgt
    return jnp.asarray(a, BF16)


def _rope_inv_freq_row():
    inv = ROPE_THETA ** (-np.arange(ROPE_HALF, dtype=np.float32) * 2.0 / ROPE_DIM)
    lane = np.arange(LANES) % HEAD_DIM
    row = np.where(lane < ROPE_DIM, inv[lane % ROPE_HALF], 0.0).astype(np.float32)
    return jnp.asarray(row.reshape(1, LANES))


def _forward(x, positions, p, *, tm, tq, tc, nbh):
    b, s, _ = x.shape
    depth = p["w_in"].shape[0]
    nch = s // CMP_STRIDE
    nsel = s // SEL_BLOCK
    pos3 = positions.reshape(b, s, 1)
    invf = _rope_inv_freq_row()
    agg = _agg_matrix(nch, nsel)
    row = lambda v: v.reshape(1, -1)
    for l in range(depth):
        x2d = x.reshape(b * s, D_MODEL)
        x2d = _ffn_call(x2d, row(p["ffn1_norm_pre"][l]), p["ffn1_w_gate"][l].astype(BF16),
                        p["ffn1_w_up"][l].astype(BF16), p["ffn1_w_down"][l].astype(BF16),
                        row(p["ffn1_norm_post"][l]), tm)
        (q_raw, q_rot, kc_in, vc_in, ksaug, vs, kw, vw, gates, convn) = _inproj_call(
            x2d.reshape(b, s, D_MODEL), pos3, row(p["mix_norm_pre"][l]), _prep_w_in(p["w_in"][l]), invf,
            p["conv_w"][l], row(p["conv_out_norm"][l]), tm, nbh)
        pek, w1k, w2k = _prep_compress(p["cmp_pe_k"][l], p["cmp_w1_k"][l], p["cmp_w2_k"][l])
        pev, w1v, w2v = _prep_compress(p["cmp_pe_v"][l], p["cmp_w1_v"][l], p["cmp_w2_v"][l])
        kcmp, vcmp = _compress_call(kc_in.reshape(b, nch, CMP_STRIDE * D_KV),
                                    vc_in.reshape(b, nch, CMP_STRIDE * D_KV),
                                    pek, w1k, w2k, pev, w1v, w2v, tc)
        o_cmp, bias = _cmp_call(q_raw, kcmp, vcmp, gates, agg, tq)
        o_slc = _slc_call(q_rot, bias, ksaug, vs, gates, tq, nbh)
        o_win = _win_call(q_rot, kw, vw, gates, tq)
        flat = lambda a: a.reshape(b * s, a.shape[-1])
        x2d = _out_call(x2d, flat(o_cmp), flat(o_slc), flat(o_win), flat(convn),
                        row(p["attn_out_norm"][l]), p["w_out"][l].astype(BF16), row(p["mix_norm_post"][l]),
                        row(p["ffn2_norm_pre"][l]), p["ffn2_w_gate"][l].astype(BF16),
                        p["ffn2_w_up"][l].astype(BF16), p["ffn2_w_down"][l].astype(BF16),
                        row(p["ffn2_norm_post"][l]), tm)
        x = x2d.reshape(b, s, D_MODEL)
    return x


def kernel(x, positions, ffn1_norm_pre, ffn1_w_gate, ffn1_w_up, ffn1_w_down, ffn1_norm_post, mix_norm_pre, w_in, cmp_pe_k, cmp_w1_k, cmp_w2_k, cmp_pe_v, cmp_w1_v, cmp_w2_v, conv_w, attn_out_norm, conv_out_norm, w_out, mix_norm_post, ffn2_norm_pre, ffn2_w_gate, ffn2_w_up, ffn2_w_down, ffn2_norm_post):
    params = dict(
        ffn1_norm_pre=ffn1_norm_pre, ffn1_w_gate=ffn1_w_gate, ffn1_w_up=ffn1_w_up, ffn1_w_down=ffn1_w_down,
        ffn1_norm_post=ffn1_norm_post, mix_norm_pre=mix_norm_pre, w_in=w_in,
        cmp_pe_k=cmp_pe_k, cmp_w1_k=cmp_w1_k, cmp_w2_k=cmp_w2_k,
        cmp_pe_v=cmp_pe_v, cmp_w1_v=cmp_w1_v, cmp_w2_v=cmp_w2_v,
        conv_w=conv_w, attn_out_norm=attn_out_norm, conv_out_norm=conv_out_norm, w_out=w_out,
        mix_norm_post=mix_norm_post, ffn2_norm_pre=ffn2_norm_pre, ffn2_w_gate=ffn2_w_gate,
        ffn2_w_up=ffn2_w_up, ffn2_w_down=ffn2_w_down, ffn2_norm_post=ffn2_norm_post)
    return _forward(x, positions, params, tm=512, tq=256, tc=256, nbh=128)
```

```python
import functools
import math

import numpy as np
import jax
import jax.numpy as jnp
from jax import lax
from jax.experimental import pallas as pl
from jax.experimental.pallas import tpu as pltpu

D_MODEL = 1024
N_HEADS = 8
HEAD_DIM = 64
N_GROUPS = 2
HPG = N_HEADS // N_GROUPS
D_ATTN = N_HEADS * HEAD_DIM
D_KV = N_GROUPS * HEAD_DIM
D_CONV = D_MODEL - D_ATTN
CONV_WIDTH = 3
CMP_BLOCK = 32
CMP_STRIDE = 16
CMP_HIDDEN = 256
SEL_BLOCK = 64
SEL_SHIFT = 6
SEL_RATIO = SEL_BLOCK // CMP_STRIDE
N_SELECT = 16
WINDOW = 512
ROPE_THETA = 500000.0
ROPE_DIM = HEAD_DIM // 4
ROPE_HALF = ROPE_DIM // 2
D_FF = 2816
N_GATES = 3 * N_HEADS
EPS = 1e-6
NEG_INF = -1e30
FORCE_SCORE = 1e9
SCALE = 1.0 / math.sqrt(HEAD_DIM)

LANES = 128
GATE_COLS = N_GROUPS * LANES
VMEM_LIMIT = 56 * 1024 * 1024
FF_CHUNKS = ((0, 768), (768, 1536), (1536, 2304), (2304, 2816))
N_SHIFT_COLS = 2
SAFE_SCORE_BOUND = 40.0

F32 = jnp.float32
BF16 = jnp.bfloat16


def _const_spec(shape):
    nd = len(shape)
    return pl.BlockSpec(shape, lambda *_: (0,) * nd, pipeline_mode=pl.Buffered(1))


def _rms(x, g):
    ms = jnp.mean(x * x, axis=-1, keepdims=True)
    return x * lax.rsqrt(ms + EPS) * g


def _ffn(x, g_pre, wg_ref, wu_ref, wd_ref, g_post):
    h = _rms(x, g_pre).astype(BF16)
    d = None
    for c0, c1 in FF_CHUNKS:
        gate = jnp.dot(h, wg_ref[:, c0:c1], preferred_element_type=F32)
        up = jnp.dot(h, wu_ref[:, c0:c1], preferred_element_type=F32)
        a = (gate * jax.nn.sigmoid(gate) * up).astype(BF16)
        part = jnp.dot(a, wd_ref[c0:c1, :], preferred_element_type=F32)
        d = part if d is None else d + part
    return x + 0.5 * _rms(d, g_post)


def _ffn_kernel(x_ref, gpre_ref, wg_ref, wu_ref, wd_ref, gpost_ref, o_ref):
    o_ref[...] = _ffn(x_ref[...], gpre_ref[...], wg_ref, wu_ref, wd_ref, gpost_ref[...])


def _ffn_call(x2d, g_pre, wg, wu, wd, g_post, tm):
    t = x2d.shape[0]
    row = pl.BlockSpec((tm, D_MODEL), lambda i: (i, 0))
    return pl.pallas_call(
        _ffn_kernel,
        grid=(t // tm,),
        in_specs=[row, _const_spec((1, D_MODEL)), _const_spec(wg.shape), _const_spec(wu.shape),
                  _const_spec(wd.shape), _const_spec((1, D_MODEL))],
        out_specs=row,
        out_shape=jax.ShapeDtypeStruct(x2d.shape, F32),
        compiler_params=pltpu.CompilerParams(dimension_semantics=("arbitrary",),
                                             vmem_limit_bytes=VMEM_LIMIT),
        name="ffn1",
    )(x2d, g_pre, wg, wu, wd, g_post)


_C_Q = 0
_C_KC = _C_Q + D_ATTN
_C_VC = _C_KC + D_KV
_C_KS = _C_VC + D_KV
_C_VS = _C_KS + D_KV
_C_KW = _C_VS + D_KV
_C_VW = _C_KW + D_KV
_C_GATE = _C_VW + D_KV
_C_BG = _C_GATE + GATE_COLS
_C_CG = _C_BG + D_CONV
_C_XC = _C_CG + D_CONV
_C_END = _C_XC + D_CONV


def _inproj_kernel(nbh, x_ref, pos_ref, g_ref, w_ref, invf_ref, convw_ref, convg_ref,
                   qraw_ref, qrot_ref, kc_ref, vc_ref, ksaug_ref, vs_ref, kw_ref, vw_ref,
                   gates_ref, convn_ref, ubuf_ref):
    i = pl.program_id(1)
    tm = x_ref.shape[1]
    h = _rms(x_ref[0], g_ref[...]).astype(BF16)

    def proj(c0, c1):
        return jnp.dot(h, w_ref[:, c0:c1], preferred_element_type=F32)

    ang = pos_ref[0].astype(F32) * invf_ref[...]
    cos = jnp.cos(ang)
    sin = jnp.sin(ang)
    lane = lax.broadcasted_iota(jnp.int32, (1, LANES), 1) & (HEAD_DIM - 1)
    s_lo = jnp.where(lane < ROPE_HALF, -sin, 0.0)
    s_hi = jnp.where(lane >= ROPE_HALF, sin, 0.0)

    def rope(z):
        return (z * cos + pltpu.roll(z, LANES - ROPE_HALF, 1) * s_lo
                + pltpu.roll(z, ROPE_HALF, 1) * s_hi)

    for c in range(D_ATTN // LANES):
        zq = proj(_C_Q + c * LANES, _C_Q + (c + 1) * LANES)
        qraw_ref[0, :, c * LANES:(c + 1) * LANES] = zq.astype(BF16)
        qrot_ref[0, :, c * LANES:(c + 1) * LANES] = rope(zq).astype(BF16)

    kc_ref[0] = proj(_C_KC, _C_VC)
    vc_ref[0] = proj(_C_VC, _C_KS)

    ks = rope(proj(_C_KS, _C_VS)).astype(BF16)
    vs = proj(_C_VS, _C_KW).astype(BF16)
    kw = rope(proj(_C_KW, _C_VW)).astype(BF16)
    vw = proj(_C_VW, _C_GATE).astype(BF16)
    row_blk = ((i * tm + lax.broadcasted_iota(jnp.int32, (tm, nbh), 0)) >> SEL_SHIFT) & (nbh - 1)
    onehot = jnp.where(row_blk == lax.broadcasted_iota(jnp.int32, (tm, nbh), 1), 1.0, 0.0).astype(BF16)
    tail = jnp.where(lax.broadcasted_iota(jnp.int32, (tm, HEAD_DIM), 1) < N_SHIFT_COLS, 1.0, 0.0).astype(BF16)
    for g in range(N_GROUPS):
        sl = slice(g * HEAD_DIM, (g + 1) * HEAD_DIM)
        ksaug_ref[0, g, :, 0:nbh] = onehot
        ksaug_ref[0, g, :, nbh:nbh + HEAD_DIM] = ks[:, sl]
        ksaug_ref[0, g, :, nbh + HEAD_DIM:nbh + 2 * HEAD_DIM] = tail
        vs_ref[0, g, :, 0:HEAD_DIM] = vs[:, sl]
        vs_ref[0, g, :, HEAD_DIM:2 * HEAD_DIM] = tail
        kw_ref[0, g] = kw[:, sl]
        vw_ref[0, g] = vw[:, sl]

    gates_ref[0] = jax.nn.sigmoid(proj(_C_GATE, _C_BG))

    u = proj(_C_CG, _C_XC) * proj(_C_XC, _C_END)

    @pl.when(i == 0)
    def _():
        ubuf_ref[0:8, :] = jnp.zeros((8, D_CONV), F32)

    @pl.when(i > 0)
    def _():
        ubuf_ref[0:8, :] = ubuf_ref[tm:tm + 8, :]

    ubuf_ref[8:tm + 8, :] = u
    w = convw_ref[...]
    y = (w[2:3, :] * u + w[1:2, :] * ubuf_ref[7:tm + 7, :] + w[0:1, :] * ubuf_ref[6:tm + 6, :])
    conv = proj(_C_BG, _C_CG) * y
    convn_ref[0] = _rms(conv, convg_ref[...]).astype(BF16)


def _inproj_call(x, pos3, g_pre, w_in, invf, conv_w, conv_g, tm, nbh):
    b, s, _ = x.shape
    grid = (b, s // tm)
    row = lambda width: pl.BlockSpec((1, tm, width), lambda bi, i: (bi, i, 0))
    grp = lambda width: pl.BlockSpec((1, N_GROUPS, tm, width), lambda bi, i: (bi, 0, i, 0))
    kern = functools.partial(_inproj_kernel, nbh)
    return pl.pallas_call(
        kern,
        grid=grid,
        in_specs=[row(D_MODEL), row(1), _const_spec((1, D_MODEL)), _const_spec(w_in.shape),
                  _const_spec((1, LANES)), _const_spec((CONV_WIDTH, D_CONV)), _const_spec((1, D_CONV))],
        out_specs=[row(D_ATTN), row(D_ATTN), row(D_KV), row(D_KV), grp(nbh + 2 * HEAD_DIM), grp(2 * HEAD_DIM),
                   grp(HEAD_DIM), grp(HEAD_DIM), row(GATE_COLS), row(D_CONV)],
        out_shape=[
            jax.ShapeDtypeStruct((b, s, D_ATTN), BF16),
            jax.ShapeDtypeStruct((b, s, D_ATTN), BF16),
            jax.ShapeDtypeStruct((b, s, D_KV), F32),
            jax.ShapeDtypeStruct((b, s, D_KV), F32),
            jax.ShapeDtypeStruct((b, N_GROUPS, s, nbh + 2 * HEAD_DIM), BF16),
            jax.ShapeDtypeStruct((b, N_GROUPS, s, 2 * HEAD_DIM), BF16),
            jax.ShapeDtypeStruct((b, N_GROUPS, s, HEAD_DIM), BF16),
            jax.ShapeDtypeStruct((b, N_GROUPS, s, HEAD_DIM), BF16),
            jax.ShapeDtypeStruct((b, s, GATE_COLS), F32),
            jax.ShapeDtypeStruct((b, s, D_CONV), BF16),
        ],
        scratch_shapes=[pltpu.VMEM((tm + 8, D_CONV), F32)],
        compiler_params=pltpu.CompilerParams(dimension_semantics=("arbitrary", "arbitrary"),
                                             vmem_limit_bytes=VMEM_LIMIT),
        name="inproj",
    )(x, pos3, g_pre, w_in, invf, conv_w, conv_g)


def _compress_kernel(kc_ref, kcn_ref, vc_ref, vcn_ref, pek_ref, w1k_ref, w2k_ref,
                     pev_ref, w1v_ref, w2v_ref, ko_ref, vo_ref):
    tc = kc_ref.shape[1]
    hid_w = N_GROUPS * CMP_HIDDEN
    last = lax.broadcasted_iota(jnp.int32, (tc, 1), 0) == tc - 1

    def one(x_ref, xn_ref, pe_ref, w1_ref, w2_ref, o_ref):
        x = x_ref[0]
        top = jnp.dot((x + pe_ref[0:1, :]).astype(BF16), w1_ref[0], preferred_element_type=F32)
        xb = (x + pe_ref[1:2, :]).astype(BF16)
        bot = jnp.dot(xb, w1_ref[1], preferred_element_type=F32)
        xnb = (xn_ref[0] + pe_ref[1:2, :]).astype(BF16)
        botn = jnp.dot(xnb, w1_ref[1], preferred_element_type=F32)
        shifted = jnp.where(last, botn[0:1, :], pltpu.roll(bot, tc - 1, 0))
        hid = jax.nn.gelu(top + shifted).astype(BF16)
        out = jnp.dot(hid, w2_ref[...], preferred_element_type=F32)
        for g in range(N_GROUPS):
            o_ref[0, g] = out[:, g * HEAD_DIM:(g + 1) * HEAD_DIM].astype(BF16)

    one(kc_ref, kcn_ref, pek_ref, w1k_ref, w2k_ref, ko_ref)
    one(vc_ref, vcn_ref, pev_ref, w1v_ref, w2v_ref, vo_ref)


def _compress_call(kc_in, vc_in, pek, w1k, w2k, pev, w1v, w2v, tc):
    b, nch, width = kc_in.shape
    nt = nch // tc
    last8 = nch // 8 - 1
    cur = pl.BlockSpec((1, tc, width), lambda bi, i: (bi, i, 0))
    nxt = pl.BlockSpec((1, 8, width), lambda bi, i: (bi, jnp.minimum((i + 1) * (tc // 8), last8), 0))
    out = pl.BlockSpec((1, N_GROUPS, tc, HEAD_DIM), lambda bi, i: (bi, 0, i, 0))
    oshape = jax.ShapeDtypeStruct((b, N_GROUPS, nch, HEAD_DIM), BF16)
    return pl.pallas_call(
        _compress_kernel,
        grid=(b, nt),
        in_specs=[cur, nxt, cur, nxt, _const_spec(pek.shape), _const_spec(w1k.shape), _const_spec(w2k.shape),
                  _const_spec(pev.shape), _const_spec(w1v.shape), _const_spec(w2v.shape)],
        out_specs=[out, out],
        out_shape=[oshape, oshape],
        compiler_params=pltpu.CompilerParams(dimension_semantics=("arbitrary", "arbitrary"),
                                             vmem_limit_bytes=VMEM_LIMIT),
        name="compress",
    )(kc_in, kc_in, vc_in, vc_in, pek, w1k, w2k, pev, w1v, w2v)


def _split3(x):
    hi = x.astype(BF16)
    r = x - hi.astype(F32)
    mid = r.astype(BF16)
    lo = (r - mid.astype(F32)).astype(BF16)
    return hi, mid, lo


def _cmp_kernel(k_top, q_ref, kc_ref, vc_ref, gates_ref, agg_ref, o_ref, bias_ref):
    qt = pl.program_id(2)
    tq = q_ref.shape[1]
    nch = kc_ref.shape[2]
    nsel = bias_ref.shape[3]
    t = qt * tq + lax.broadcasted_iota(jnp.int32, (tq, 1), 0)
    n_idx = lax.broadcasted_iota(jnp.int32, (1, nch), 1)
    valid_c = (n_idx * CMP_STRIDE + (CMP_BLOCK - 1) <= t) & (n_idx < nch - 1)
    kc = kc_ref[0, 0]
    vc = vc_ref[0, 0]
    q = q_ref[0]
    gates = gates_ref[0]
    imp = jnp.zeros((tq, nch), F32)
    outs = []
    for h in range(HPG):
        qh = q[:, h * HEAD_DIM:(h + 1) * HEAD_DIM]
        s = lax.dot_general(qh, kc, (((1,), (1,)), ((), ())), preferred_element_type=F32) * SCALE
        s = jnp.where(valid_c, s, NEG_INF)
        m = jnp.max(s, axis=-1, keepdims=True)
        e = jnp.where(valid_c, jnp.exp(s - m), 0.0)
        l = jnp.sum(e, axis=-1, keepdims=True)
        p = e * jnp.where(l > 0.0, 1.0 / l, 0.0)
        imp = imp + p
        oh = jnp.dot(p.astype(BF16), vc, preferred_element_type=F32)
        outs.append(oh * gates[:, h:h + 1])
    o_ref[0] = jnp.concatenate(outs, axis=-1)

    agg = agg_ref[...]
    p_slc = sum(jnp.dot(part, agg, preferred_element_type=F32) for part in _split3(imp))

    j_idx = lax.broadcasted_iota(jnp.int32, (1, nsel), 1)
    cur = t >> SEL_SHIFT
    valid_b = j_idx <= cur
    forced = (j_idx == 0) | (j_idx == cur) | (j_idx == cur - 1)
    score = jnp.where(valid_b, jnp.where(forced, FORCE_SCORE, p_slc), NEG_INF)
    j_f = j_idx.astype(F32)
    bias = jnp.full((tq, nsel), NEG_INF, F32)
    for _ in range(k_top):
        m = jnp.max(score, axis=-1, keepdims=True)
        first = jnp.min(jnp.where(score == m, j_f, float(nsel)), axis=-1, keepdims=True)
        hit = j_f == first
        bias = jnp.where(hit, 0.0, bias)
        score = jnp.where(hit, -jnp.inf, score)
    bias_ref[0, 0] = bias.astype(BF16)


def _cmp_call(q_raw, kcmp, vcmp, gates, agg, tq):
    b, s, _ = q_raw.shape
    nch = kcmp.shape[2]
    nsel = agg.shape[1]
    k_top = min(N_SELECT, nsel)
    gw = HPG * HEAD_DIM
    qspec = pl.BlockSpec((1, tq, gw), lambda bi, g, i: (bi, i, g))
    kvspec = pl.BlockSpec((1, 1, nch, HEAD_DIM), lambda bi, g, i: (bi, g, 0, 0))
    gspec = pl.BlockSpec((1, tq, LANES), lambda bi, g, i: (bi, i, g))
    return pl.pallas_call(
        functools.partial(_cmp_kernel, k_top),
        grid=(b, N_GROUPS, s // tq),
        in_specs=[qspec, kvspec, kvspec, gspec, _const_spec(agg.shape)],
        out_specs=[qspec, pl.BlockSpec((1, 1, tq, nsel), lambda bi, g, i: (bi, g, i, 0))],
        out_shape=[jax.ShapeDtypeStruct((b, s, D_ATTN), F32),
                   jax.ShapeDtypeStruct((b, N_GROUPS, s, nsel), BF16)],
        compiler_params=pltpu.CompilerParams(dimension_semantics=("arbitrary",) * 3,
                                             vmem_limit_bytes=VMEM_LIMIT),
        name="cmp_attn_topk",
    )(q_raw, kcmp, vcmp, gates, agg)


def _slc_kernel(nbh, tk, q_ref, bias_ref, kaug_ref, vaug_ref, gates_ref, o_ref,
                qaug_ref, acc_ref, shift_ref, kmax_ref):
    qt = pl.program_id(2)
    tq = q_ref.shape[1]
    m_rows = HPG * tq
    t0 = qt * tq
    n_full = (t0 + tq - 1) // tk
    n_half = qaug_ref.shape[0]
    tiles_per_half = nbh * SEL_BLOCK // tk
    kcol = slice(nbh, nbh + HEAD_DIM)
    tcol = slice(nbh + HEAD_DIM, nbh + 2 * HEAD_DIM)

    @pl.when(qt == 0)
    def _():
        def body(c, mx):
            k = kaug_ref[0, 0, pl.ds(pl.multiple_of(c * tk, tk), tk), kcol].astype(F32)
            return jnp.maximum(mx, jnp.sum(k * k, axis=-1, keepdims=True))
        mx = lax.fori_loop(0, kaug_ref.shape[2] // tk, body, jnp.zeros((tk, 1), F32))
        kmax_ref[...] = jnp.broadcast_to(jnp.sqrt(jnp.max(mx, axis=0, keepdims=True)), kmax_ref.shape)

    qs = q_ref[0] * SCALE
    bias = bias_ref[0, 0]
    kmax = kmax_ref[0:1, 0:1]
    bounds = []
    for h in range(HPG):
        rows = slice(h * tq, (h + 1) * tq)
        qh = qs[:, h * HEAD_DIM:(h + 1) * HEAD_DIM]
        for hf in range(n_half):
            qaug_ref[hf, rows, 0:nbh] = bias[:, hf * nbh:(hf + 1) * nbh]
            qaug_ref[hf, rows, kcol] = qh
            qaug_ref[hf, rows, tcol] = jnp.zeros((tq, HEAD_DIM), BF16)
        qf = qh.astype(F32)
        bounds.append(jnp.sqrt(jnp.sum(qf * qf, axis=-1, keepdims=True)) * kmax)
    bound = jnp.concatenate(bounds, axis=0)
    shift_ref[...] = bound

    def scores(kt, causal):
        start = pl.multiple_of(kt * tk, tk)
        ka = kaug_ref[0, 0, pl.ds(start, tk), :]
        qa = qaug_ref[kt // tiles_per_half]
        s = lax.dot_general(qa, ka, (((1,), (1,)), ((), ())), preferred_element_type=F32)
        if causal:
            r = lax.broadcasted_iota(jnp.int32, (m_rows, tk), 0) & (tq - 1)
            c = lax.broadcasted_iota(jnp.int32, (m_rows, tk), 1)
            s = jnp.where(c - r <= t0 - kt * tk, s, NEG_INF)
        return s

    @pl.when(jnp.max(bound) > SAFE_SCORE_BOUND)
    def _():
        rowmax = lambda kt, causal: jnp.max(scores(kt, causal), axis=-1, keepdims=True)
        mx = lax.fori_loop(0, n_full, lambda kt, m: jnp.maximum(m, rowmax(kt, False)),
                           jnp.full((m_rows, 1), NEG_INF, F32))
        shift_ref[...] = jnp.maximum(mx, rowmax(n_full, True))

    shift = shift_ref[...]
    hi = shift.astype(BF16).astype(F32)
    lo = (shift - hi).astype(BF16).astype(F32)
    lane = lax.broadcasted_iota(jnp.int32, (m_rows, HEAD_DIM), 1)
    tail = jnp.where(lane == 0, -hi, jnp.where(lane == 1, -lo, 0.0)).astype(BF16)
    for hf in range(n_half):
        qaug_ref[hf, :, tcol] = tail

    acc_ref[...] = jnp.zeros(acc_ref.shape, F32)

    def tile(kt, causal):
        p = jnp.exp(scores(kt, causal)).astype(BF16)
        va = vaug_ref[0, 0, pl.ds(pl.multiple_of(kt * tk, tk), tk), :]
        acc_ref[...] += jnp.dot(p, va, preferred_element_type=F32)

    def body(kt, c):
        tile(kt, False)
        return c

    lax.fori_loop(0, n_full, body, 0)
    tile(n_full, True)
    acc = acc_ref[...]
    o = acc[:, 0:HEAD_DIM] / acc[:, HEAD_DIM:HEAD_DIM + 1]
    gates = gates_ref[0]
    o_ref[0] = jnp.concatenate(
        [o[h * tq:(h + 1) * tq] * gates[:, HPG + h:HPG + h + 1] for h in range(HPG)], axis=-1)


def _slc_call(q_rot, bias, kaug, v, gates, tq, tk, nbh):
    b, s, _ = q_rot.shape
    assert tk % tq == 0 and (nbh * SEL_BLOCK) % tk == 0 and s % tk == 0
    nsel = bias.shape[3]
    gw = HPG * HEAD_DIM
    aug = nbh + 2 * HEAD_DIM
    qspec = pl.BlockSpec((1, tq, gw), lambda bi, g, i: (bi, i, g))
    return pl.pallas_call(
        functools.partial(_slc_kernel, nbh, tk),
        grid=(b, N_GROUPS, s // tq),
        in_specs=[qspec,
                  pl.BlockSpec((1, 1, tq, nsel), lambda bi, g, i: (bi, g, i, 0)),
                  pl.BlockSpec((1, 1, s, aug), lambda bi, g, i: (bi, g, 0, 0)),
                  pl.BlockSpec((1, 1, s, 2 * HEAD_DIM), lambda bi, g, i: (bi, g, 0, 0)),
                  pl.BlockSpec((1, tq, LANES), lambda bi, g, i: (bi, i, g))],
        out_specs=qspec,
        out_shape=jax.ShapeDtypeStruct((b, s, D_ATTN), F32),
        scratch_shapes=[pltpu.VMEM((nsel // nbh, HPG * tq, aug), BF16),
                        pltpu.VMEM((HPG * tq, 2 * HEAD_DIM), F32),
                        pltpu.VMEM((HPG * tq, 1), F32),
                        pltpu.VMEM((8, LANES), F32)],
        compiler_params=pltpu.CompilerParams(dimension_semantics=("arbitrary",) * 3,
                                             vmem_limit_bytes=VMEM_LIMIT),
        name="slc_attn",
    )(q_rot, bias, kaug, v, gates)


def _win_kernel(q_ref, k_ref, v_ref, gates_ref, o_ref):
    qt = pl.program_id(2)
    tq = q_ref.shape[1]
    span = WINDOW + tq
    m_rows = HPG * tq
    t0 = qt * tq
    start = pl.multiple_of(jnp.maximum(t0 - WINDOW, 0), tq)
    k = k_ref[0, 0, pl.ds(start, span), :]
    v = v_ref[0, 0, pl.ds(start, span), :]
    q = q_ref[0]
    qs = jnp.concatenate([q[:, h * HEAD_DIM:(h + 1) * HEAD_DIM] for h in range(HPG)], axis=0) * SCALE
    s = lax.dot_general(qs, k, (((1,), (1,)), ((), ())), preferred_element_type=F32)
    t = t0 + (lax.broadcasted_iota(jnp.int32, (m_rows, span), 0) & (tq - 1))
    diff = t - (start + lax.broadcasted_iota(jnp.int32, (m_rows, span), 1))
    s = jnp.where((diff >= 0) & (diff < WINDOW), s, NEG_INF)
    m = jnp.max(s, axis=-1, keepdims=True)
    p = jnp.exp(s - m)
    l = jnp.sum(p, axis=-1, keepdims=True)
    o = jnp.dot(p.astype(BF16), v, preferred_element_type=F32) / l
    gates = gates_ref[0]
    o_ref[0] = jnp.concatenate(
        [o[h * tq:(h + 1) * tq] * gates[:, 2 * HPG + h:2 * HPG + h + 1] for h in range(HPG)], axis=-1)


def _win_call(q_rot, k, v, gates, tq):
    b, s, _ = q_rot.shape
    gw = HPG * HEAD_DIM
    qspec = pl.BlockSpec((1, tq, gw), lambda bi, g, i: (bi, i, g))
    kvspec = pl.BlockSpec((1, 1, s, HEAD_DIM), lambda bi, g, i: (bi, g, 0, 0))
    return pl.pallas_call(
        _win_kernel,
        grid=(b, N_GROUPS, s // tq),
        in_specs=[qspec, kvspec, kvspec, pl.BlockSpec((1, tq, LANES), lambda bi, g, i: (bi, i, g))],
        out_specs=qspec,
        out_shape=jax.ShapeDtypeStruct((b, s, D_ATTN), F32),
        compiler_params=pltpu.CompilerParams(dimension_semantics=("arbitrary",) * 3,
                                             vmem_limit_bytes=VMEM_LIMIT),
        name="win_attn",
    )(q_rot, k, v, gates)


def _out_kernel(x_ref, oc_ref, os_ref, ow_ref, convn_ref, ga_ref, wout_ref, gpost_ref,
                gpre2_ref, wg_ref, wu_ref, wd_ref, gpost2_ref, o_ref):
    attn = oc_ref[...] + os_ref[...] + ow_ref[...]
    an = _rms(attn, ga_ref[...]).astype(BF16)
    h = (jnp.dot(an, wout_ref[0:D_ATTN, :], preferred_element_type=F32)
         + jnp.dot(convn_ref[...], wout_ref[D_ATTN:D_MODEL, :], preferred_element_type=F32))
    x1 = x_ref[...] + _rms(h, gpost_ref[...])
    o_ref[...] = _ffn(x1, gpre2_ref[...], wg_ref, wu_ref, wd_ref, gpost2_ref[...])


def _out_call(x2d, oc, osl, ow, convn, ga, wout, gpost, gpre2, wg, wu, wd, gpost2, tm):
    t = x2d.shape[0]
    row = lambda width: pl.BlockSpec((tm, width), lambda i: (i, 0))
    return pl.pallas_call(
        _out_kernel,
        grid=(t // tm,),
        in_specs=[row(D_MODEL), row(D_ATTN), row(D_ATTN), row(D_ATTN), row(D_CONV),
                  _const_spec((1, D_ATTN)), _const_spec(wout.shape), _const_spec((1, D_MODEL)),
                  _const_spec((1, D_MODEL)), _const_spec(wg.shape), _const_spec(wu.shape),
                  _const_spec(wd.shape), _const_spec((1, D_MODEL))],
        out_specs=row(D_MODEL),
        out_shape=jax.ShapeDtypeStruct(x2d.shape, F32),
        compiler_params=pltpu.CompilerParams(dimension_semantics=("arbitrary",),
                                             vmem_limit_bytes=VMEM_LIMIT),
        name="outproj_ffn2",
    )(x2d, oc, osl, ow, convn, ga, wout, gpost, gpre2, wg, wu, wd, gpost2)


def _prep_w_in(w_in):
    sizes = [D_ATTN] + [D_KV] * 6 + [N_GATES] + [D_CONV] * 3
    cuts = np.cumsum([0] + sizes)
    q, kc, vc, ks, vs, kw, vw, gt, bg, cg, xc = [w_in[:, cuts[i]:cuts[i + 1]] for i in range(len(sizes))]
    gt = gt.reshape(D_MODEL, N_GROUPS, HPG, 3).transpose(0, 1, 3, 2).reshape(D_MODEL, N_GROUPS, 3 * HPG)
    gt = jnp.pad(gt, ((0, 0), (0, 0), (0, LANES - 3 * HPG))).reshape(D_MODEL, GATE_COLS)
    return jnp.concatenate([q, kc, vc, ks, vs, kw, vw, gt, bg, cg, xc], axis=1).astype(BF16)


def _prep_compress(pe, w1, w2):
    half = CMP_BLOCK // 2
    eye = jnp.eye(N_GROUPS, dtype=F32)
    w1r = w1.reshape(2, half, HEAD_DIM, CMP_HIDDEN)
    w1big = jnp.einsum("ptdc,gh->ptgdhc", w1r, eye).reshape(2, half * D_KV, N_GROUPS * CMP_HIDDEN)
    w2big = jnp.einsum("cd,gh->gchd", w2, eye).reshape(N_GROUPS * CMP_HIDDEN, D_KV)
    pe_rows = jnp.broadcast_to(pe.reshape(2, half, 1, HEAD_DIM), (2, half, N_GROUPS, HEAD_DIM))
    return pe_rows.reshape(2, half * D_KV), w1big.astype(BF16), w2big.astype(BF16)


def _agg_matrix(nch, nsel):
    agg_w = np.convolve(np.ones(SEL_RATIO), np.ones(CMP_BLOCK // CMP_STRIDE))
    a = np.zeros((nch, nsel), np.float32)
    for j in range(nsel):
        for o, wgt in enumerate(agg_w):
            c = SEL_RATIO * j + o - (CMP_BLOCK // CMP_STRIDE - 1)
            if 0 <= c < nch - 1:
                a[c, j] = wgt
    return jnp.asarray(a, BF16)


def _rope_inv_freq_row():
    inv = ROPE_THETA ** (-np.arange(ROPE_HALF, dtype=np.float32) * 2.0 / ROPE_DIM)
    lane = np.arange(LANES) % HEAD_DIM
    row = np.where(lane < ROPE_DIM, inv[lane % ROPE_HALF], 0.0).astype(np.float32)
    return jnp.asarray(row.reshape(1, LANES))


def _forward(x, positions, p, *, tm, tq, tc, nbh, tqs, tks):
    b, s, _ = x.shape
    depth = p["w_in"].shape[0]
    nch = s // CMP_STRIDE
    nsel = s // SEL_BLOCK
    pos3 = positions.reshape(b, s, 1)
    invf = _rope_inv_freq_row()
    agg = _agg_matrix(nch, nsel)
    row = lambda v: v.reshape(1, -1)
    for l in range(depth):
        x2d = x.reshape(b * s, D_MODEL)
        x2d = _ffn_call(x2d, row(p["ffn1_norm_pre"][l]), p["ffn1_w_gate"][l].astype(BF16),
                        p["ffn1_w_up"][l].astype(BF16), p["ffn1_w_down"][l].astype(BF16),
                        row(p["ffn1_norm_post"][l]), tm)
        (q_raw, q_rot, kc_in, vc_in, ksaug, vs, kw, vw, gates, convn) = _inproj_call(
            x2d.reshape(b, s, D_MODEL), pos3, row(p["mix_norm_pre"][l]), _prep_w_in(p["w_in"][l]), invf,
            p["conv_w"][l], row(p["conv_out_norm"][l]), tm, nbh)
        pek, w1k, w2k = _prep_compress(p["cmp_pe_k"][l], p["cmp_w1_k"][l], p["cmp_w2_k"][l])
        pev, w1v, w2v = _prep_compress(p["cmp_pe_v"][l], p["cmp_w1_v"][l], p["cmp_w2_v"][l])
        kcmp, vcmp = _compress_call(kc_in.reshape(b, nch, CMP_STRIDE * D_KV),
                                    vc_in.reshape(b, nch, CMP_STRIDE * D_KV),
                                    pek, w1k, w2k, pev, w1v, w2v, tc)
        o_cmp, bias = _cmp_call(q_raw, kcmp, vcmp, gates, agg, tq)
        o_slc = _slc_call(q_rot, bias, ksaug, vs, gates, tqs, tks, nbh)
        o_win = _win_call(q_rot, kw, vw, gates, tq)
        flat = lambda a: a.reshape(b * s, a.shape[-1])
        x2d = _out_call(x2d, flat(o_cmp), flat(o_slc), flat(o_win), flat(convn),
                        row(p["attn_out_norm"][l]), p["w_out"][l].astype(BF16), row(p["mix_norm_post"][l]),
                        row(p["ffn2_norm_pre"][l]), p["ffn2_w_gate"][l].astype(BF16),
                        p["ffn2_w_up"][l].astype(BF16), p["ffn2_w_down"][l].astype(BF16),
                        row(p["ffn2_norm_post"][l]), tm)
        x = x2d.reshape(b, s, D_MODEL)
    return x


def kernel(x, positions, ffn1_norm_pre, ffn1_w_gate, ffn1_w_up, ffn1_w_down, ffn1_norm_post, mix_norm_pre, w_in, cmp_pe_k, cmp_w1_k, cmp_w2_k, cmp_pe_v, cmp_w1_v, cmp_w2_v, conv_w, attn_out_norm, conv_out_norm, w_out, mix_norm_post, ffn2_norm_pre, ffn2_w_gate, ffn2_w_up, ffn2_w_down, ffn2_norm_post):
    params = dict(
        ffn1_norm_pre=ffn1_norm_pre, ffn1_w_gate=ffn1_w_gate, ffn1_w_up=ffn1_w_up, ffn1_w_down=ffn1_w_down,
        ffn1_norm_post=ffn1_norm_post, mix_norm_pre=mix_norm_pre, w_in=w_in,
        cmp_pe_k=cmp_pe_k, cmp_w1_k=cmp_w1_k, cmp_w2_k=cmp_w2_k,
        cmp_pe_v=cmp_pe_v, cmp_w1_v=cmp_w1_v, cmp_w2_v=cmp_w2_v,
        conv_w=conv_w, attn_out_norm=attn_out_norm, conv_out_norm=conv_out_norm, w_out=w_out,
        mix_norm_post=mix_norm_post, ffn2_norm_pre=ffn2_norm_pre, ffn2_w_gate=ffn2_w_gate,
        ffn2_w_up=ffn2_w_up, ffn2_w_down=ffn2_w_down, ffn2_norm_post=ffn2_norm_post)
    return _forward(x, positions, params, tm=512, tq=256, tc=256, nbh=128, tqs=256, tks=1024)
```

```python
import functools
import math

import numpy as np
import jax
import jax.numpy as jnp
from jax import lax
from jax.experimental import pallas as pl
from jax.experimental.pallas import tpu as pltpu

D_MODEL = 1024
N_HEADS = 8
HEAD_DIM = 64
N_GROUPS = 2
HPG = N_HEADS // N_GROUPS
D_ATTN = N_HEADS * HEAD_DIM
D_KV = N_GROUPS * HEAD_DIM
D_CONV = D_MODEL - D_ATTN
CONV_WIDTH = 3
CMP_BLOCK = 32
CMP_STRIDE = 16
CMP_SHIFT = 4
CMP_HIDDEN = 256
SEL_BLOCK = 64
SEL_SHIFT = 6
SEL_RATIO = SEL_BLOCK // CMP_STRIDE
N_SELECT = 16
N_FORCED = 3
WINDOW = 512
ROPE_THETA = 500000.0
ROPE_DIM = HEAD_DIM // 4
ROPE_HALF = ROPE_DIM // 2
D_FF = 2816
N_GATES = 3 * N_HEADS
EPS = 1e-6
NEG_INF = -1e30
FORCE_SCORE = 1e9
SCALE = 1.0 / math.sqrt(HEAD_DIM)

LANES = 128
GATE_COLS = N_GROUPS * LANES
VMEM_LIMIT = 56 * 1024 * 1024
FF_CHUNKS = ((0, 768), (768, 1536), (1536, 2304), (2304, 2816))
N_SHIFT_COLS = 2
SAFE_SCORE_BOUND = 40.0

F32 = jnp.float32
BF16 = jnp.bfloat16


def _const_spec(shape):
    nd = len(shape)
    return pl.BlockSpec(shape, lambda *_: (0,) * nd, pipeline_mode=pl.Buffered(1))


def _rms(x, g):
    ms = jnp.mean(x * x, axis=-1, keepdims=True)
    return x * lax.rsqrt(ms + EPS) * g


def _ffn(x, g_pre, wg_ref, wu_ref, wd_ref, g_post):
    h = _rms(x, g_pre).astype(BF16)
    d = None
    for c0, c1 in FF_CHUNKS:
        gate = jnp.dot(h, wg_ref[:, c0:c1], preferred_element_type=F32)
        up = jnp.dot(h, wu_ref[:, c0:c1], preferred_element_type=F32)
        a = (gate * jax.nn.sigmoid(gate) * up).astype(BF16)
        part = jnp.dot(a, wd_ref[c0:c1, :], preferred_element_type=F32)
        d = part if d is None else d + part
    return x + 0.5 * _rms(d, g_post)


def _ffn_kernel(x_ref, gpre_ref, wg_ref, wu_ref, wd_ref, gpost_ref, o_ref):
    o_ref[...] = _ffn(x_ref[...], gpre_ref[...], wg_ref, wu_ref, wd_ref, gpost_ref[...])


def _ffn_call(x2d, g_pre, wg, wu, wd, g_post, tm):
    t = x2d.shape[0]
    row = pl.BlockSpec((tm, D_MODEL), lambda i: (i, 0))
    return pl.pallas_call(
        _ffn_kernel,
        grid=(t // tm,),
        in_specs=[row, _const_spec((1, D_MODEL)), _const_spec(wg.shape), _const_spec(wu.shape),
                  _const_spec(wd.shape), _const_spec((1, D_MODEL))],
        out_specs=row,
        out_shape=jax.ShapeDtypeStruct(x2d.shape, F32),
        compiler_params=pltpu.CompilerParams(dimension_semantics=("arbitrary",),
                                             vmem_limit_bytes=VMEM_LIMIT),
        name="ffn1",
    )(x2d, g_pre, wg, wu, wd, g_post)


_C_Q = 0
_C_KC = _C_Q + D_ATTN
_C_VC = _C_KC + D_KV
_C_KS = _C_VC + D_KV
_C_VS = _C_KS + D_KV
_C_KW = _C_VS + D_KV
_C_VW = _C_KW + D_KV
_C_GATE = _C_VW + D_KV
_C_BG = _C_GATE + GATE_COLS
_C_CG = _C_BG + D_CONV
_C_XC = _C_CG + D_CONV
_C_END = _C_XC + D_CONV


def _inproj_kernel(nbh, x_ref, pos_ref, g_ref, w_ref, invf_ref, convw_ref, convg_ref,
                   qraw_ref, qrot_ref, kc_ref, vc_ref, ksaug_ref, vs_ref, kw_ref, vw_ref,
                   gates_ref, convn_ref, ubuf_ref):
    i = pl.program_id(1)
    tm = x_ref.shape[1]
    h = _rms(x_ref[0], g_ref[...]).astype(BF16)

    def proj(c0, c1):
        return jnp.dot(h, w_ref[:, c0:c1], preferred_element_type=F32)

    ang = pos_ref[0].astype(F32) * invf_ref[...]
    cos = jnp.cos(ang)
    sin = jnp.sin(ang)
    lane = lax.broadcasted_iota(jnp.int32, (1, LANES), 1) & (HEAD_DIM - 1)
    s_lo = jnp.where(lane < ROPE_HALF, -sin, 0.0)
    s_hi = jnp.where(lane >= ROPE_HALF, sin, 0.0)

    def rope(z):
        return (z * cos + pltpu.roll(z, LANES - ROPE_HALF, 1) * s_lo
                + pltpu.roll(z, ROPE_HALF, 1) * s_hi)

    for c in range(D_ATTN // LANES):
        zq = proj(_C_Q + c * LANES, _C_Q + (c + 1) * LANES)
        qraw_ref[0, :, c * LANES:(c + 1) * LANES] = zq.astype(BF16)
        qrot_ref[0, :, c * LANES:(c + 1) * LANES] = rope(zq).astype(BF16)

    kc_ref[0] = proj(_C_KC, _C_VC)
    vc_ref[0] = proj(_C_VC, _C_KS)

    ks = rope(proj(_C_KS, _C_VS)).astype(BF16)
    vs = proj(_C_VS, _C_KW).astype(BF16)
    kw = rope(proj(_C_KW, _C_VW)).astype(BF16)
    vw = proj(_C_VW, _C_GATE).astype(BF16)
    row_blk = ((i * tm + lax.broadcasted_iota(jnp.int32, (tm, nbh), 0)) >> SEL_SHIFT) & (nbh - 1)
    onehot = jnp.where(row_blk == lax.broadcasted_iota(jnp.int32, (tm, nbh), 1), 1.0, 0.0).astype(BF16)
    tail = jnp.where(lax.broadcasted_iota(jnp.int32, (tm, HEAD_DIM), 1) < N_SHIFT_COLS, 1.0, 0.0).astype(BF16)
    for g in range(N_GROUPS):
        sl = slice(g * HEAD_DIM, (g + 1) * HEAD_DIM)
        ksaug_ref[0, g, :, 0:nbh] = onehot
        ksaug_ref[0, g, :, nbh:nbh + HEAD_DIM] = ks[:, sl]
        ksaug_ref[0, g, :, nbh + HEAD_DIM:nbh + 2 * HEAD_DIM] = tail
        vs_ref[0, g, :, 0:HEAD_DIM] = vs[:, sl]
        vs_ref[0, g, :, HEAD_DIM:2 * HEAD_DIM] = tail
        kw_ref[0, g, :, 0:HEAD_DIM] = kw[:, sl]
        kw_ref[0, g, :, HEAD_DIM:2 * HEAD_DIM] = tail
        vw_ref[0, g, :, 0:HEAD_DIM] = vw[:, sl]
        vw_ref[0, g, :, HEAD_DIM:2 * HEAD_DIM] = tail

    gates_ref[0] = jax.nn.sigmoid(proj(_C_GATE, _C_BG))

    u = proj(_C_CG, _C_XC) * proj(_C_XC, _C_END)

    @pl.when(i == 0)
    def _():
        ubuf_ref[0:8, :] = jnp.zeros((8, D_CONV), F32)

    @pl.when(i > 0)
    def _():
        ubuf_ref[0:8, :] = ubuf_ref[tm:tm + 8, :]

    ubuf_ref[8:tm + 8, :] = u
    w = convw_ref[...]
    y = (w[2:3, :] * u + w[1:2, :] * ubuf_ref[7:tm + 7, :] + w[0:1, :] * ubuf_ref[6:tm + 6, :])
    conv = proj(_C_BG, _C_CG) * y
    convn_ref[0] = _rms(conv, convg_ref[...]).astype(BF16)


def _inproj_call(x, pos3, g_pre, w_in, invf, conv_w, conv_g, tm, nbh):
    b, s, _ = x.shape
    grid = (b, s // tm)
    row = lambda width: pl.BlockSpec((1, tm, width), lambda bi, i: (bi, i, 0))
    grp = lambda width: pl.BlockSpec((1, N_GROUPS, tm, width), lambda bi, i: (bi, 0, i, 0))
    kern = functools.partial(_inproj_kernel, nbh)
    return pl.pallas_call(
        kern,
        grid=grid,
        in_specs=[row(D_MODEL), row(1), _const_spec((1, D_MODEL)), _const_spec(w_in.shape),
                  _const_spec((1, LANES)), _const_spec((CONV_WIDTH, D_CONV)), _const_spec((1, D_CONV))],
        out_specs=[row(D_ATTN), row(D_ATTN), row(D_KV), row(D_KV), grp(nbh + 2 * HEAD_DIM), grp(2 * HEAD_DIM),
                   grp(2 * HEAD_DIM), grp(2 * HEAD_DIM), row(GATE_COLS), row(D_CONV)],
        out_shape=[
            jax.ShapeDtypeStruct((b, s, D_ATTN), BF16),
            jax.ShapeDtypeStruct((b, s, D_ATTN), BF16),
            jax.ShapeDtypeStruct((b, s, D_KV), F32),
            jax.ShapeDtypeStruct((b, s, D_KV), F32),
            jax.ShapeDtypeStruct((b, N_GROUPS, s, nbh + 2 * HEAD_DIM), BF16),
            jax.ShapeDtypeStruct((b, N_GROUPS, s, 2 * HEAD_DIM), BF16),
            jax.ShapeDtypeStruct((b, N_GROUPS, s, 2 * HEAD_DIM), BF16),
            jax.ShapeDtypeStruct((b, N_GROUPS, s, 2 * HEAD_DIM), BF16),
            jax.ShapeDtypeStruct((b, s, GATE_COLS), F32),
            jax.ShapeDtypeStruct((b, s, D_CONV), BF16),
        ],
        scratch_shapes=[pltpu.VMEM((tm + 8, D_CONV), F32)],
        compiler_params=pltpu.CompilerParams(dimension_semantics=("arbitrary", "arbitrary"),
                                             vmem_limit_bytes=VMEM_LIMIT),
        name="inproj",
    )(x, pos3, g_pre, w_in, invf, conv_w, conv_g)


def _compress_kernel(kc_ref, kcn_ref, vc_ref, vcn_ref, pek_ref, w1k_ref, w2k_ref,
                     pev_ref, w1v_ref, w2v_ref, ko_ref, vo_ref):
    tc = kc_ref.shape[1]
    hid_w = N_GROUPS * CMP_HIDDEN
    last = lax.broadcasted_iota(jnp.int32, (tc, 1), 0) == tc - 1

    def one(x_ref, xn_ref, pe_ref, w1_ref, w2_ref, o_ref):
        x = x_ref[0]
        top = jnp.dot((x + pe_ref[0:1, :]).astype(BF16), w1_ref[0], preferred_element_type=F32)
        xb = (x + pe_ref[1:2, :]).astype(BF16)
        bot = jnp.dot(xb, w1_ref[1], preferred_element_type=F32)
        xnb = (xn_ref[0] + pe_ref[1:2, :]).astype(BF16)
        botn = jnp.dot(xnb, w1_ref[1], preferred_element_type=F32)
        shifted = jnp.where(last, botn[0:1, :], pltpu.roll(bot, tc - 1, 0))
        hid = jax.nn.gelu(top + shifted).astype(BF16)
        out = jnp.dot(hid, w2_ref[...], preferred_element_type=F32)
        tail = jnp.where(lax.broadcasted_iota(jnp.int32, (tc, HEAD_DIM), 1) < N_SHIFT_COLS, 1.0, 0.0).astype(BF16)
        for g in range(N_GROUPS):
            o_ref[0, g, :, 0:HEAD_DIM] = out[:, g * HEAD_DIM:(g + 1) * HEAD_DIM].astype(BF16)
            o_ref[0, g, :, HEAD_DIM:2 * HEAD_DIM] = tail

    one(kc_ref, kcn_ref, pek_ref, w1k_ref, w2k_ref, ko_ref)
    one(vc_ref, vcn_ref, pev_ref, w1v_ref, w2v_ref, vo_ref)


def _compress_call(kc_in, vc_in, pek, w1k, w2k, pev, w1v, w2v, tc):
    b, nch, width = kc_in.shape
    nt = nch // tc
    last8 = nch // 8 - 1
    cur = pl.BlockSpec((1, tc, width), lambda bi, i: (bi, i, 0))
    nxt = pl.BlockSpec((1, 8, width), lambda bi, i: (bi, jnp.minimum((i + 1) * (tc // 8), last8), 0))
    out = pl.BlockSpec((1, N_GROUPS, tc, 2 * HEAD_DIM), lambda bi, i: (bi, 0, i, 0))
    oshape = jax.ShapeDtypeStruct((b, N_GROUPS, nch, 2 * HEAD_DIM), BF16)
    return pl.pallas_call(
        _compress_kernel,
        grid=(b, nt),
        in_specs=[cur, nxt, cur, nxt, _const_spec(pek.shape), _const_spec(w1k.shape), _const_spec(w2k.shape),
                  _const_spec(pev.shape), _const_spec(w1v.shape), _const_spec(w2v.shape)],
        out_specs=[out, out],
        out_shape=[oshape, oshape],
        compiler_params=pltpu.CompilerParams(dimension_semantics=("arbitrary", "arbitrary"),
                                             vmem_limit_bytes=VMEM_LIMIT),
        name="compress",
    )(kc_in, kc_in, vc_in, vc_in, pek, w1k, w2k, pev, w1v, w2v)


def _split3(x):
    hi = x.astype(BF16)
    r = x - hi.astype(F32)
    mid = r.astype(BF16)
    lo = (r - mid.astype(F32)).astype(BF16)
    return hi, mid, lo


def _tile_score_bound(qs, kmax):
    qf = qs.astype(F32)
    sq = qf * qf
    nrm2 = None
    for h in range(HPG):
        rows = jnp.sum(sq[:, h * HEAD_DIM:(h + 1) * HEAD_DIM], axis=-1, keepdims=True)
        top = jnp.max(rows, axis=0, keepdims=True)
        nrm2 = top if nrm2 is None else jnp.maximum(nrm2, top)
    return jnp.sqrt(nrm2) * kmax


def _shift_cols(shift):
    hi = shift.astype(BF16).astype(F32)
    lo = (shift - hi).astype(BF16).astype(F32)
    lane = lax.broadcasted_iota(jnp.int32, (1, HEAD_DIM), 1)
    return jnp.where(lane == 0, -hi, jnp.where(lane == 1, -lo, 0.0)).astype(BF16)


def _cmp_kernel(k_top, cw, q_ref, kaug_ref, vaug_ref, gates_ref, aggt_ref, o_ref, bias_ref,
                qaug_ref, e_ref, acc_ref, kmax_ref, pslc_ref):
    qt = pl.program_id(2)
    tq = q_ref.shape[1]
    nch = kaug_ref.shape[2]
    nsel = bias_ref.shape[3]
    m_rows = HPG * tq
    t0 = qt * tq
    tcol = slice(HEAD_DIM, 2 * HEAD_DIM)
    c_last = ((t0 + tq - CMP_BLOCK) >> CMP_SHIFT) // cw
    c_mask = jnp.maximum(c_last - 1, 0)

    @pl.when(qt == 0)
    def _():
        k = kaug_ref[0, 0, :, 0:HEAD_DIM].astype(F32)
        n_ok = lax.broadcasted_iota(jnp.int32, (nch, 1), 0) < nch - 1
        ksq = jnp.where(n_ok, jnp.sum(k * k, axis=-1, keepdims=True), 0.0)
        kmax_ref[...] = jnp.broadcast_to(jnp.sqrt(jnp.max(ksq, axis=0, keepdims=True)), kmax_ref.shape)

    qs = q_ref[0] * SCALE
    for h in range(HPG):
        qaug_ref[h * tq:(h + 1) * tq, 0:HEAD_DIM] = qs[:, h * HEAD_DIM:(h + 1) * HEAD_DIM]
    bound = _tile_score_bound(qs, kmax_ref[0:1, 0:1])
    safe = jnp.max(bound) <= SAFE_SCORE_BOUND

    def scores(c, masked):
        ka = kaug_ref[0, 0, pl.ds(pl.multiple_of(c * cw, cw), cw), :]
        s = lax.dot_general(qaug_ref[...], ka, (((1,), (1,)), ((), ())), preferred_element_type=F32)
        if masked:
            row_t = t0 + (lax.broadcasted_iota(jnp.int32, (m_rows, 1), 0) & (tq - 1))
            n_vis = (row_t - (CMP_BLOCK - 1)) >> CMP_SHIFT
            n = c * cw + lax.broadcasted_iota(jnp.int32, (m_rows, cw), 1)
            s = jnp.where(n <= n_vis, s, NEG_INF)
        return s

    def sweep(fn_full, fn_masked, init):
        carry = lax.fori_loop(0, c_mask, fn_full, init)
        return lax.fori_loop(c_mask, c_last + 1, fn_masked, carry)

    @pl.when(safe)
    def _():
        qaug_ref[:, tcol] = jnp.broadcast_to(_shift_cols(bound), (m_rows, HEAD_DIM))

    @pl.when(jnp.logical_not(safe))
    def _():
        qaug_ref[:, tcol] = jnp.zeros((m_rows, HEAD_DIM), BF16)
        step = lambda masked: (lambda c, m: jnp.maximum(m, jnp.max(scores(c, masked), axis=-1, keepdims=True)))
        mx = sweep(step(False), step(True), jnp.full((m_rows, 1), NEG_INF, F32))
        qaug_ref[:, tcol] = _shift_cols(jnp.where(mx > 0.5 * NEG_INF, mx, 0.0))

    acc_ref[...] = jnp.zeros(acc_ref.shape, F32)

    def chunk(masked):
        def body(c, carry):
            e = jnp.exp(scores(c, masked))
            start = pl.multiple_of(c * cw, cw)
            e_ref[c] = e
            acc_ref[...] += jnp.dot(e.astype(BF16), vaug_ref[0, 0, pl.ds(start, cw), :],
                                    preferred_element_type=F32)
            return carry
        return body

    sweep(chunk(False), chunk(True), 0)
    acc = acc_ref[...]
    l = acc[:, HEAD_DIM:HEAD_DIM + 1]
    rinv = jnp.where(l > 0.0, 1.0 / l, 0.0)
    o = acc[:, 0:HEAD_DIM] * rinv
    gates = gates_ref[0]
    o_ref[0] = jnp.concatenate(
        [o[h * tq:(h + 1) * tq] * gates[:, h:h + 1] for h in range(HPG)], axis=-1)

    pslc_ref[...] = jnp.zeros(pslc_ref.shape, F32)

    def agg_body(c, carry):
        imp = sum(e_ref[c, h * tq:(h + 1) * tq, :] * rinv[h * tq:(h + 1) * tq] for h in range(HPG))
        at = aggt_ref[c]
        pslc_ref[...] += sum(lax.dot_general(at, part, (((1,), (1,)), ((), ())), preferred_element_type=F32)
                             for part in _split3(imp))
        return carry

    lax.fori_loop(0, c_last + 1, agg_body, 0)

    j_idx = lax.broadcasted_iota(jnp.int32, (nsel, 1), 0)
    j_f = j_idx.astype(F32)
    for col in range(tq // LANES):
        lanes = slice(col * LANES, (col + 1) * LANES)
        cur = (t0 + col * LANES + lax.broadcasted_iota(jnp.int32, (1, LANES), 1)) >> SEL_SHIFT
        valid_b = j_idx <= cur
        forced = (j_idx == 0) | (j_idx == cur) | (j_idx == cur - 1)
        score = jnp.where(valid_b & jnp.logical_not(forced), pslc_ref[:, lanes], -jnp.inf)
        for _ in range(max(k_top - N_FORCED, 0)):
            m = jnp.max(score, axis=0, keepdims=True)
            first = jnp.min(jnp.where(score == m, j_f, float(nsel)), axis=0, keepdims=True)
            score = jnp.where(j_f == first, -jnp.inf, score)
        pslc_ref[:, lanes] = jnp.where(valid_b & (score == -jnp.inf), 0.0, NEG_INF)
    bias_ref[0, 0] = pslc_ref[...].T.astype(BF16)


def _cmp_call(q_raw, kaug, vaug, gates, aggt, tq, cw):
    b, s, _ = q_raw.shape
    nch = kaug.shape[2]
    nsel = aggt.shape[1]
    assert nch % cw == 0 and tq // CMP_STRIDE < cw
    k_top = min(N_SELECT, nsel)
    gw = HPG * HEAD_DIM
    m_rows = HPG * tq
    qspec = pl.BlockSpec((1, tq, gw), lambda bi, g, i: (bi, i, g))
    kvspec = pl.BlockSpec((1, 1, nch, 2 * HEAD_DIM), lambda bi, g, i: (bi, g, 0, 0))
    gspec = pl.BlockSpec((1, tq, LANES), lambda bi, g, i: (bi, i, g))
    return pl.pallas_call(
        functools.partial(_cmp_kernel, k_top, cw),
        grid=(b, N_GROUPS, s // tq),
        in_specs=[qspec, kvspec, kvspec, gspec, _const_spec(aggt.shape)],
        out_specs=[qspec, pl.BlockSpec((1, 1, tq, nsel), lambda bi, g, i: (bi, g, i, 0))],
        out_shape=[jax.ShapeDtypeStruct((b, s, D_ATTN), F32),
                   jax.ShapeDtypeStruct((b, N_GROUPS, s, nsel), BF16)],
        scratch_shapes=[pltpu.VMEM((m_rows, 2 * HEAD_DIM), BF16),
                        pltpu.VMEM((nch // cw, m_rows, cw), F32),
                        pltpu.VMEM((m_rows, 2 * HEAD_DIM), F32),
                        pltpu.VMEM((8, LANES), F32),
                        pltpu.VMEM((nsel, tq), F32)],
        compiler_params=pltpu.CompilerParams(dimension_semantics=("arbitrary",) * 3,
                                             vmem_limit_bytes=VMEM_LIMIT),
        name="cmp_attn_topk",
    )(q_raw, kaug, vaug, gates, aggt)


def _slc_kernel(nbh, tk, q_ref, bias_ref, kaug_ref, vaug_ref, gates_ref, o_ref,
                qaug_ref, acc_ref, kmax_ref):
    qt = pl.program_id(2)
    tq = q_ref.shape[1]
    m_rows = HPG * tq
    t0 = qt * tq
    n_full = (t0 + tq - 1) // tk
    n_half = qaug_ref.shape[0]
    tiles_per_half = nbh * SEL_BLOCK // tk
    kcol = slice(nbh, nbh + HEAD_DIM)
    tcol = slice(nbh + HEAD_DIM, nbh + 2 * HEAD_DIM)

    @pl.when(qt == 0)
    def _():
        def body(c, mx):
            k = kaug_ref[0, 0, pl.ds(pl.multiple_of(c * tk, tk), tk), kcol].astype(F32)
            return jnp.maximum(mx, jnp.sum(k * k, axis=-1, keepdims=True))
        mx = lax.fori_loop(0, kaug_ref.shape[2] // tk, body, jnp.zeros((tk, 1), F32))
        kmax_ref[...] = jnp.broadcast_to(jnp.sqrt(jnp.max(mx, axis=0, keepdims=True)), kmax_ref.shape)

    qs = q_ref[0] * SCALE
    bias = bias_ref[0, 0]
    for h in range(HPG):
        rows = slice(h * tq, (h + 1) * tq)
        for hf in range(n_half):
            qaug_ref[hf, rows, 0:nbh] = bias[:, hf * nbh:(hf + 1) * nbh]
            qaug_ref[hf, rows, kcol] = qs[:, h * HEAD_DIM:(h + 1) * HEAD_DIM]
    bound = _tile_score_bound(qs, kmax_ref[0:1, 0:1])
    safe = jnp.max(bound) <= SAFE_SCORE_BOUND

    def scores(kt, causal):
        start = pl.multiple_of(kt * tk, tk)
        ka = kaug_ref[0, 0, pl.ds(start, tk), :]
        qa = qaug_ref[kt // tiles_per_half]
        s = lax.dot_general(qa, ka, (((1,), (1,)), ((), ())), preferred_element_type=F32)
        if causal:
            r = lax.broadcasted_iota(jnp.int32, (m_rows, tk), 0) & (tq - 1)
            c = lax.broadcasted_iota(jnp.int32, (m_rows, tk), 1)
            s = jnp.where(c - r <= t0 - kt * tk, s, NEG_INF)
        return s

    @pl.when(safe)
    def _():
        tail = jnp.broadcast_to(_shift_cols(bound), (m_rows, HEAD_DIM))
        for hf in range(n_half):
            qaug_ref[hf, :, tcol] = tail

    @pl.when(jnp.logical_not(safe))
    def _():
        for hf in range(n_half):
            qaug_ref[hf, :, tcol] = jnp.zeros((m_rows, HEAD_DIM), BF16)
        rowmax = lambda kt, causal: jnp.max(scores(kt, causal), axis=-1, keepdims=True)
        mx = lax.fori_loop(0, n_full, lambda kt, m: jnp.maximum(m, rowmax(kt, False)),
                           jnp.full((m_rows, 1), NEG_INF, F32))
        tail = _shift_cols(jnp.maximum(mx, rowmax(n_full, True)))
        for hf in range(n_half):
            qaug_ref[hf, :, tcol] = tail

    acc_ref[...] = jnp.zeros(acc_ref.shape, F32)

    def tile(kt, causal):
        p = jnp.exp(scores(kt, causal)).astype(BF16)
        va = vaug_ref[0, 0, pl.ds(pl.multiple_of(kt * tk, tk), tk), :]
        acc_ref[...] += jnp.dot(p, va, preferred_element_type=F32)

    def body(kt, c):
        tile(kt, False)
        return c

    lax.fori_loop(0, n_full, body, 0)
    tile(n_full, True)
    acc = acc_ref[...]
    o = acc[:, 0:HEAD_DIM] / acc[:, HEAD_DIM:HEAD_DIM + 1]
    gates = gates_ref[0]
    o_ref[0] = jnp.concatenate(
        [o[h * tq:(h + 1) * tq] * gates[:, HPG + h:HPG + h + 1] for h in range(HPG)], axis=-1)


def _slc_call(q_rot, bias, kaug, v, gates, tq, tk, nbh):
    b, s, _ = q_rot.shape
    assert tk % tq == 0 and (nbh * SEL_BLOCK) % tk == 0 and s % tk == 0
    nsel = bias.shape[3]
    gw = HPG * HEAD_DIM
    aug = nbh + 2 * HEAD_DIM
    qspec = pl.BlockSpec((1, tq, gw), lambda bi, g, i: (bi, i, g))
    return pl.pallas_call(
        functools.partial(_slc_kernel, nbh, tk),
        grid=(b, N_GROUPS, s // tq),
        in_specs=[qspec,
                  pl.BlockSpec((1, 1, tq, nsel), lambda bi, g, i: (bi, g, i, 0)),
                  pl.BlockSpec((1, 1, s, aug), lambda bi, g, i: (bi, g, 0, 0)),
                  pl.BlockSpec((1, 1, s, 2 * HEAD_DIM), lambda bi, g, i: (bi, g, 0, 0)),
                  pl.BlockSpec((1, tq, LANES), lambda bi, g, i: (bi, i, g))],
        out_specs=qspec,
        out_shape=jax.ShapeDtypeStruct((b, s, D_ATTN), F32),
        scratch_shapes=[pltpu.VMEM((nsel // nbh, HPG * tq, aug), BF16),
                        pltpu.VMEM((HPG * tq, 2 * HEAD_DIM), F32),
                        pltpu.VMEM((8, LANES), F32)],
        compiler_params=pltpu.CompilerParams(dimension_semantics=("arbitrary",) * 3,
                                             vmem_limit_bytes=VMEM_LIMIT),
        name="slc_attn",
    )(q_rot, bias, kaug, v, gates)


def _win_kernel(q_ref, kaug_ref, vaug_ref, gates_ref, o_ref, qaug_ref, kmax_ref):
    qt = pl.program_id(2)
    tq = q_ref.shape[1]
    span = WINDOW + tq
    m_rows = HPG * tq
    t0 = qt * tq
    tcol = slice(HEAD_DIM, 2 * HEAD_DIM)

    @pl.when(qt == 0)
    def _():
        def body(c, mx):
            k = kaug_ref[0, 0, pl.ds(pl.multiple_of(c * tq, tq), tq), 0:HEAD_DIM].astype(F32)
            return jnp.maximum(mx, jnp.sum(k * k, axis=-1, keepdims=True))
        mx = lax.fori_loop(0, kaug_ref.shape[2] // tq, body, jnp.zeros((tq, 1), F32))
        kmax_ref[...] = jnp.broadcast_to(jnp.sqrt(jnp.max(mx, axis=0, keepdims=True)), kmax_ref.shape)

    start = pl.multiple_of(jnp.maximum(t0 - WINDOW, 0), tq)
    ka = kaug_ref[0, 0, pl.ds(start, span), :]
    qs = q_ref[0] * SCALE
    for h in range(HPG):
        qaug_ref[h * tq:(h + 1) * tq, 0:HEAD_DIM] = qs[:, h * HEAD_DIM:(h + 1) * HEAD_DIM]
    bound = _tile_score_bound(qs, kmax_ref[0:1, 0:1])
    safe = jnp.max(bound) <= SAFE_SCORE_BOUND

    def scores():
        s = lax.dot_general(qaug_ref[...], ka, (((1,), (1,)), ((), ())), preferred_element_type=F32)
        d = (lax.broadcasted_iota(jnp.int32, (m_rows, tq), 1)
             - (lax.broadcasted_iota(jnp.int32, (m_rows, tq), 0) & (tq - 1)))
        blocks = []
        for blk in range(span // tq):
            off = t0 - start - blk * tq
            ok = lax.bitcast_convert_type(off - d, jnp.uint32) < jnp.uint32(WINDOW)
            blocks.append(jnp.where(ok, s[:, blk * tq:(blk + 1) * tq], NEG_INF))
        return jnp.concatenate(blocks, axis=-1)

    @pl.when(safe)
    def _():
        qaug_ref[:, tcol] = jnp.broadcast_to(_shift_cols(bound), (m_rows, HEAD_DIM))

    @pl.when(jnp.logical_not(safe))
    def _():
        qaug_ref[:, tcol] = jnp.zeros((m_rows, HEAD_DIM), BF16)
        qaug_ref[:, tcol] = _shift_cols(jnp.max(scores(), axis=-1, keepdims=True))

    p = jnp.exp(scores()).astype(BF16)
    acc = jnp.dot(p, vaug_ref[0, 0, pl.ds(start, span), :], preferred_element_type=F32)
    o = acc[:, 0:HEAD_DIM] / acc[:, HEAD_DIM:HEAD_DIM + 1]
    gates = gates_ref[0]
    o_ref[0] = jnp.concatenate(
        [o[h * tq:(h + 1) * tq] * gates[:, 2 * HPG + h:2 * HPG + h + 1] for h in range(HPG)], axis=-1)


def _win_call(q_rot, k, v, gates, tq):
    b, s, _ = q_rot.shape
    gw = HPG * HEAD_DIM
    qspec = pl.BlockSpec((1, tq, gw), lambda bi, g, i: (bi, i, g))
    kvspec = pl.BlockSpec((1, 1, s, 2 * HEAD_DIM), lambda bi, g, i: (bi, g, 0, 0))
    return pl.pallas_call(
        _win_kernel,
        grid=(b, N_GROUPS, s // tq),
        in_specs=[qspec, kvspec, kvspec, pl.BlockSpec((1, tq, LANES), lambda bi, g, i: (bi, i, g))],
        out_specs=qspec,
        out_shape=jax.ShapeDtypeStruct((b, s, D_ATTN), F32),
        scratch_shapes=[pltpu.VMEM((HPG * tq, 2 * HEAD_DIM), BF16),
                        pltpu.VMEM((8, LANES), F32)],
        compiler_params=pltpu.CompilerParams(dimension_semantics=("arbitrary",) * 3,
                                             vmem_limit_bytes=VMEM_LIMIT),
        name="win_attn",
    )(q_rot, k, v, gates)


def _out_kernel(x_ref, oc_ref, os_ref, ow_ref, convn_ref, ga_ref, wout_ref, gpost_ref,
                gpre2_ref, wg_ref, wu_ref, wd_ref, gpost2_ref, o_ref):
    attn = oc_ref[...] + os_ref[...] + ow_ref[...]
    an = _rms(attn, ga_ref[...]).astype(BF16)
    h = (jnp.dot(an, wout_ref[0:D_ATTN, :], preferred_element_type=F32)
         + jnp.dot(convn_ref[...], wout_ref[D_ATTN:D_MODEL, :], preferred_element_type=F32))
    x1 = x_ref[...] + _rms(h, gpost_ref[...])
    o_ref[...] = _ffn(x1, gpre2_ref[...], wg_ref, wu_ref, wd_ref, gpost2_ref[...])


def _out_call(x2d, oc, osl, ow, convn, ga, wout, gpost, gpre2, wg, wu, wd, gpost2, tm):
    t = x2d.shape[0]
    row = lambda width: pl.BlockSpec((tm, width), lambda i: (i, 0))
    return pl.pallas_call(
        _out_kernel,
        grid=(t // tm,),
        in_specs=[row(D_MODEL), row(D_ATTN), row(D_ATTN), row(D_ATTN), row(D_CONV),
                  _const_spec((1, D_ATTN)), _const_spec(wout.shape), _const_spec((1, D_MODEL)),
                  _const_spec((1, D_MODEL)), _const_spec(wg.shape), _const_spec(wu.shape),
                  _const_spec(wd.shape), _const_spec((1, D_MODEL))],
        out_specs=row(D_MODEL),
        out_shape=jax.ShapeDtypeStruct(x2d.shape, F32),
        compiler_params=pltpu.CompilerParams(dimension_semantics=("arbitrary",),
                                             vmem_limit_bytes=VMEM_LIMIT),
        name="outproj_ffn2",
    )(x2d, oc, osl, ow, convn, ga, wout, gpost, gpre2, wg, wu, wd, gpost2)


def _prep_w_in(w_in):
    sizes = [D_ATTN] + [D_KV] * 6 + [N_GATES] + [D_CONV] * 3
    cuts = np.cumsum([0] + sizes)
    q, kc, vc, ks, vs, kw, vw, gt, bg, cg, xc = [w_in[:, cuts[i]:cuts[i + 1]] for i in range(len(sizes))]
    gt = gt.reshape(D_MODEL, N_GROUPS, HPG, 3).transpose(0, 1, 3, 2).reshape(D_MODEL, N_GROUPS, 3 * HPG)
    gt = jnp.pad(gt, ((0, 0), (0, 0), (0, LANES - 3 * HPG))).reshape(D_MODEL, GATE_COLS)
    return jnp.concatenate([q, kc, vc, ks, vs, kw, vw, gt, bg, cg, xc], axis=1).astype(BF16)


def _prep_compress(pe, w1, w2):
    half = CMP_BLOCK // 2
    eye = jnp.eye(N_GROUPS, dtype=F32)
    w1r = w1.reshape(2, half, HEAD_DIM, CMP_HIDDEN)
    w1big = jnp.einsum("ptdc,gh->ptgdhc", w1r, eye).reshape(2, half * D_KV, N_GROUPS * CMP_HIDDEN)
    w2big = jnp.einsum("cd,gh->gchd", w2, eye).reshape(N_GROUPS * CMP_HIDDEN, D_KV)
    pe_rows = jnp.broadcast_to(pe.reshape(2, half, 1, HEAD_DIM), (2, half, N_GROUPS, HEAD_DIM))
    return pe_rows.reshape(2, half * D_KV), w1big.astype(BF16), w2big.astype(BF16)


def _agg_matrix(nch, nsel, cw):
    agg_w = np.convolve(np.ones(SEL_RATIO), np.ones(CMP_BLOCK // CMP_STRIDE))
    a = np.zeros((nch, nsel), np.float32)
    for j in range(nsel):
        for o, wgt in enumerate(agg_w):
            c = SEL_RATIO * j + o - (CMP_BLOCK // CMP_STRIDE - 1)
            if 0 <= c < nch - 1:
                a[c, j] = wgt
    a = a.T.reshape(nsel, nch // cw, cw).transpose(1, 0, 2)
    return jnp.asarray(a, BF16)


def _rope_inv_freq_row():
    inv = ROPE_THETA ** (-np.arange(ROPE_HALF, dtype=np.float32) * 2.0 / ROPE_DIM)
    lane = np.arange(LANES) % HEAD_DIM
    row = np.where(lane < ROPE_DIM, inv[lane % ROPE_HALF], 0.0).astype(np.float32)
    return jnp.asarray(row.reshape(1, LANES))


def _forward(x, positions, p, *, tm, tq, tc, nbh, tqs, tks, cw):
    b, s, _ = x.shape
    depth = p["w_in"].shape[0]
    nch = s // CMP_STRIDE
    nsel = s // SEL_BLOCK
    pos3 = positions.reshape(b, s, 1)
    invf = _rope_inv_freq_row()
    agg = _agg_matrix(nch, nsel, cw)
    row = lambda v: v.reshape(1, -1)
    for l in range(depth):
        x2d = x.reshape(b * s, D_MODEL)
        x2d = _ffn_call(x2d, row(p["ffn1_norm_pre"][l]), p["ffn1_w_gate"][l].astype(BF16),
                        p["ffn1_w_up"][l].astype(BF16), p["ffn1_w_down"][l].astype(BF16),
                        row(p["ffn1_norm_post"][l]), tm)
        (q_raw, q_rot, kc_in, vc_in, ksaug, vs, kw, vw, gates, convn) = _inproj_call(
            x2d.reshape(b, s, D_MODEL), pos3, row(p["mix_norm_pre"][l]), _prep_w_in(p["w_in"][l]), invf,
            p["conv_w"][l], row(p["conv_out_norm"][l]), tm, nbh)
        pek, w1k, w2k = _prep_compress(p["cmp_pe_k"][l], p["cmp_w1_k"][l], p["cmp_w2_k"][l])
        pev, w1v, w2v = _prep_compress(p["cmp_pe_v"][l], p["cmp_w1_v"][l], p["cmp_w2_v"][l])
        kcmp, vcmp = _compress_call(kc_in.reshape(b, nch, CMP_STRIDE * D_KV),
                                    vc_in.reshape(b, nch, CMP_STRIDE * D_KV),
                                    pek, w1k, w2k, pev, w1v, w2v, tc)
        o_cmp, bias = _cmp_call(q_raw, kcmp, vcmp, gates, agg, tq, cw)
        o_slc = _slc_call(q_rot, bias, ksaug, vs, gates, tqs, tks, nbh)
        o_win = _win_call(q_rot, kw, vw, gates, tq)
        flat = lambda a: a.reshape(b * s, a.shape[-1])
        x2d = _out_call(x2d, flat(o_cmp), flat(o_slc), flat(o_win), flat(convn),
                        row(p["attn_out_norm"][l]), p["w_out"][l].astype(BF16), row(p["mix_norm_post"][l]),
                        row(p["ffn2_norm_pre"][l]), p["ffn2_w_gate"][l].astype(BF16),
                        p["ffn2_w_up"][l].astype(BF16), p["ffn2_w_down"][l].astype(BF16),
                        row(p["ffn2_norm_post"][l]), tm)
        x = x2d.reshape(b, s, D_MODEL)
    return x


def kernel(x, positions, ffn1_norm_pre, ffn1_w_gate, ffn1_w_up, ffn1_w_down, ffn1_norm_post, mix_norm_pre, w_in, cmp_pe_k, cmp_w1_k, cmp_w2_k, cmp_pe_v, cmp_w1_v, cmp_w2_v, conv_w, attn_out_norm, conv_out_norm, w_out, mix_norm_post, ffn2_norm_pre, ffn2_w_gate, ffn2_w_up, ffn2_w_down, ffn2_norm_post):
    params = dict(
        ffn1_norm_pre=ffn1_norm_pre, ffn1_w_gate=ffn1_w_gate, ffn1_w_up=ffn1_w_up, ffn1_w_down=ffn1_w_down,
        ffn1_norm_post=ffn1_norm_post, mix_norm_pre=mix_norm_pre, w_in=w_in,
        cmp_pe_k=cmp_pe_k, cmp_w1_k=cmp_w1_k, cmp_w2_k=cmp_w2_k,
        cmp_pe_v=cmp_pe_v, cmp_w1_v=cmp_w1_v, cmp_w2_v=cmp_w2_v,
        conv_w=conv_w, attn_out_norm=attn_out_norm, conv_out_norm=conv_out_norm, w_out=w_out,
        mix_norm_post=mix_norm_post, ffn2_norm_pre=ffn2_norm_pre, ffn2_w_gate=ffn2_w_gate,
        ffn2_w_up=ffn2_w_up, ffn2_w_down=ffn2_w_down, ffn2_norm_post=ffn2_norm_post)
    return _forward(x, positions, params, tm=512, tq=256, tc=256, nbh=128, tqs=256, tks=1024, cw=256)
```

```python
import functools
import math

import numpy as np
import jax
import jax.numpy as jnp
from jax import lax
from jax.experimental import pallas as pl
from jax.experimental.pallas import tpu as pltpu

D_MODEL = 1024
N_HEADS = 8
HEAD_DIM = 64
N_GROUPS = 2
HPG = N_HEADS // N_GROUPS
D_ATTN = N_HEADS * HEAD_DIM
D_KV = N_GROUPS * HEAD_DIM
D_CONV = D_MODEL - D_ATTN
CONV_WIDTH = 3
CMP_BLOCK = 32
CMP_STRIDE = 16
CMP_SHIFT = 4
CMP_HIDDEN = 256
SEL_BLOCK = 64
SEL_SHIFT = 6
SEL_RATIO = SEL_BLOCK // CMP_STRIDE
N_SELECT = 16
N_FORCED = 3
WINDOW = 512
ROPE_THETA = 500000.0
ROPE_DIM = HEAD_DIM // 4
ROPE_HALF = ROPE_DIM // 2
D_FF = 2816
N_GATES = 3 * N_HEADS
EPS = 1e-6
NEG_INF = -1e30
FORCE_SCORE = 1e9
SCALE = 1.0 / math.sqrt(HEAD_DIM)

LANES = 128
SUBLANES = 8
GATE_COLS = N_GROUPS * LANES
VMEM_LIMIT = 56 * 1024 * 1024
FF_CHUNKS = ((0, 768), (768, 1536), (1536, 2304), (2304, 2816))
N_SHIFT_COLS = 2
SAFE_SCORE_BOUND = 40.0

F32 = jnp.float32
BF16 = jnp.bfloat16


def _const_spec(shape):
    nd = len(shape)
    return pl.BlockSpec(shape, lambda *_: (0,) * nd, pipeline_mode=pl.Buffered(1))


def _rms(x, g):
    ms = jnp.mean(x * x, axis=-1, keepdims=True)
    return x * lax.rsqrt(ms + EPS) * g


def _ffn(x, g_pre, wg_ref, wu_ref, wd_ref, g_post):
    h = _rms(x, g_pre).astype(BF16)
    d = None
    for c0, c1 in FF_CHUNKS:
        gate = jnp.dot(h, wg_ref[:, c0:c1], preferred_element_type=F32)
        up = jnp.dot(h, wu_ref[:, c0:c1], preferred_element_type=F32)
        a = (gate * jax.nn.sigmoid(gate) * up).astype(BF16)
        part = jnp.dot(a, wd_ref[c0:c1, :], preferred_element_type=F32)
        d = part if d is None else d + part
    return x + 0.5 * _rms(d, g_post)


def _ffn_kernel(x_ref, gpre_ref, wg_ref, wu_ref, wd_ref, gpost_ref, o_ref):
    o_ref[...] = _ffn(x_ref[...], gpre_ref[...], wg_ref, wu_ref, wd_ref, gpost_ref[...])


def _ffn_call(x2d, g_pre, wg, wu, wd, g_post, tm):
    t = x2d.shape[0]
    row = pl.BlockSpec((tm, D_MODEL), lambda i: (i, 0))
    return pl.pallas_call(
        _ffn_kernel,
        grid=(t // tm,),
        in_specs=[row, _const_spec((1, D_MODEL)), _const_spec(wg.shape), _const_spec(wu.shape),
                  _const_spec(wd.shape), _const_spec((1, D_MODEL))],
        out_specs=row,
        out_shape=jax.ShapeDtypeStruct(x2d.shape, F32),
        compiler_params=pltpu.CompilerParams(dimension_semantics=("arbitrary",),
                                             vmem_limit_bytes=VMEM_LIMIT),
        name="ffn1",
    )(x2d, g_pre, wg, wu, wd, g_post)


_C_Q = 0
_C_KC = _C_Q + D_ATTN
_C_VC = _C_KC + D_KV
_C_KS = _C_VC + D_KV
_C_VS = _C_KS + D_KV
_C_KW = _C_VS + D_KV
_C_VW = _C_KW + D_KV
_C_GATE = _C_VW + D_KV
_C_BG = _C_GATE + GATE_COLS
_C_CG = _C_BG + D_CONV
_C_XC = _C_CG + D_CONV
_C_END = _C_XC + D_CONV


def _rope_kernel(pos_ref, invf_ref, cos_ref, sin_ref):
    ang = pos_ref[0].astype(F32) * invf_ref[...]
    cos_ref[0] = jnp.cos(ang)
    sin_ref[0] = jnp.sin(ang)


def _rope_call(pos3, invf, tm):
    b, s, _ = pos3.shape
    row = lambda width: pl.BlockSpec((1, tm, width), lambda bi, i: (bi, i, 0))
    shape = jax.ShapeDtypeStruct((b, s, LANES), F32)
    return pl.pallas_call(
        _rope_kernel,
        grid=(b, s // tm),
        in_specs=[row(1), _const_spec((1, LANES))],
        out_specs=[row(LANES), row(LANES)],
        out_shape=[shape, shape],
        compiler_params=pltpu.CompilerParams(dimension_semantics=("arbitrary", "arbitrary"),
                                             vmem_limit_bytes=VMEM_LIMIT),
        name="rope_tables",
    )(pos3, invf)


def _inproj_kernel(nbh, x_ref, cos_ref, sin_ref, g_ref, w_ref, convw_ref, convg_ref,
                   qraw_ref, qrot_ref, kc_ref, vc_ref, ksaug_ref, vs_ref, kw_ref, vw_ref,
                   gates_ref, convn_ref, ubuf_ref):
    i = pl.program_id(1)
    tm = x_ref.shape[1]
    h = _rms(x_ref[0], g_ref[...]).astype(BF16)

    def proj(c0, c1):
        return jnp.dot(h, w_ref[:, c0:c1], preferred_element_type=F32)

    cos = cos_ref[0]
    sin = sin_ref[0]
    lane = lax.broadcasted_iota(jnp.int32, (1, LANES), 1) & (HEAD_DIM - 1)
    s_lo = jnp.where(lane < ROPE_HALF, -sin, 0.0)
    s_hi = jnp.where(lane >= ROPE_HALF, sin, 0.0)

    def rope(z):
        return (z * cos + pltpu.roll(z, LANES - ROPE_HALF, 1) * s_lo
                + pltpu.roll(z, ROPE_HALF, 1) * s_hi)

    for c in range(D_ATTN // LANES):
        zq = proj(_C_Q + c * LANES, _C_Q + (c + 1) * LANES)
        qraw_ref[0, :, c * LANES:(c + 1) * LANES] = zq.astype(BF16)
        qrot_ref[0, :, c * LANES:(c + 1) * LANES] = rope(zq).astype(BF16)

    kc_ref[0] = proj(_C_KC, _C_VC)
    vc_ref[0] = proj(_C_VC, _C_KS)

    ks = rope(proj(_C_KS, _C_VS)).astype(BF16)
    vs = proj(_C_VS, _C_KW).astype(BF16)
    kw = rope(proj(_C_KW, _C_VW)).astype(BF16)
    vw = proj(_C_VW, _C_GATE).astype(BF16)
    row_blk = ((i * tm + lax.broadcasted_iota(jnp.int32, (tm, nbh), 0)) >> SEL_SHIFT) & (nbh - 1)
    onehot = jnp.where(row_blk == lax.broadcasted_iota(jnp.int32, (tm, nbh), 1), 1.0, 0.0).astype(BF16)
    tail = jnp.where(lax.broadcasted_iota(jnp.int32, (tm, HEAD_DIM), 1) < N_SHIFT_COLS, 1.0, 0.0).astype(BF16)
    for g in range(N_GROUPS):
        sl = slice(g * HEAD_DIM, (g + 1) * HEAD_DIM)
        ksaug_ref[0, g, :, 0:nbh] = onehot
        ksaug_ref[0, g, :, nbh:nbh + HEAD_DIM] = ks[:, sl]
        ksaug_ref[0, g, :, nbh + HEAD_DIM:nbh + 2 * HEAD_DIM] = tail
        vs_ref[0, g, :, 0:HEAD_DIM] = vs[:, sl]
        vs_ref[0, g, :, HEAD_DIM:2 * HEAD_DIM] = tail
        kw_ref[0, g, :, 0:HEAD_DIM] = kw[:, sl]
        kw_ref[0, g, :, HEAD_DIM:2 * HEAD_DIM] = tail
        vw_ref[0, g, :, 0:HEAD_DIM] = vw[:, sl]
        vw_ref[0, g, :, HEAD_DIM:2 * HEAD_DIM] = tail

    gates_ref[0] = jax.nn.sigmoid(proj(_C_GATE, _C_BG))

    u = proj(_C_CG, _C_XC) * proj(_C_XC, _C_END)

    @pl.when(i == 0)
    def _():
        ubuf_ref[0:8, :] = jnp.zeros((8, D_CONV), F32)

    @pl.when(i > 0)
    def _():
        ubuf_ref[0:8, :] = ubuf_ref[tm:tm + 8, :]

    ubuf_ref[8:tm + 8, :] = u
    w = convw_ref[...]
    y = (w[2:3, :] * u + w[1:2, :] * ubuf_ref[7:tm + 7, :] + w[0:1, :] * ubuf_ref[6:tm + 6, :])
    conv = proj(_C_BG, _C_CG) * y
    convn_ref[0] = _rms(conv, convg_ref[...]).astype(BF16)


def _inproj_call(x, cos, sin, g_pre, w_in, conv_w, conv_g, tm, nbh):
    b, s, _ = x.shape
    grid = (b, s // tm)
    row = lambda width: pl.BlockSpec((1, tm, width), lambda bi, i: (bi, i, 0))
    grp = lambda width: pl.BlockSpec((1, N_GROUPS, tm, width), lambda bi, i: (bi, 0, i, 0))
    kern = functools.partial(_inproj_kernel, nbh)
    return pl.pallas_call(
        kern,
        grid=grid,
        in_specs=[row(D_MODEL), row(LANES), row(LANES), _const_spec((1, D_MODEL)), _const_spec(w_in.shape),
                  _const_spec((CONV_WIDTH, D_CONV)), _const_spec((1, D_CONV))],
        out_specs=[row(D_ATTN), row(D_ATTN), row(D_KV), row(D_KV), grp(nbh + 2 * HEAD_DIM), grp(2 * HEAD_DIM),
                   grp(2 * HEAD_DIM), grp(2 * HEAD_DIM), row(GATE_COLS), row(D_CONV)],
        out_shape=[
            jax.ShapeDtypeStruct((b, s, D_ATTN), BF16),
            jax.ShapeDtypeStruct((b, s, D_ATTN), BF16),
            jax.ShapeDtypeStruct((b, s, D_KV), F32),
            jax.ShapeDtypeStruct((b, s, D_KV), F32),
            jax.ShapeDtypeStruct((b, N_GROUPS, s, nbh + 2 * HEAD_DIM), BF16),
            jax.ShapeDtypeStruct((b, N_GROUPS, s, 2 * HEAD_DIM), BF16),
            jax.ShapeDtypeStruct((b, N_GROUPS, s, 2 * HEAD_DIM), BF16),
            jax.ShapeDtypeStruct((b, N_GROUPS, s, 2 * HEAD_DIM), BF16),
            jax.ShapeDtypeStruct((b, s, GATE_COLS), F32),
            jax.ShapeDtypeStruct((b, s, D_CONV), BF16),
        ],
        scratch_shapes=[pltpu.VMEM((tm + 8, D_CONV), F32)],
        compiler_params=pltpu.CompilerParams(dimension_semantics=("arbitrary", "arbitrary"),
                                             vmem_limit_bytes=VMEM_LIMIT),
        name="inproj",
    )(x, cos, sin, g_pre, w_in, conv_w, conv_g)


def _compress_kernel(kc_ref, kcn_ref, vc_ref, vcn_ref, pek_ref, w1k_ref, w2k_ref,
                     pev_ref, w1v_ref, w2v_ref, ko_ref, vo_ref):
    tc = kc_ref.shape[1]
    hid_w = N_GROUPS * CMP_HIDDEN
    last = lax.broadcasted_iota(jnp.int32, (tc, 1), 0) == tc - 1

    def one(x_ref, xn_ref, pe_ref, w1_ref, w2_ref, o_ref):
        x = x_ref[0]
        top = jnp.dot((x + pe_ref[0:1, :]).astype(BF16), w1_ref[0], preferred_element_type=F32)
        xb = (x + pe_ref[1:2, :]).astype(BF16)
        bot = jnp.dot(xb, w1_ref[1], preferred_element_type=F32)
        xnb = (xn_ref[0] + pe_ref[1:2, :]).astype(BF16)
        botn = jnp.dot(xnb, w1_ref[1], preferred_element_type=F32)
        shifted = jnp.where(last, botn[0:1, :], pltpu.roll(bot, tc - 1, 0))
        hid = jax.nn.gelu(top + shifted).astype(BF16)
        out = jnp.dot(hid, w2_ref[...], preferred_element_type=F32)
        tail = jnp.where(lax.broadcasted_iota(jnp.int32, (tc, HEAD_DIM), 1) < N_SHIFT_COLS, 1.0, 0.0).astype(BF16)
        for g in range(N_GROUPS):
            o_ref[0, g, :, 0:HEAD_DIM] = out[:, g * HEAD_DIM:(g + 1) * HEAD_DIM].astype(BF16)
            o_ref[0, g, :, HEAD_DIM:2 * HEAD_DIM] = tail

    one(kc_ref, kcn_ref, pek_ref, w1k_ref, w2k_ref, ko_ref)
    one(vc_ref, vcn_ref, pev_ref, w1v_ref, w2v_ref, vo_ref)


def _compress_call(kc_in, vc_in, pek, w1k, w2k, pev, w1v, w2v, tc):
    b, nch, width = kc_in.shape
    nt = nch // tc
    last8 = nch // 8 - 1
    cur = pl.BlockSpec((1, tc, width), lambda bi, i: (bi, i, 0))
    nxt = pl.BlockSpec((1, 8, width), lambda bi, i: (bi, jnp.minimum((i + 1) * (tc // 8), last8), 0))
    out = pl.BlockSpec((1, N_GROUPS, tc, 2 * HEAD_DIM), lambda bi, i: (bi, 0, i, 0))
    oshape = jax.ShapeDtypeStruct((b, N_GROUPS, nch, 2 * HEAD_DIM), BF16)
    return pl.pallas_call(
        _compress_kernel,
        grid=(b, nt),
        in_specs=[cur, nxt, cur, nxt, _const_spec(pek.shape), _const_spec(w1k.shape), _const_spec(w2k.shape),
                  _const_spec(pev.shape), _const_spec(w1v.shape), _const_spec(w2v.shape)],
        out_specs=[out, out],
        out_shape=[oshape, oshape],
        compiler_params=pltpu.CompilerParams(dimension_semantics=("arbitrary", "arbitrary"),
                                             vmem_limit_bytes=VMEM_LIMIT),
        name="compress",
    )(kc_in, kc_in, vc_in, vc_in, pek, w1k, w2k, pev, w1v, w2v)


def _split3(x):
    hi = x.astype(BF16)
    r = x - hi.astype(F32)
    mid = r.astype(BF16)
    lo = (r - mid.astype(F32)).astype(BF16)
    return hi, mid, lo


def _tile_score_bound(qs, kmax):
    qf = qs.astype(F32)
    sq = qf * qf
    nrm2 = None
    for h in range(HPG):
        rows = jnp.sum(sq[:, h * HEAD_DIM:(h + 1) * HEAD_DIM], axis=-1, keepdims=True)
        top = jnp.max(rows, axis=0, keepdims=True)
        nrm2 = top if nrm2 is None else jnp.maximum(nrm2, top)
    return jnp.sqrt(nrm2) * kmax


def _shift_cols(shift):
    hi = shift.astype(BF16).astype(F32)
    lo = (shift - hi).astype(BF16).astype(F32)
    lane = lax.broadcasted_iota(jnp.int32, (1, HEAD_DIM), 1)
    return jnp.where(lane == 0, -hi, jnp.where(lane == 1, -lo, 0.0)).astype(BF16)


def _cmp_kernel(k_top, cw, q_ref, kaug_ref, vaug_ref, gates_ref, aggt_ref, o_ref, bias_ref,
                qaug_ref, e_ref, acc_ref, kmax_ref, pslc_ref):
    qt = pl.program_id(2)
    tq = q_ref.shape[1]
    nch = kaug_ref.shape[2]
    nsel = bias_ref.shape[3]
    m_rows = HPG * tq
    t0 = qt * tq
    tcol = slice(HEAD_DIM, 2 * HEAD_DIM)
    c_last = ((t0 + tq - CMP_BLOCK) >> CMP_SHIFT) // cw
    c_mask = jnp.maximum(c_last - 1, 0)

    @pl.when(qt == 0)
    def _():
        k = kaug_ref[0, 0, :, 0:HEAD_DIM].astype(F32)
        n_ok = lax.broadcasted_iota(jnp.int32, (nch, 1), 0) < nch - 1
        ksq = jnp.where(n_ok, jnp.sum(k * k, axis=-1, keepdims=True), 0.0)
        kmax_ref[...] = jnp.broadcast_to(jnp.sqrt(jnp.max(ksq, axis=0, keepdims=True)), kmax_ref.shape)

    qs = q_ref[0] * SCALE
    for h in range(HPG):
        qaug_ref[h * tq:(h + 1) * tq, 0:HEAD_DIM] = qs[:, h * HEAD_DIM:(h + 1) * HEAD_DIM]
    bound = _tile_score_bound(qs, kmax_ref[0:1, 0:1])
    safe = jnp.max(bound) <= SAFE_SCORE_BOUND

    def scores(c, masked):
        ka = kaug_ref[0, 0, pl.ds(pl.multiple_of(c * cw, cw), cw), :]
        s = lax.dot_general(qaug_ref[...], ka, (((1,), (1,)), ((), ())), preferred_element_type=F32)
        if masked:
            row_t = t0 + (lax.broadcasted_iota(jnp.int32, (m_rows, 1), 0) & (tq - 1))
            n_vis = (row_t - (CMP_BLOCK - 1)) >> CMP_SHIFT
            n = c * cw + lax.broadcasted_iota(jnp.int32, (m_rows, cw), 1)
            s = jnp.where(n <= n_vis, s, NEG_INF)
        return s

    def sweep(fn_full, fn_masked, init):
        carry = lax.fori_loop(0, c_mask, fn_full, init)
        return lax.fori_loop(c_mask, c_last + 1, fn_masked, carry)

    @pl.when(safe)
    def _():
        qaug_ref[:, tcol] = jnp.broadcast_to(_shift_cols(bound), (m_rows, HEAD_DIM))

    @pl.when(jnp.logical_not(safe))
    def _():
        qaug_ref[:, tcol] = jnp.zeros((m_rows, HEAD_DIM), BF16)
        step = lambda masked: (lambda c, m: jnp.maximum(m, jnp.max(scores(c, masked), axis=-1, keepdims=True)))
        mx = sweep(step(False), step(True), jnp.full((m_rows, 1), NEG_INF, F32))
        qaug_ref[:, tcol] = _shift_cols(jnp.where(mx > 0.5 * NEG_INF, mx, 0.0))

    acc_ref[...] = jnp.zeros(acc_ref.shape, F32)

    def chunk(masked):
        def body(c, carry):
            e = jnp.exp(scores(c, masked))
            start = pl.multiple_of(c * cw, cw)
            e_ref[c] = e
            acc_ref[...] += jnp.dot(e.astype(BF16), vaug_ref[0, 0, pl.ds(start, cw), :],
                                    preferred_element_type=F32)
            return carry
        return body

    sweep(chunk(False), chunk(True), 0)
    acc = acc_ref[...]
    l = acc[:, HEAD_DIM:HEAD_DIM + 1]
    rinv = jnp.where(l > 0.0, 1.0 / l, 0.0)
    o = acc[:, 0:HEAD_DIM] * rinv
    gates = gates_ref[0]
    o_ref[0] = jnp.concatenate(
        [o[h * tq:(h + 1) * tq] * gates[:, h:h + 1] for h in range(HPG)], axis=-1)

    pslc_ref[...] = jnp.zeros(pslc_ref.shape, F32)

    def agg_body(c, carry):
        imp = sum(e_ref[c, h * tq:(h + 1) * tq, :] * rinv[h * tq:(h + 1) * tq] for h in range(HPG))
        at = aggt_ref[c]
        pslc_ref[...] += sum(lax.dot_general(at, part, (((1,), (1,)), ((), ())), preferred_element_type=F32)
                             for part in _split3(imp))
        return carry

    lax.fori_loop(0, c_last + 1, agg_body, 0)

    n_slabs = nsel // SUBLANES
    sub = lax.broadcasted_iota(jnp.int32, (SUBLANES, LANES), 0)
    sub_f = sub.astype(F32)
    no_slab = float(n_slabs)
    def tree(fn, xs):
        while len(xs) > 1:
            xs = [fn(xs[i], xs[i + 1]) if i + 1 < len(xs) else xs[i] for i in range(0, len(xs), 2)]
        return xs[0]

    def first_slab(score, m):
        firsts = []
        for g0 in range(0, n_slabs, SUBLANES):
            slab = jnp.full((SUBLANES, LANES), no_slab, F32)
            for r in reversed(range(g0, min(g0 + SUBLANES, n_slabs))):
                slab = jnp.where(score[r] == m, float(r), slab)
            firsts.append(slab)
        return tree(jnp.minimum, firsts)

    for col in range(tq // LANES):
        lanes = slice(col * LANES, (col + 1) * LANES)
        cur = (t0 + col * LANES + lax.broadcasted_iota(jnp.int32, (1, LANES), 1)) >> SEL_SHIFT
        score = []
        for r in range(n_slabs):
            j = sub + r * SUBLANES
            free = (j <= cur - 2) & (j > 0)
            score.append(jnp.where(free, pslc_ref[r * SUBLANES:(r + 1) * SUBLANES, lanes], -jnp.inf))
        for _ in range(max(k_top - N_FORCED, 0)):
            m = jnp.max(tree(jnp.maximum, score), axis=0, keepdims=True)
            slab = first_slab(score, m)
            block = slab * SUBLANES + sub_f
            first = jnp.min(block, axis=0, keepdims=True)
            taken = jnp.where(block == first, slab, -1.0)
            score = [jnp.where(taken == float(r), -jnp.inf, score[r]) for r in range(n_slabs)]
        for r in range(n_slabs):
            pslc_ref[r * SUBLANES:(r + 1) * SUBLANES, lanes] = jnp.where(
                (sub + r * SUBLANES <= cur) & (score[r] == -jnp.inf), 0.0, NEG_INF)
    bias_ref[0, 0] = pslc_ref[...].T.astype(BF16)


def _cmp_call(q_raw, kaug, vaug, gates, aggt, tq, cw):
    b, s, _ = q_raw.shape
    nch = kaug.shape[2]
    nsel = aggt.shape[1]
    assert nch % cw == 0 and tq // CMP_STRIDE < cw
    k_top = min(N_SELECT, nsel)
    gw = HPG * HEAD_DIM
    m_rows = HPG * tq
    qspec = pl.BlockSpec((1, tq, gw), lambda bi, g, i: (bi, i, g))
    kvspec = pl.BlockSpec((1, 1, nch, 2 * HEAD_DIM), lambda bi, g, i: (bi, g, 0, 0))
    gspec = pl.BlockSpec((1, tq, LANES), lambda bi, g, i: (bi, i, g))
    return pl.pallas_call(
        functools.partial(_cmp_kernel, k_top, cw),
        grid=(b, N_GROUPS, s // tq),
        in_specs=[qspec, kvspec, kvspec, gspec, _const_spec(aggt.shape)],
        out_specs=[qspec, pl.BlockSpec((1, 1, tq, nsel), lambda bi, g, i: (bi, g, i, 0))],
        out_shape=[jax.ShapeDtypeStruct((b, s, D_ATTN), F32),
                   jax.ShapeDtypeStruct((b, N_GROUPS, s, nsel), BF16)],
        scratch_shapes=[pltpu.VMEM((m_rows, 2 * HEAD_DIM), BF16),
                        pltpu.VMEM((nch // cw, m_rows, cw), F32),
                        pltpu.VMEM((m_rows, 2 * HEAD_DIM), F32),
                        pltpu.VMEM((8, LANES), F32),
                        pltpu.VMEM((nsel, tq), F32)],
        compiler_params=pltpu.CompilerParams(dimension_semantics=("arbitrary",) * 3,
                                             vmem_limit_bytes=VMEM_LIMIT),
        name="cmp_attn_topk",
    )(q_raw, kaug, vaug, gates, aggt)


def _slc_kernel(nbh, tk, q_ref, bias_ref, kaug_ref, vaug_ref, gates_ref, o_ref,
                qaug_ref, acc_ref, kmax_ref):
    qt = pl.program_id(2)
    tq = q_ref.shape[1]
    m_rows = HPG * tq
    t0 = qt * tq
    n_full = (t0 + tq - 1) // tk
    n_half = qaug_ref.shape[0]
    tiles_per_half = nbh * SEL_BLOCK // tk
    kcol = slice(nbh, nbh + HEAD_DIM)
    tcol = slice(nbh + HEAD_DIM, nbh + 2 * HEAD_DIM)

    @pl.when(qt == 0)
    def _():
        def body(c, mx):
            k = kaug_ref[0, 0, pl.ds(pl.multiple_of(c * tk, tk), tk), kcol].astype(F32)
            return jnp.maximum(mx, jnp.sum(k * k, axis=-1, keepdims=True))
        mx = lax.fori_loop(0, kaug_ref.shape[2] // tk, body, jnp.zeros((tk, 1), F32))
        kmax_ref[...] = jnp.broadcast_to(jnp.sqrt(jnp.max(mx, axis=0, keepdims=True)), kmax_ref.shape)

    qs = q_ref[0] * SCALE
    bias = bias_ref[0, 0]
    for h in range(HPG):
        rows = slice(h * tq, (h + 1) * tq)
        for hf in range(n_half):
            qaug_ref[hf, rows, 0:nbh] = bias[:, hf * nbh:(hf + 1) * nbh]
            qaug_ref[hf, rows, kcol] = qs[:, h * HEAD_DIM:(h + 1) * HEAD_DIM]
    bound = _tile_score_bound(qs, kmax_ref[0:1, 0:1])
    safe = jnp.max(bound) <= SAFE_SCORE_BOUND

    def scores(kt, causal):
        start = pl.multiple_of(kt * tk, tk)
        ka = kaug_ref[0, 0, pl.ds(start, tk), :]
        qa = qaug_ref[kt // tiles_per_half]
        s = lax.dot_general(qa, ka, (((1,), (1,)), ((), ())), preferred_element_type=F32)
        if causal:
            r = lax.broadcasted_iota(jnp.int32, (m_rows, tk), 0) & (tq - 1)
            c = lax.broadcasted_iota(jnp.int32, (m_rows, tk), 1)
            s = jnp.where(c - r <= t0 - kt * tk, s, NEG_INF)
        return s

    @pl.when(safe)
    def _():
        tail = jnp.broadcast_to(_shift_cols(bound), (m_rows, HEAD_DIM))
        for hf in range(n_half):
            qaug_ref[hf, :, tcol] = tail

    @pl.when(jnp.logical_not(safe))
    def _():
        for hf in range(n_half):
            qaug_ref[hf, :, tcol] = jnp.zeros((m_rows, HEAD_DIM), BF16)
        rowmax = lambda kt, causal: jnp.max(scores(kt, causal), axis=-1, keepdims=True)
        mx = lax.fori_loop(0, n_full, lambda kt, m: jnp.maximum(m, rowmax(kt, False)),
                           jnp.full((m_rows, 1), NEG_INF, F32))
        tail = _shift_cols(jnp.maximum(mx, rowmax(n_full, True)))
        for hf in range(n_half):
            qaug_ref[hf, :, tcol] = tail

    acc_ref[...] = jnp.zeros(acc_ref.shape, F32)

    def tile(kt, causal):
        p = jnp.exp(scores(kt, causal)).astype(BF16)
        va = vaug_ref[0, 0, pl.ds(pl.multiple_of(kt * tk, tk), tk), :]
        acc_ref[...] += jnp.dot(p, va, preferred_element_type=F32)

    def body(kt, c):
        tile(kt, False)
        return c

    lax.fori_loop(0, n_full, body, 0)
    tile(n_full, True)
    acc = acc_ref[...]
    o = acc[:, 0:HEAD_DIM] / acc[:, HEAD_DIM:HEAD_DIM + 1]
    gates = gates_ref[0]
    o_ref[0] = jnp.concatenate(
        [o[h * tq:(h + 1) * tq] * gates[:, HPG + h:HPG + h + 1] for h in range(HPG)], axis=-1)


def _slc_call(q_rot, bias, kaug, v, gates, tq, tk, nbh):
    b, s, _ = q_rot.shape
    assert tk % tq == 0 and (nbh * SEL_BLOCK) % tk == 0 and s % tk == 0
    nsel = bias.shape[3]
    gw = HPG * HEAD_DIM
    aug = nbh + 2 * HEAD_DIM
    qspec = pl.BlockSpec((1, tq, gw), lambda bi, g, i: (bi, i, g))
    return pl.pallas_call(
        functools.partial(_slc_kernel, nbh, tk),
        grid=(b, N_GROUPS, s // tq),
        in_specs=[qspec,
                  pl.BlockSpec((1, 1, tq, nsel), lambda bi, g, i: (bi, g, i, 0)),
                  pl.BlockSpec((1, 1, s, aug), lambda bi, g, i: (bi, g, 0, 0)),
                  pl.BlockSpec((1, 1, s, 2 * HEAD_DIM), lambda bi, g, i: (bi, g, 0, 0)),
                  pl.BlockSpec((1, tq, LANES), lambda bi, g, i: (bi, i, g))],
        out_specs=qspec,
        out_shape=jax.ShapeDtypeStruct((b, s, D_ATTN), F32),
        scratch_shapes=[pltpu.VMEM((nsel // nbh, HPG * tq, aug), BF16),
                        pltpu.VMEM((HPG * tq, 2 * HEAD_DIM), F32),
                        pltpu.VMEM((8, LANES), F32)],
        compiler_params=pltpu.CompilerParams(dimension_semantics=("arbitrary",) * 3,
                                             vmem_limit_bytes=VMEM_LIMIT),
        name="slc_attn",
    )(q_rot, bias, kaug, v, gates)


def _win_kernel(q_ref, kaug_ref, vaug_ref, gates_ref, o_ref, qaug_ref, kmax_ref):
    qt = pl.program_id(2)
    tq = q_ref.shape[1]
    span = WINDOW + tq
    m_rows = HPG * tq
    t0 = qt * tq
    tcol = slice(HEAD_DIM, 2 * HEAD_DIM)

    @pl.when(qt == 0)
    def _():
        def body(c, mx):
            k = kaug_ref[0, 0, pl.ds(pl.multiple_of(c * tq, tq), tq), 0:HEAD_DIM].astype(F32)
            return jnp.maximum(mx, jnp.sum(k * k, axis=-1, keepdims=True))
        mx = lax.fori_loop(0, kaug_ref.shape[2] // tq, body, jnp.zeros((tq, 1), F32))
        kmax_ref[...] = jnp.broadcast_to(jnp.sqrt(jnp.max(mx, axis=0, keepdims=True)), kmax_ref.shape)

    start = pl.multiple_of(jnp.maximum(t0 - WINDOW, 0), tq)
    ka = kaug_ref[0, 0, pl.ds(start, span), :]
    qs = q_ref[0] * SCALE
    for h in range(HPG):
        qaug_ref[h * tq:(h + 1) * tq, 0:HEAD_DIM] = qs[:, h * HEAD_DIM:(h + 1) * HEAD_DIM]
    bound = _tile_score_bound(qs, kmax_ref[0:1, 0:1])
    safe = jnp.max(bound) <= SAFE_SCORE_BOUND

    def scores():
        s = lax.dot_general(qaug_ref[...], ka, (((1,), (1,)), ((), ())), preferred_element_type=F32)
        d = (lax.broadcasted_iota(jnp.int32, (m_rows, tq), 1)
             - (lax.broadcasted_iota(jnp.int32, (m_rows, tq), 0) & (tq - 1)))
        blocks = []
        for blk in range(span // tq):
            off = t0 - start - blk * tq
            ok = lax.bitcast_convert_type(off - d, jnp.uint32) < jnp.uint32(WINDOW)
            blocks.append(jnp.where(ok, s[:, blk * tq:(blk + 1) * tq], NEG_INF))
        return jnp.concatenate(blocks, axis=-1)

    @pl.when(safe)
    def _():
        qaug_ref[:, tcol] = jnp.broadcast_to(_shift_cols(bound), (m_rows, HEAD_DIM))

    @pl.when(jnp.logical_not(safe))
    def _():
        qaug_ref[:, tcol] = jnp.zeros((m_rows, HEAD_DIM), BF16)
        qaug_ref[:, tcol] = _shift_cols(jnp.max(scores(), axis=-1, keepdims=True))

    p = jnp.exp(scores()).astype(BF16)
    acc = jnp.dot(p, vaug_ref[0, 0, pl.ds(start, span), :], preferred_element_type=F32)
    o = acc[:, 0:HEAD_DIM] / acc[:, HEAD_DIM:HEAD_DIM + 1]
    gates = gates_ref[0]
    o_ref[0] = jnp.concatenate(
        [o[h * tq:(h + 1) * tq] * gates[:, 2 * HPG + h:2 * HPG + h + 1] for h in range(HPG)], axis=-1)


def _win_call(q_rot, k, v, gates, tq):
    b, s, _ = q_rot.shape
    gw = HPG * HEAD_DIM
    qspec = pl.BlockSpec((1, tq, gw), lambda bi, g, i: (bi, i, g))
    kvspec = pl.BlockSpec((1, 1, s, 2 * HEAD_DIM), lambda bi, g, i: (bi, g, 0, 0))
    return pl.pallas_call(
        _win_kernel,
        grid=(b, N_GROUPS, s // tq),
        in_specs=[qspec, kvspec, kvspec, pl.BlockSpec((1, tq, LANES), lambda bi, g, i: (bi, i, g))],
        out_specs=qspec,
        out_shape=jax.ShapeDtypeStruct((b, s, D_ATTN), F32),
        scratch_shapes=[pltpu.VMEM((HPG * tq, 2 * HEAD_DIM), BF16),
                        pltpu.VMEM((8, LANES), F32)],
        compiler_params=pltpu.CompilerParams(dimension_semantics=("arbitrary",) * 3,
                                             vmem_limit_bytes=VMEM_LIMIT),
        name="win_attn",
    )(q_rot, k, v, gates)


def _out_kernel(x_ref, oc_ref, os_ref, ow_ref, convn_ref, ga_ref, wout_ref, gpost_ref,
                gpre2_ref, wg_ref, wu_ref, wd_ref, gpost2_ref, o_ref):
    attn = oc_ref[...] + os_ref[...] + ow_ref[...]
    an = _rms(attn, ga_ref[...]).astype(BF16)
    h = (jnp.dot(an, wout_ref[0:D_ATTN, :], preferred_element_type=F32)
         + jnp.dot(convn_ref[...], wout_ref[D_ATTN:D_MODEL, :], preferred_element_type=F32))
    x1 = x_ref[...] + _rms(h, gpost_ref[...])
    o_ref[...] = _ffn(x1, gpre2_ref[...], wg_ref, wu_ref, wd_ref, gpost2_ref[...])


def _out_call(x2d, oc, osl, ow, convn, ga, wout, gpost, gpre2, wg, wu, wd, gpost2, tm):
    t = x2d.shape[0]
    row = lambda width: pl.BlockSpec((tm, width), lambda i: (i, 0))
    return pl.pallas_call(
        _out_kernel,
        grid=(t // tm,),
        in_specs=[row(D_MODEL), row(D_ATTN), row(D_ATTN), row(D_ATTN), row(D_CONV),
                  _const_spec((1, D_ATTN)), _const_spec(wout.shape), _const_spec((1, D_MODEL)),
                  _const_spec((1, D_MODEL)), _const_spec(wg.shape), _const_spec(wu.shape),
                  _const_spec(wd.shape), _const_spec((1, D_MODEL))],
        out_specs=row(D_MODEL),
        out_shape=jax.ShapeDtypeStruct(x2d.shape, F32),
        compiler_params=pltpu.CompilerParams(dimension_semantics=("arbitrary",),
                                             vmem_limit_bytes=VMEM_LIMIT),
        name="outproj_ffn2",
    )(x2d, oc, osl, ow, convn, ga, wout, gpost, gpre2, wg, wu, wd, gpost2)


def _prep_w_in(w_in):
    sizes = [D_ATTN] + [D_KV] * 6 + [N_GATES] + [D_CONV] * 3
    cuts = np.cumsum([0] + sizes)
    q, kc, vc, ks, vs, kw, vw, gt, bg, cg, xc = [w_in[:, cuts[i]:cuts[i + 1]] for i in range(len(sizes))]
    gt = gt.reshape(D_MODEL, N_GROUPS, HPG, 3).transpose(0, 1, 3, 2).reshape(D_MODEL, N_GROUPS, 3 * HPG)
    gt = jnp.pad(gt, ((0, 0), (0, 0), (0, LANES - 3 * HPG))).reshape(D_MODEL, GATE_COLS)
    return jnp.concatenate([q, kc, vc, ks, vs, kw, vw, gt, bg, cg, xc], axis=1).astype(BF16)


def _prep_compress(pe, w1, w2):
    half = CMP_BLOCK // 2
    eye = jnp.eye(N_GROUPS, dtype=F32)
    w1r = w1.reshape(2, half, HEAD_DIM, CMP_HIDDEN)
    w1big = jnp.einsum("ptdc,gh->ptgdhc", w1r, eye).reshape(2, half * D_KV, N_GROUPS * CMP_HIDDEN)
    w2big = jnp.einsum("cd,gh->gchd", w2, eye).reshape(N_GROUPS * CMP_HIDDEN, D_KV)
    pe_rows = jnp.broadcast_to(pe.reshape(2, half, 1, HEAD_DIM), (2, half, N_GROUPS, HEAD_DIM))
    return pe_rows.reshape(2, half * D_KV), w1big.astype(BF16), w2big.astype(BF16)


def _agg_matrix(nch, nsel, cw):
    agg_w = np.convolve(np.ones(SEL_RATIO), np.ones(CMP_BLOCK // CMP_STRIDE))
    a = np.zeros((nch, nsel), np.float32)
    for j in range(nsel):
        for o, wgt in enumerate(agg_w):
            c = SEL_RATIO * j + o - (CMP_BLOCK // CMP_STRIDE - 1)
            if 0 <= c < nch - 1:
                a[c, j] = wgt
    a = a.T.reshape(nsel, nch // cw, cw).transpose(1, 0, 2)
    return jnp.asarray(a, BF16)


def _rope_inv_freq_row():
    inv = ROPE_THETA ** (-np.arange(ROPE_HALF, dtype=np.float32) * 2.0 / ROPE_DIM)
    lane = np.arange(LANES) % HEAD_DIM
    row = np.where(lane < ROPE_DIM, inv[lane % ROPE_HALF], 0.0).astype(np.float32)
    return jnp.asarray(row.reshape(1, LANES))


def _forward(x, positions, p, *, tm, tq, tc, nbh, tqs, tks, cw):
    b, s, _ = x.shape
    depth = p["w_in"].shape[0]
    nch = s // CMP_STRIDE
    nsel = s // SEL_BLOCK
    cos, sin = _rope_call(positions.reshape(b, s, 1), _rope_inv_freq_row(), tm)
    agg = _agg_matrix(nch, nsel, cw)
    row = lambda v: v.reshape(1, -1)
    for l in range(depth):
        x2d = x.reshape(b * s, D_MODEL)
        x2d = _ffn_call(x2d, row(p["ffn1_norm_pre"][l]), p["ffn1_w_gate"][l].astype(BF16),
                        p["ffn1_w_up"][l].astype(BF16), p["ffn1_w_down"][l].astype(BF16),
                        row(p["ffn1_norm_post"][l]), tm)
        (q_raw, q_rot, kc_in, vc_in, ksaug, vs, kw, vw, gates, convn) = _inproj_call(
            x2d.reshape(b, s, D_MODEL), cos, sin, row(p["mix_norm_pre"][l]), _prep_w_in(p["w_in"][l]),
            p["conv_w"][l], row(p["conv_out_norm"][l]), tm, nbh)
        pek, w1k, w2k = _prep_compress(p["cmp_pe_k"][l], p["cmp_w1_k"][l], p["cmp_w2_k"][l])
        pev, w1v, w2v = _prep_compress(p["cmp_pe_v"][l], p["cmp_w1_v"][l], p["cmp_w2_v"][l])
        kcmp, vcmp = _compress_call(kc_in.reshape(b, nch, CMP_STRIDE * D_KV),
                                    vc_in.reshape(b, nch, CMP_STRIDE * D_KV),
                                    pek, w1k, w2k, pev, w1v, w2v, tc)
        o_cmp, bias = _cmp_call(q_raw, kcmp, vcmp, gates, agg, tq, cw)
        o_slc = _slc_call(q_rot, bias, ksaug, vs, gates, tqs, tks, nbh)
        o_win = _win_call(q_rot, kw, vw, gates, tq)
        flat = lambda a: a.reshape(b * s, a.shape[-1])
        x2d = _out_call(x2d, flat(o_cmp), flat(o_slc), flat(o_win), flat(convn),
                        row(p["attn_out_norm"][l]), p["w_out"][l].astype(BF16), row(p["mix_norm_post"][l]),
                        row(p["ffn2_norm_pre"][l]), p["ffn2_w_gate"][l].astype(BF16),
                        p["ffn2_w_up"][l].astype(BF16), p["ffn2_w_down"][l].astype(BF16),
                        row(p["ffn2_norm_post"][l]), tm)
        x = x2d.reshape(b, s, D_MODEL)
    return x


def kernel(x, positions, ffn1_norm_pre, ffn1_w_gate, ffn1_w_up, ffn1_w_down, ffn1_norm_post, mix_norm_pre, w_in, cmp_pe_k, cmp_w1_k, cmp_w2_k, cmp_pe_v, cmp_w1_v, cmp_w2_v, conv_w, attn_out_norm, conv_out_norm, w_out, mix_norm_post, ffn2_norm_pre, ffn2_w_gate, ffn2_w_up, ffn2_w_down, ffn2_norm_post):
    params = dict(
        ffn1_norm_pre=ffn1_norm_pre, ffn1_w_gate=ffn1_w_gate, ffn1_w_up=ffn1_w_up, ffn1_w_down=ffn1_w_down,
        ffn1_norm_post=ffn1_norm_post, mix_norm_pre=mix_norm_pre, w_in=w_in,
        cmp_pe_k=cmp_pe_k, cmp_w1_k=cmp_w1_k, cmp_w2_k=cmp_w2_k,
        cmp_pe_v=cmp_pe_v, cmp_w1_v=cmp_w1_v, cmp_w2_v=cmp_w2_v,
        conv_w=conv_w, attn_out_norm=attn_out_norm, conv_out_norm=conv_out_norm, w_out=w_out,
        mix_norm_post=mix_norm_post, ffn2_norm_pre=ffn2_norm_pre, ffn2_w_gate=ffn2_w_gate,
        ffn2_w_up=ffn2_w_up, ffn2_w_down=ffn2_w_down, ffn2_norm_post=ffn2_norm_post)
    return _forward(x, positions, params, tm=512, tq=256, tc=256, nbh=128, tqs=512, tks=1024, cw=256)
```

```python
import functools
import math

import numpy as np
import jax
import jax.numpy as jnp
from jax import lax
from jax.experimental import pallas as pl
from jax.experimental.pallas import tpu as pltpu

D_MODEL = 1024
N_HEADS = 8
HEAD_DIM = 64
N_GROUPS = 2
HPG = N_HEADS // N_GROUPS
D_ATTN = N_HEADS * HEAD_DIM
D_KV = N_GROUPS * HEAD_DIM
D_CONV = D_MODEL - D_ATTN
CONV_WIDTH = 3
CMP_BLOCK = 32
CMP_STRIDE = 16
CMP_SHIFT = 4
CMP_HIDDEN = 256
SEL_BLOCK = 64
SEL_SHIFT = 6
SEL_RATIO = SEL_BLOCK // CMP_STRIDE
N_SELECT = 16
N_FORCED = 3
WINDOW = 512
ROPE_THETA = 500000.0
ROPE_DIM = HEAD_DIM // 4
ROPE_HALF = ROPE_DIM // 2
D_FF = 2816
N_GATES = 3 * N_HEADS
EPS = 1e-6
NEG_INF = -1e30
FORCE_SCORE = 1e9
SCALE = 1.0 / math.sqrt(HEAD_DIM)

LANES = 128
SUBLANES = 8
GATE_COLS = N_GROUPS * LANES
VMEM_LIMIT = 56 * 1024 * 1024
FF_CHUNKS = ((0, 768), (768, 1536), (1536, 2304), (2304, 2816))
N_SHIFT_COLS = 2
SAFE_SCORE_BOUND = 40.0

F32 = jnp.float32
BF16 = jnp.bfloat16


def _const_spec(shape):
    nd = len(shape)
    return pl.BlockSpec(shape, lambda *_: (0,) * nd, pipeline_mode=pl.Buffered(1))


def _rms(x, g):
    ms = jnp.mean(x * x, axis=-1, keepdims=True)
    return x * lax.rsqrt(ms + EPS) * g


def _ffn(x, g_pre, wg_ref, wu_ref, wd_ref, g_post):
    h = _rms(x, g_pre).astype(BF16)
    d = None
    for c0, c1 in FF_CHUNKS:
        gate = jnp.dot(h, wg_ref[:, c0:c1], preferred_element_type=F32)
        up = jnp.dot(h, wu_ref[:, c0:c1], preferred_element_type=F32)
        a = (gate * jax.nn.sigmoid(gate) * up).astype(BF16)
        part = jnp.dot(a, wd_ref[c0:c1, :], preferred_element_type=F32)
        d = part if d is None else d + part
    return x + 0.5 * _rms(d, g_post)


def _ffn_kernel(x_ref, gpre_ref, wg_ref, wu_ref, wd_ref, gpost_ref, o_ref):
    o_ref[...] = _ffn(x_ref[...], gpre_ref[...], wg_ref, wu_ref, wd_ref, gpost_ref[...])


def _ffn_call(x2d, g_pre, wg, wu, wd, g_post, tm):
    t = x2d.shape[0]
    row = pl.BlockSpec((tm, D_MODEL), lambda i: (i, 0))
    return pl.pallas_call(
        _ffn_kernel,
        grid=(t // tm,),
        in_specs=[row, _const_spec((1, D_MODEL)), _const_spec(wg.shape), _const_spec(wu.shape),
                  _const_spec(wd.shape), _const_spec((1, D_MODEL))],
        out_specs=row,
        out_shape=jax.ShapeDtypeStruct(x2d.shape, F32),
        compiler_params=pltpu.CompilerParams(dimension_semantics=("arbitrary",),
                                             vmem_limit_bytes=VMEM_LIMIT),
        name="ffn1",
    )(x2d, g_pre, wg, wu, wd, g_post)


_C_Q = 0
_C_KC = _C_Q + D_ATTN
_C_VC = _C_KC + D_KV
_C_KS = _C_VC + D_KV
_C_VS = _C_KS + D_KV
_C_KW = _C_VS + D_KV
_C_VW = _C_KW + D_KV
_C_GATE = _C_VW + D_KV
_C_BG = _C_GATE + GATE_COLS
_C_CG = _C_BG + D_CONV
_C_XC = _C_CG + D_CONV
_C_END = _C_XC + D_CONV


def _rope_kernel(pos_ref, invf_ref, cos_ref, sin_ref):
    ang = pos_ref[0].astype(F32) * invf_ref[...]
    cos_ref[0] = jnp.cos(ang)
    sin_ref[0] = jnp.sin(ang)


def _rope_call(pos3, invf, tm):
    b, s, _ = pos3.shape
    row = lambda width: pl.BlockSpec((1, tm, width), lambda bi, i: (bi, i, 0))
    shape = jax.ShapeDtypeStruct((b, s, LANES), F32)
    return pl.pallas_call(
        _rope_kernel,
        grid=(b, s // tm),
        in_specs=[row(1), _const_spec((1, LANES))],
        out_specs=[row(LANES), row(LANES)],
        out_shape=[shape, shape],
        compiler_params=pltpu.CompilerParams(dimension_semantics=("arbitrary", "arbitrary"),
                                             vmem_limit_bytes=VMEM_LIMIT),
        name="rope_tables",
    )(pos3, invf)


def _inproj_kernel(nbh, x_ref, cos_ref, sin_ref, g_ref, w_ref, convw_ref, convg_ref,
                   qraw_ref, qrot_ref, kc_ref, vc_ref, ksaug_ref, vs_ref, kw_ref, vw_ref,
                   gates_ref, convn_ref, ubuf_ref):
    i = pl.program_id(1)
    tm = x_ref.shape[1]
    h = _rms(x_ref[0], g_ref[...]).astype(BF16)

    def proj(c0, c1):
        return jnp.dot(h, w_ref[:, c0:c1], preferred_element_type=F32)

    cos = cos_ref[0]
    sin = sin_ref[0]
    lane = lax.broadcasted_iota(jnp.int32, (1, LANES), 1) & (HEAD_DIM - 1)
    s_lo = jnp.where(lane < ROPE_HALF, -sin, 0.0)
    s_hi = jnp.where(lane >= ROPE_HALF, sin, 0.0)

    def rope(z):
        return (z * cos + pltpu.roll(z, LANES - ROPE_HALF, 1) * s_lo
                + pltpu.roll(z, ROPE_HALF, 1) * s_hi)

    for c in range(0, D_ATTN // LANES, 2):
        zq2 = proj(_C_Q + c * LANES, _C_Q + (c + 2) * LANES)
        for cc in (c, c + 1):
            zq = zq2[:, (cc - c) * LANES:(cc - c + 1) * LANES]
            qraw_ref[0, :, cc * LANES:(cc + 1) * LANES] = zq.astype(BF16)
            qrot_ref[0, :, cc * LANES:(cc + 1) * LANES] = rope(zq).astype(BF16)

    kvc = proj(_C_KC, _C_KS)
    kc_ref[0] = kvc[:, 0:D_KV]
    vc_ref[0] = kvc[:, D_KV:2 * D_KV]
    kvs = proj(_C_KS, _C_KW)
    ks = rope(kvs[:, 0:D_KV]).astype(BF16)
    vs = kvs[:, D_KV:2 * D_KV].astype(BF16)
    kvw = proj(_C_KW, _C_GATE)
    kw = rope(kvw[:, 0:D_KV]).astype(BF16)
    vw = kvw[:, D_KV:2 * D_KV].astype(BF16)
    row_blk = ((i * tm + lax.broadcasted_iota(jnp.int32, (tm, nbh), 0)) >> SEL_SHIFT) & (nbh - 1)
    onehot = jnp.where(row_blk == lax.broadcasted_iota(jnp.int32, (tm, nbh), 1), 1.0, 0.0).astype(BF16)
    tail = jnp.where(lax.broadcasted_iota(jnp.int32, (tm, HEAD_DIM), 1) < N_SHIFT_COLS, 1.0, 0.0).astype(BF16)
    for g in range(N_GROUPS):
        sl = slice(g * HEAD_DIM, (g + 1) * HEAD_DIM)
        ksaug_ref[0, g, :, 0:nbh] = onehot
        ksaug_ref[0, g, :, nbh:nbh + HEAD_DIM] = ks[:, sl]
        ksaug_ref[0, g, :, nbh + HEAD_DIM:nbh + 2 * HEAD_DIM] = tail
        vs_ref[0, g, :, 0:HEAD_DIM] = vs[:, sl]
        vs_ref[0, g, :, HEAD_DIM:2 * HEAD_DIM] = tail
        kw_ref[0, g, :, 0:HEAD_DIM] = kw[:, sl]
        kw_ref[0, g, :, HEAD_DIM:2 * HEAD_DIM] = tail
        vw_ref[0, g, :, 0:HEAD_DIM] = vw[:, sl]
        vw_ref[0, g, :, HEAD_DIM:2 * HEAD_DIM] = tail

    gates_ref[0] = jax.nn.sigmoid(proj(_C_GATE, _C_BG))

    u = proj(_C_CG, _C_XC) * proj(_C_XC, _C_END)

    @pl.when(i == 0)
    def _():
        ubuf_ref[0:8, :] = jnp.zeros((8, D_CONV), F32)

    @pl.when(i > 0)
    def _():
        ubuf_ref[0:8, :] = ubuf_ref[tm:tm + 8, :]

    ubuf_ref[8:tm + 8, :] = u
    w = convw_ref[...]
    y = (w[2:3, :] * u + w[1:2, :] * ubuf_ref[7:tm + 7, :] + w[0:1, :] * ubuf_ref[6:tm + 6, :])
    conv = proj(_C_BG, _C_CG) * y
    convn_ref[0] = _rms(conv, convg_ref[...]).astype(BF16)


def _inproj_call(x, cos, sin, g_pre, w_in, conv_w, conv_g, tm, nbh):
    b, s, _ = x.shape
    grid = (b, s // tm)
    row = lambda width: pl.BlockSpec((1, tm, width), lambda bi, i: (bi, i, 0))
    grp = lambda width: pl.BlockSpec((1, N_GROUPS, tm, width), lambda bi, i: (bi, 0, i, 0))
    kern = functools.partial(_inproj_kernel, nbh)
    return pl.pallas_call(
        kern,
        grid=grid,
        in_specs=[row(D_MODEL), row(LANES), row(LANES), _const_spec((1, D_MODEL)), _const_spec(w_in.shape),
                  _const_spec((CONV_WIDTH, D_CONV)), _const_spec((1, D_CONV))],
        out_specs=[row(D_ATTN), row(D_ATTN), row(D_KV), row(D_KV), grp(nbh + 2 * HEAD_DIM), grp(2 * HEAD_DIM),
                   grp(2 * HEAD_DIM), grp(2 * HEAD_DIM), row(GATE_COLS), row(D_CONV)],
        out_shape=[
            jax.ShapeDtypeStruct((b, s, D_ATTN), BF16),
            jax.ShapeDtypeStruct((b, s, D_ATTN), BF16),
            jax.ShapeDtypeStruct((b, s, D_KV), F32),
            jax.ShapeDtypeStruct((b, s, D_KV), F32),
            jax.ShapeDtypeStruct((b, N_GROUPS, s, nbh + 2 * HEAD_DIM), BF16),
            jax.ShapeDtypeStruct((b, N_GROUPS, s, 2 * HEAD_DIM), BF16),
            jax.ShapeDtypeStruct((b, N_GROUPS, s, 2 * HEAD_DIM), BF16),
            jax.ShapeDtypeStruct((b, N_GROUPS, s, 2 * HEAD_DIM), BF16),
            jax.ShapeDtypeStruct((b, s, GATE_COLS), F32),
            jax.ShapeDtypeStruct((b, s, D_CONV), BF16),
        ],
        scratch_shapes=[pltpu.VMEM((tm + 8, D_CONV), F32)],
        compiler_params=pltpu.CompilerParams(dimension_semantics=("arbitrary", "arbitrary"),
                                             vmem_limit_bytes=VMEM_LIMIT),
        name="inproj",
    )(x, cos, sin, g_pre, w_in, conv_w, conv_g)


def _compress_kernel(kc_ref, kcn_ref, vc_ref, vcn_ref, pek_ref, w1k_ref, w2k_ref,
                     pev_ref, w1v_ref, w2v_ref, ko_ref, vo_ref):
    tc = kc_ref.shape[1]
    hid_w = N_GROUPS * CMP_HIDDEN
    last = lax.broadcasted_iota(jnp.int32, (tc, 1), 0) == tc - 1

    def one(x_ref, xn_ref, pe_ref, w1_ref, w2_ref, o_ref):
        x = x_ref[0]
        top = jnp.dot((x + pe_ref[0:1, :]).astype(BF16), w1_ref[0], preferred_element_type=F32)
        xb = (x + pe_ref[1:2, :]).astype(BF16)
        bot = jnp.dot(xb, w1_ref[1], preferred_element_type=F32)
        xnb = (xn_ref[0] + pe_ref[1:2, :]).astype(BF16)
        botn = jnp.dot(xnb, w1_ref[1], preferred_element_type=F32)
        shifted = jnp.where(last, botn[0:1, :], pltpu.roll(bot, tc - 1, 0))
        hid = jax.nn.gelu(top + shifted).astype(BF16)
        out = jnp.dot(hid, w2_ref[...], preferred_element_type=F32)
        tail = jnp.where(lax.broadcasted_iota(jnp.int32, (tc, HEAD_DIM), 1) < N_SHIFT_COLS, 1.0, 0.0).astype(BF16)
        for g in range(N_GROUPS):
            o_ref[0, g, :, 0:HEAD_DIM] = out[:, g * HEAD_DIM:(g + 1) * HEAD_DIM].astype(BF16)
            o_ref[0, g, :, HEAD_DIM:2 * HEAD_DIM] = tail

    one(kc_ref, kcn_ref, pek_ref, w1k_ref, w2k_ref, ko_ref)
    one(vc_ref, vcn_ref, pev_ref, w1v_ref, w2v_ref, vo_ref)


def _compress_call(kc_in, vc_in, pek, w1k, w2k, pev, w1v, w2v, tc):
    b, nch, width = kc_in.shape
    nt = nch // tc
    last8 = nch // 8 - 1
    cur = pl.BlockSpec((1, tc, width), lambda bi, i: (bi, i, 0))
    nxt = pl.BlockSpec((1, 8, width), lambda bi, i: (bi, jnp.minimum((i + 1) * (tc // 8), last8), 0))
    out = pl.BlockSpec((1, N_GROUPS, tc, 2 * HEAD_DIM), lambda bi, i: (bi, 0, i, 0))
    oshape = jax.ShapeDtypeStruct((b, N_GROUPS, nch, 2 * HEAD_DIM), BF16)
    return pl.pallas_call(
        _compress_kernel,
        grid=(b, nt),
        in_specs=[cur, nxt, cur, nxt, _const_spec(pek.shape), _const_spec(w1k.shape), _const_spec(w2k.shape),
                  _const_spec(pev.shape), _const_spec(w1v.shape), _const_spec(w2v.shape)],
        out_specs=[out, out],
        out_shape=[oshape, oshape],
        compiler_params=pltpu.CompilerParams(dimension_semantics=("arbitrary", "arbitrary"),
                                             vmem_limit_bytes=VMEM_LIMIT),
        name="compress",
    )(kc_in, kc_in, vc_in, vc_in, pek, w1k, w2k, pev, w1v, w2v)


def _split3(x):
    hi = x.astype(BF16)
    r = x - hi.astype(F32)
    mid = r.astype(BF16)
    lo = (r - mid.astype(F32)).astype(BF16)
    return hi, mid, lo


def _tile_score_bound(qs, kmax):
    qf = qs.astype(F32)
    sq = qf * qf
    nrm2 = None
    for h in range(HPG):
        rows = jnp.sum(sq[:, h * HEAD_DIM:(h + 1) * HEAD_DIM], axis=-1, keepdims=True)
        top = jnp.max(rows, axis=0, keepdims=True)
        nrm2 = top if nrm2 is None else jnp.maximum(nrm2, top)
    return jnp.sqrt(nrm2) * kmax


def _shift_cols(shift):
    hi = shift.astype(BF16).astype(F32)
    lo = (shift - hi).astype(BF16).astype(F32)
    lane = lax.broadcasted_iota(jnp.int32, (1, HEAD_DIM), 1)
    return jnp.where(lane == 0, -hi, jnp.where(lane == 1, -lo, 0.0)).astype(BF16)


def _cmp_kernel(k_top, cw, q_ref, kaug_ref, vaug_ref, gates_ref, aggt_ref, o_ref, bias_ref,
                qaug_ref, e_ref, acc_ref, kmax_ref, pslc_ref):
    qt = pl.program_id(2)
    tq = q_ref.shape[1]
    nch = kaug_ref.shape[2]
    nsel = bias_ref.shape[3]
    m_rows = HPG * tq
    t0 = qt * tq
    tcol = slice(HEAD_DIM, 2 * HEAD_DIM)
    c_last = ((t0 + tq - CMP_BLOCK) >> CMP_SHIFT) // cw
    c_mask = jnp.maximum(c_last - 1, 0)

    @pl.when(qt == 0)
    def _():
        k = kaug_ref[0, 0, :, 0:HEAD_DIM].astype(F32)
        n_ok = lax.broadcasted_iota(jnp.int32, (nch, 1), 0) < nch - 1
        ksq = jnp.where(n_ok, jnp.sum(k * k, axis=-1, keepdims=True), 0.0)
        kmax_ref[...] = jnp.broadcast_to(jnp.sqrt(jnp.max(ksq, axis=0, keepdims=True)), kmax_ref.shape)

    qs = q_ref[0] * SCALE
    for h in range(HPG):
        qaug_ref[h * tq:(h + 1) * tq, 0:HEAD_DIM] = qs[:, h * HEAD_DIM:(h + 1) * HEAD_DIM]
    bound = _tile_score_bound(qs, kmax_ref[0:1, 0:1])
    safe = jnp.max(bound) <= SAFE_SCORE_BOUND

    def scores(c, masked):
        ka = kaug_ref[0, 0, pl.ds(pl.multiple_of(c * cw, cw), cw), :]
        s = lax.dot_general(qaug_ref[...], ka, (((1,), (1,)), ((), ())), preferred_element_type=F32)
        if masked:
            row_t = t0 + (lax.broadcasted_iota(jnp.int32, (m_rows, 1), 0) & (tq - 1))
            n_vis = (row_t - (CMP_BLOCK - 1)) >> CMP_SHIFT
            n = c * cw + lax.broadcasted_iota(jnp.int32, (m_rows, cw), 1)
            s = jnp.where(n <= n_vis, s, NEG_INF)
        return s

    def sweep(fn_full, fn_masked, init):
        carry = lax.fori_loop(0, c_mask, fn_full, init)
        return lax.fori_loop(c_mask, c_last + 1, fn_masked, carry)

    @pl.when(safe)
    def _():
        qaug_ref[:, tcol] = jnp.broadcast_to(_shift_cols(bound), (m_rows, HEAD_DIM))

    @pl.when(jnp.logical_not(safe))
    def _():
        qaug_ref[:, tcol] = jnp.zeros((m_rows, HEAD_DIM), BF16)
        step = lambda masked: (lambda c, m: jnp.maximum(m, jnp.max(scores(c, masked), axis=-1, keepdims=True)))
        mx = sweep(step(False), step(True), jnp.full((m_rows, 1), NEG_INF, F32))
        qaug_ref[:, tcol] = _shift_cols(jnp.where(mx > 0.5 * NEG_INF, mx, 0.0))

    acc_ref[...] = jnp.zeros(acc_ref.shape, F32)

    def chunk(masked):
        def body(c, carry):
            e = jnp.exp(scores(c, masked))
            start = pl.multiple_of(c * cw, cw)
            e_ref[c] = e
            acc_ref[...] += jnp.dot(e.astype(BF16), vaug_ref[0, 0, pl.ds(start, cw), :],
                                    preferred_element_type=F32)
            return carry
        return body

    sweep(chunk(False), chunk(True), 0)
    acc = acc_ref[...]
    l = acc[:, HEAD_DIM:HEAD_DIM + 1]
    rinv = jnp.where(l > 0.0, 1.0 / l, 0.0)
    o = acc[:, 0:HEAD_DIM] * rinv
    gates = gates_ref[0]
    o_ref[0] = jnp.concatenate(
        [o[h * tq:(h + 1) * tq] * gates[:, h:h + 1] for h in range(HPG)], axis=-1)

    pslc_ref[...] = jnp.zeros(pslc_ref.shape, F32)

    def agg_body(c, carry):
        imp = sum(e_ref[c, h * tq:(h + 1) * tq, :] * rinv[h * tq:(h + 1) * tq] for h in range(HPG))
        at = aggt_ref[c]
        pslc_ref[...] += sum(lax.dot_general(at, part, (((1,), (1,)), ((), ())), preferred_element_type=F32)
                             for part in _split3(imp))
        return carry

    lax.fori_loop(0, c_last + 1, agg_body, 0)

    n_slabs = nsel // SUBLANES
    sub = lax.broadcasted_iota(jnp.int32, (SUBLANES, LANES), 0)
    sub_f = sub.astype(F32)
    no_slab = float(n_slabs)
    def tree(fn, xs):
        while len(xs) > 1:
            xs = [fn(xs[i], xs[i + 1]) if i + 1 < len(xs) else xs[i] for i in range(0, len(xs), 2)]
        return xs[0]

    def first_slab(score, m):
        firsts = []
        for g0 in range(0, n_slabs, SUBLANES):
            slab = jnp.full((SUBLANES, LANES), no_slab, F32)
            for r in reversed(range(g0, min(g0 + SUBLANES, n_slabs))):
                slab = jnp.where(score[r] == m, float(r), slab)
            firsts.append(slab)
        return tree(jnp.minimum, firsts)

    for col in range(tq // LANES):
        lanes = slice(col * LANES, (col + 1) * LANES)
        cur = (t0 + col * LANES + lax.broadcasted_iota(jnp.int32, (1, LANES), 1)) >> SEL_SHIFT
        score = []
        for r in range(n_slabs):
            j = sub + r * SUBLANES
            free = (j <= cur - 2) & (j > 0)
            score.append(jnp.where(free, pslc_ref[r * SUBLANES:(r + 1) * SUBLANES, lanes], -jnp.inf))
        for _ in range(max(k_top - N_FORCED, 0)):
            m = jnp.max(tree(jnp.maximum, score), axis=0, keepdims=True)
            slab = first_slab(score, m)
            block = slab * SUBLANES + sub_f
            first = jnp.min(block, axis=0, keepdims=True)
            taken = jnp.where(block == first, slab, -1.0)
            score = [jnp.where(taken == float(r), -jnp.inf, score[r]) for r in range(n_slabs)]
        for r in range(n_slabs):
            pslc_ref[r * SUBLANES:(r + 1) * SUBLANES, lanes] = jnp.where(
                (sub + r * SUBLANES <= cur) & (score[r] == -jnp.inf), 0.0, NEG_INF)
    bias_ref[0, 0] = pslc_ref[...].T.astype(BF16)


def _cmp_call(q_raw, kaug, vaug, gates, aggt, tq, cw):
    b, s, _ = q_raw.shape
    nch = kaug.shape[2]
    nsel = aggt.shape[1]
    assert nch % cw == 0 and tq // CMP_STRIDE < cw
    k_top = min(N_SELECT, nsel)
    gw = HPG * HEAD_DIM
    m_rows = HPG * tq
    qspec = pl.BlockSpec((1, tq, gw), lambda bi, g, i: (bi, i, g))
    kvspec = pl.BlockSpec((1, 1, nch, 2 * HEAD_DIM), lambda bi, g, i: (bi, g, 0, 0))
    gspec = pl.BlockSpec((1, tq, LANES), lambda bi, g, i: (bi, i, g))
    return pl.pallas_call(
        functools.partial(_cmp_kernel, k_top, cw),
        grid=(b, N_GROUPS, s // tq),
        in_specs=[qspec, kvspec, kvspec, gspec, _const_spec(aggt.shape)],
        out_specs=[qspec, pl.BlockSpec((1, 1, tq, nsel), lambda bi, g, i: (bi, g, i, 0))],
        out_shape=[jax.ShapeDtypeStruct((b, s, D_ATTN), F32),
                   jax.ShapeDtypeStruct((b, N_GROUPS, s, nsel), BF16)],
        scratch_shapes=[pltpu.VMEM((m_rows, 2 * HEAD_DIM), BF16),
                        pltpu.VMEM((nch // cw, m_rows, cw), F32),
                        pltpu.VMEM((m_rows, 2 * HEAD_DIM), F32),
                        pltpu.VMEM((8, LANES), F32),
                        pltpu.VMEM((nsel, tq), F32)],
        compiler_params=pltpu.CompilerParams(dimension_semantics=("arbitrary",) * 3,
                                             vmem_limit_bytes=VMEM_LIMIT),
        name="cmp_attn_topk",
    )(q_raw, kaug, vaug, gates, aggt)


def _slc_kernel(nbh, tk, q_ref, bias_ref, kaug_ref, vaug_ref, gates_ref, o_ref,
                qaug_ref, acc_ref, kmax_ref):
    qt = pl.program_id(2)
    tq = q_ref.shape[1]
    m_rows = HPG * tq
    t0 = qt * tq
    n_wide = t0 // tk
    n_narrow = (t0 - n_wide * tk) // tq
    n_half = qaug_ref.shape[0]
    kcol = slice(nbh, nbh + HEAD_DIM)
    tcol = slice(nbh + HEAD_DIM, nbh + 2 * HEAD_DIM)

    @pl.when(qt == 0)
    def _():
        def body(c, mx):
            k = kaug_ref[0, 0, pl.ds(pl.multiple_of(c * tk, tk), tk), kcol].astype(F32)
            return jnp.maximum(mx, jnp.sum(k * k, axis=-1, keepdims=True))
        mx = lax.fori_loop(0, kaug_ref.shape[2] // tk, body, jnp.zeros((tk, 1), F32))
        kmax_ref[...] = jnp.broadcast_to(jnp.sqrt(jnp.max(mx, axis=0, keepdims=True)), kmax_ref.shape)

    qs = q_ref[0] * SCALE
    bias = bias_ref[0, 0]
    for h in range(HPG):
        rows = slice(h * tq, (h + 1) * tq)
        for hf in range(n_half):
            qaug_ref[hf, rows, 0:nbh] = bias[:, hf * nbh:(hf + 1) * nbh]
            qaug_ref[hf, rows, kcol] = qs[:, h * HEAD_DIM:(h + 1) * HEAD_DIM]
    bound = _tile_score_bound(qs, kmax_ref[0:1, 0:1])
    safe = jnp.max(bound) <= SAFE_SCORE_BOUND

    def scores(start, width, diagonal):
        ka = kaug_ref[0, 0, pl.ds(start, width), :]
        qa = qaug_ref[start // (nbh * SEL_BLOCK)]
        s = lax.dot_general(qa, ka, (((1,), (1,)), ((), ())), preferred_element_type=F32)
        if diagonal:
            r = lax.broadcasted_iota(jnp.int32, (m_rows, width), 0) & (tq - 1)
            c = lax.broadcasted_iota(jnp.int32, (m_rows, width), 1)
            s = jnp.where(c <= r, s, NEG_INF)
        return s

    def sweep(fn, carry):
        carry = lax.fori_loop(
            0, n_wide, lambda i, c: fn(pl.multiple_of(i * tk, tk), tk, False, c), carry)
        carry = lax.fori_loop(
            0, n_narrow, lambda i, c: fn(pl.multiple_of(n_wide * tk + i * tq, tq), tq, False, c), carry)
        return fn(pl.multiple_of(t0, tq), tq, True, carry)

    @pl.when(safe)
    def _():
        tail = jnp.broadcast_to(_shift_cols(bound), (m_rows, HEAD_DIM))
        for hf in range(n_half):
            qaug_ref[hf, :, tcol] = tail

    @pl.when(jnp.logical_not(safe))
    def _():
        for hf in range(n_half):
            qaug_ref[hf, :, tcol] = jnp.zeros((m_rows, HEAD_DIM), BF16)
        mx = sweep(lambda st, w, dg, m: jnp.maximum(m, jnp.max(scores(st, w, dg), axis=-1, keepdims=True)),
                   jnp.full((m_rows, 1), NEG_INF, F32))
        tail = _shift_cols(mx)
        for hf in range(n_half):
            qaug_ref[hf, :, tcol] = tail

    acc_ref[...] = jnp.zeros(acc_ref.shape, F32)

    def tile(start, width, diagonal, carry):
        p = jnp.exp(scores(start, width, diagonal)).astype(BF16)
        acc_ref[...] += jnp.dot(p, vaug_ref[0, 0, pl.ds(start, width), :], preferred_element_type=F32)
        return carry

    sweep(tile, 0)
    acc = acc_ref[...]
    o = acc[:, 0:HEAD_DIM] / acc[:, HEAD_DIM:HEAD_DIM + 1]
    gates = gates_ref[0]
    o_ref[0] = jnp.concatenate(
        [o[h * tq:(h + 1) * tq] * gates[:, HPG + h:HPG + h + 1] for h in range(HPG)], axis=-1)


def _slc_call(q_rot, bias, kaug, v, gates, tq, tk, nbh):
    b, s, _ = q_rot.shape
    assert tk % tq == 0 and (nbh * SEL_BLOCK) % tk == 0 and s % tk == 0
    nsel = bias.shape[3]
    gw = HPG * HEAD_DIM
    aug = nbh + 2 * HEAD_DIM
    qspec = pl.BlockSpec((1, tq, gw), lambda bi, g, i: (bi, i, g))
    return pl.pallas_call(
        functools.partial(_slc_kernel, nbh, tk),
        grid=(b, N_GROUPS, s // tq),
        in_specs=[qspec,
                  pl.BlockSpec((1, 1, tq, nsel), lambda bi, g, i: (bi, g, i, 0)),
                  pl.BlockSpec((1, 1, s, aug), lambda bi, g, i: (bi, g, 0, 0)),
                  pl.BlockSpec((1, 1, s, 2 * HEAD_DIM), lambda bi, g, i: (bi, g, 0, 0)),
                  pl.BlockSpec((1, tq, LANES), lambda bi, g, i: (bi, i, g))],
        out_specs=qspec,
        out_shape=jax.ShapeDtypeStruct((b, s, D_ATTN), F32),
        scratch_shapes=[pltpu.VMEM((nsel // nbh, HPG * tq, aug), BF16),
                        pltpu.VMEM((HPG * tq, 2 * HEAD_DIM), F32),
                        pltpu.VMEM((8, LANES), F32)],
        compiler_params=pltpu.CompilerParams(dimension_semantics=("arbitrary",) * 3,
                                             vmem_limit_bytes=VMEM_LIMIT),
        name="slc_attn",
    )(q_rot, bias, kaug, v, gates)


def _win_kernel(q_ref, kaug_ref, vaug_ref, gates_ref, o_ref, qaug_ref, kmax_ref):
    qt = pl.program_id(2)
    tq = q_ref.shape[1]
    span = WINDOW + tq
    m_rows = HPG * tq
    t0 = qt * tq
    tcol = slice(HEAD_DIM, 2 * HEAD_DIM)

    @pl.when(qt == 0)
    def _():
        def body(c, mx):
            k = kaug_ref[0, 0, pl.ds(pl.multiple_of(c * tq, tq), tq), 0:HEAD_DIM].astype(F32)
            return jnp.maximum(mx, jnp.sum(k * k, axis=-1, keepdims=True))
        mx = lax.fori_loop(0, kaug_ref.shape[2] // tq, body, jnp.zeros((tq, 1), F32))
        kmax_ref[...] = jnp.broadcast_to(jnp.sqrt(jnp.max(mx, axis=0, keepdims=True)), kmax_ref.shape)

    start = pl.multiple_of(jnp.maximum(t0 - WINDOW, 0), tq)
    ka = kaug_ref[0, 0, pl.ds(start, span), :]
    qs = q_ref[0] * SCALE
    for h in range(HPG):
        qaug_ref[h * tq:(h + 1) * tq, 0:HEAD_DIM] = qs[:, h * HEAD_DIM:(h + 1) * HEAD_DIM]
    bound = _tile_score_bound(qs, kmax_ref[0:1, 0:1])
    safe = jnp.max(bound) <= SAFE_SCORE_BOUND

    def scores():
        s = lax.dot_general(qaug_ref[...], ka, (((1,), (1,)), ((), ())), preferred_element_type=F32)
        d = (lax.broadcasted_iota(jnp.int32, (m_rows, tq), 1)
             - (lax.broadcasted_iota(jnp.int32, (m_rows, tq), 0) & (tq - 1)))
        blocks = []
        for blk in range(span // tq):
            off = t0 - start - blk * tq
            ok = lax.bitcast_convert_type(off - d, jnp.uint32) < jnp.uint32(WINDOW)
            blocks.append(jnp.where(ok, s[:, blk * tq:(blk + 1) * tq], NEG_INF))
        return jnp.concatenate(blocks, axis=-1)

    @pl.when(safe)
    def _():
        qaug_ref[:, tcol] = jnp.broadcast_to(_shift_cols(bound), (m_rows, HEAD_DIM))

    @pl.when(jnp.logical_not(safe))
    def _():
        qaug_ref[:, tcol] = jnp.zeros((m_rows, HEAD_DIM), BF16)
        qaug_ref[:, tcol] = _shift_cols(jnp.max(scores(), axis=-1, keepdims=True))

    p = jnp.exp(scores()).astype(BF16)
    acc = jnp.dot(p, vaug_ref[0, 0, pl.ds(start, span), :], preferred_element_type=F32)
    o = acc[:, 0:HEAD_DIM] / acc[:, HEAD_DIM:HEAD_DIM + 1]
    gates = gates_ref[0]
    o_ref[0] = jnp.concatenate(
        [o[h * tq:(h + 1) * tq] * gates[:, 2 * HPG + h:2 * HPG + h + 1] for h in range(HPG)], axis=-1)


def _win_call(q_rot, k, v, gates, tq):
    b, s, _ = q_rot.shape
    gw = HPG * HEAD_DIM
    qspec = pl.BlockSpec((1, tq, gw), lambda bi, g, i: (bi, i, g))
    kvspec = pl.BlockSpec((1, 1, s, 2 * HEAD_DIM), lambda bi, g, i: (bi, g, 0, 0))
    return pl.pallas_call(
        _win_kernel,
        grid=(b, N_GROUPS, s // tq),
        in_specs=[qspec, kvspec, kvspec, pl.BlockSpec((1, tq, LANES), lambda bi, g, i: (bi, i, g))],
        out_specs=qspec,
        out_shape=jax.ShapeDtypeStruct((b, s, D_ATTN), F32),
        scratch_shapes=[pltpu.VMEM((HPG * tq, 2 * HEAD_DIM), BF16),
                        pltpu.VMEM((8, LANES), F32)],
        compiler_params=pltpu.CompilerParams(dimension_semantics=("arbitrary",) * 3,
                                             vmem_limit_bytes=VMEM_LIMIT),
        name="win_attn",
    )(q_rot, k, v, gates)


def _out_kernel(x_ref, oc_ref, os_ref, ow_ref, convn_ref, ga_ref, wout_ref, gpost_ref,
                gpre2_ref, wg_ref, wu_ref, wd_ref, gpost2_ref, o_ref):
    attn = oc_ref[...] + os_ref[...] + ow_ref[...]
    an = _rms(attn, ga_ref[...]).astype(BF16)
    h = (jnp.dot(an, wout_ref[0:D_ATTN, :], preferred_element_type=F32)
         + jnp.dot(convn_ref[...], wout_ref[D_ATTN:D_MODEL, :], preferred_element_type=F32))
    x1 = x_ref[...] + _rms(h, gpost_ref[...])
    o_ref[...] = _ffn(x1, gpre2_ref[...], wg_ref, wu_ref, wd_ref, gpost2_ref[...])


def _out_call(x2d, oc, osl, ow, convn, ga, wout, gpost, gpre2, wg, wu, wd, gpost2, tm):
    t = x2d.shape[0]
    row = lambda width: pl.BlockSpec((tm, width), lambda i: (i, 0))
    return pl.pallas_call(
        _out_kernel,
        grid=(t // tm,),
        in_specs=[row(D_MODEL), row(D_ATTN), row(D_ATTN), row(D_ATTN), row(D_CONV),
                  _const_spec((1, D_ATTN)), _const_spec(wout.shape), _const_spec((1, D_MODEL)),
                  _const_spec((1, D_MODEL)), _const_spec(wg.shape), _const_spec(wu.shape),
                  _const_spec(wd.shape), _const_spec((1, D_MODEL))],
        out_specs=row(D_MODEL),
        out_shape=jax.ShapeDtypeStruct(x2d.shape, F32),
        compiler_params=pltpu.CompilerParams(dimension_semantics=("arbitrary",),
                                             vmem_limit_bytes=VMEM_LIMIT),
        name="outproj_ffn2",
    )(x2d, oc, osl, ow, convn, ga, wout, gpost, gpre2, wg, wu, wd, gpost2)


def _prep_w_in(w_in):
    sizes = [D_ATTN] + [D_KV] * 6 + [N_GATES] + [D_CONV] * 3
    cuts = np.cumsum([0] + sizes)
    q, kc, vc, ks, vs, kw, vw, gt, bg, cg, xc = [w_in[:, cuts[i]:cuts[i + 1]] for i in range(len(sizes))]
    gt = gt.reshape(D_MODEL, N_GROUPS, HPG, 3).transpose(0, 1, 3, 2).reshape(D_MODEL, N_GROUPS, 3 * HPG)
    gt = jnp.pad(gt, ((0, 0), (0, 0), (0, LANES - 3 * HPG))).reshape(D_MODEL, GATE_COLS)
    return jnp.concatenate([q, kc, vc, ks, vs, kw, vw, gt, bg, cg, xc], axis=1).astype(BF16)


def _prep_compress(pe, w1, w2):
    half = CMP_BLOCK // 2
    eye = jnp.eye(N_GROUPS, dtype=F32)
    w1r = w1.reshape(2, half, HEAD_DIM, CMP_HIDDEN)
    w1big = jnp.einsum("ptdc,gh->ptgdhc", w1r, eye).reshape(2, half * D_KV, N_GROUPS * CMP_HIDDEN)
    w2big = jnp.einsum("cd,gh->gchd", w2, eye).reshape(N_GROUPS * CMP_HIDDEN, D_KV)
    pe_rows = jnp.broadcast_to(pe.reshape(2, half, 1, HEAD_DIM), (2, half, N_GROUPS, HEAD_DIM))
    return pe_rows.reshape(2, half * D_KV), w1big.astype(BF16), w2big.astype(BF16)


def _agg_matrix(nch, nsel, cw):
    agg_w = np.convolve(np.ones(SEL_RATIO), np.ones(CMP_BLOCK // CMP_STRIDE))
    a = np.zeros((nch, nsel), np.float32)
    for j in range(nsel):
        for o, wgt in enumerate(agg_w):
            c = SEL_RATIO * j + o - (CMP_BLOCK // CMP_STRIDE - 1)
            if 0 <= c < nch - 1:
                a[c, j] = wgt
    a = a.T.reshape(nsel, nch // cw, cw).transpose(1, 0, 2)
    return jnp.asarray(a, BF16)


def _rope_inv_freq_row():
    inv = ROPE_THETA ** (-np.arange(ROPE_HALF, dtype=np.float32) * 2.0 / ROPE_DIM)
    lane = np.arange(LANES) % HEAD_DIM
    row = np.where(lane < ROPE_DIM, inv[lane % ROPE_HALF], 0.0).astype(np.float32)
    return jnp.asarray(row.reshape(1, LANES))


def _forward(x, positions, p, *, tm, tq, tc, nbh, tqs, tks, cw):
    b, s, _ = x.shape
    depth = p["w_in"].shape[0]
    nch = s // CMP_STRIDE
    nsel = s // SEL_BLOCK
    cos, sin = _rope_call(positions.reshape(b, s, 1), _rope_inv_freq_row(), tm)
    agg = _agg_matrix(nch, nsel, cw)
    row = lambda v: v.reshape(1, -1)
    for l in range(depth):
        x2d = x.reshape(b * s, D_MODEL)
        x2d = _ffn_call(x2d, row(p["ffn1_norm_pre"][l]), p["ffn1_w_gate"][l].astype(BF16),
                        p["ffn1_w_up"][l].astype(BF16), p["ffn1_w_down"][l].astype(BF16),
                        row(p["ffn1_norm_post"][l]), tm)
        (q_raw, q_rot, kc_in, vc_in, ksaug, vs, kw, vw, gates, convn) = _inproj_call(
            x2d.reshape(b, s, D_MODEL), cos, sin, row(p["mix_norm_pre"][l]), _prep_w_in(p["w_in"][l]),
            p["conv_w"][l], row(p["conv_out_norm"][l]), tm, nbh)
        pek, w1k, w2k = _prep_compress(p["cmp_pe_k"][l], p["cmp_w1_k"][l], p["cmp_w2_k"][l])
        pev, w1v, w2v = _prep_compress(p["cmp_pe_v"][l], p["cmp_w1_v"][l], p["cmp_w2_v"][l])
        kcmp, vcmp = _compress_call(kc_in.reshape(b, nch, CMP_STRIDE * D_KV),
                                    vc_in.reshape(b, nch, CMP_STRIDE * D_KV),
                                    pek, w1k, w2k, pev, w1v, w2v, tc)
        o_cmp, bias = _cmp_call(q_raw, kcmp, vcmp, gates, agg, tq, cw)
        o_slc = _slc_call(q_rot, bias, ksaug, vs, gates, tqs, tks, nbh)
        o_win = _win_call(q_rot, kw, vw, gates, tq)
        flat = lambda a: a.reshape(b * s, a.shape[-1])
        x2d = _out_call(x2d, flat(o_cmp), flat(o_slc), flat(o_win), flat(convn),
                        row(p["attn_out_norm"][l]), p["w_out"][l].astype(BF16), row(p["mix_norm_post"][l]),
                        row(p["ffn2_norm_pre"][l]), p["ffn2_w_gate"][l].astype(BF16),
                        p["ffn2_w_up"][l].astype(BF16), p["ffn2_w_down"][l].astype(BF16),
                        row(p["ffn2_norm_post"][l]), tm)
        x = x2d.reshape(b, s, D_MODEL)
    return x


def kernel(x, positions, ffn1_norm_pre, ffn1_w_gate, ffn1_w_up, ffn1_w_down, ffn1_norm_post, mix_norm_pre, w_in, cmp_pe_k, cmp_w1_k, cmp_w2_k, cmp_pe_v, cmp_w1_v, cmp_w2_v, conv_w, attn_out_norm, conv_out_norm, w_out, mix_norm_post, ffn2_norm_pre, ffn2_w_gate, ffn2_w_up, ffn2_w_down, ffn2_norm_post):
    params = dict(
        ffn1_norm_pre=ffn1_norm_pre, ffn1_w_gate=ffn1_w_gate, ffn1_w_up=ffn1_w_up, ffn1_w_down=ffn1_w_down,
        ffn1_norm_post=ffn1_norm_post, mix_norm_pre=mix_norm_pre, w_in=w_in,
        cmp_pe_k=cmp_pe_k, cmp_w1_k=cmp_w1_k, cmp_w2_k=cmp_w2_k,
        cmp_pe_v=cmp_pe_v, cmp_w1_v=cmp_w1_v, cmp_w2_v=cmp_w2_v,
        conv_w=conv_w, attn_out_norm=attn_out_norm, conv_out_norm=conv_out_norm, w_out=w_out,
        mix_norm_post=mix_norm_post, ffn2_norm_pre=ffn2_norm_pre, ffn2_w_gate=ffn2_w_gate,
        ffn2_w_up=ffn2_w_up, ffn2_w_down=ffn2_w_down, ffn2_norm_post=ffn2_norm_post)
    return _forward(x, positions, params, tm=512, tq=256, tc=256, nbh=128, tqs=512, tks=1024, cw=256)
```

```python
import functools
import math

import numpy as np
import jax
import jax.numpy as jnp
from jax import lax
from jax.experimental import pallas as pl
from jax.experimental.pallas import tpu as pltpu

D_MODEL = 1024
N_HEADS = 8
HEAD_DIM = 64
N_GROUPS = 2
HPG = N_HEADS // N_GROUPS
D_ATTN = N_HEADS * HEAD_DIM
D_KV = N_GROUPS * HEAD_DIM
D_CONV = D_MODEL - D_ATTN
CONV_WIDTH = 3
CMP_BLOCK = 32
CMP_STRIDE = 16
CMP_SHIFT = 4
CMP_HIDDEN = 256
SEL_BLOCK = 64
SEL_SHIFT = 6
SEL_RATIO = SEL_BLOCK // CMP_STRIDE
N_SELECT = 16
N_FORCED = 3
WINDOW = 512
ROPE_THETA = 500000.0
ROPE_DIM = HEAD_DIM // 4
ROPE_HALF = ROPE_DIM // 2
D_FF = 2816
N_GATES = 3 * N_HEADS
EPS = 1e-6
NEG_INF = -1e30
FORCE_SCORE = 1e9
SCALE = 1.0 / math.sqrt(HEAD_DIM)

LANES = 128
SUBLANES = 8
GATE_COLS = N_GROUPS * LANES
VMEM_LIMIT = 56 * 1024 * 1024
FF_CHUNKS = ((0, 768), (768, 1536), (1536, 2304), (2304, 2816))
N_SHIFT_COLS = 2
SAFE_SCORE_BOUND = 40.0

F32 = jnp.float32
BF16 = jnp.bfloat16


def _const_spec(shape):
    nd = len(shape)
    return pl.BlockSpec(shape, lambda *_: (0,) * nd, pipeline_mode=pl.Buffered(1))


def _rms(x, g):
    ms = jnp.mean(x * x, axis=-1, keepdims=True)
    return x * lax.rsqrt(ms + EPS) * g


def _ffn(x, g_pre, wg_ref, wu_ref, wd_ref, g_post):
    h = _rms(x, g_pre).astype(BF16)
    d = None
    for c0, c1 in FF_CHUNKS:
        gate = jnp.dot(h, wg_ref[:, c0:c1], preferred_element_type=F32)
        up = jnp.dot(h, wu_ref[:, c0:c1], preferred_element_type=F32)
        a = (gate * jax.nn.sigmoid(gate) * up).astype(BF16)
        part = jnp.dot(a, wd_ref[c0:c1, :], preferred_element_type=F32)
        d = part if d is None else d + part
    return x + 0.5 * _rms(d, g_post)


def _ffn_kernel(x_ref, gpre_ref, wg_ref, wu_ref, wd_ref, gpost_ref, o_ref):
    o_ref[...] = _ffn(x_ref[...], gpre_ref[...], wg_ref, wu_ref, wd_ref, gpost_ref[...])


def _ffn_call(x2d, g_pre, wg, wu, wd, g_post, tm):
    t = x2d.shape[0]
    row = pl.BlockSpec((tm, D_MODEL), lambda i: (i, 0))
    return pl.pallas_call(
        _ffn_kernel,
        grid=(t // tm,),
        in_specs=[row, _const_spec((1, D_MODEL)), _const_spec(wg.shape), _const_spec(wu.shape),
                  _const_spec(wd.shape), _const_spec((1, D_MODEL))],
        out_specs=row,
        out_shape=jax.ShapeDtypeStruct(x2d.shape, F32),
        compiler_params=pltpu.CompilerParams(dimension_semantics=("arbitrary",),
                                             vmem_limit_bytes=VMEM_LIMIT),
        name="ffn1",
    )(x2d, g_pre, wg, wu, wd, g_post)


_C_Q = 0
_C_KC = _C_Q + D_ATTN
_C_VC = _C_KC + D_KV
_C_KS = _C_VC + D_KV
_C_VS = _C_KS + D_KV
_C_KW = _C_VS + D_KV
_C_VW = _C_KW + D_KV
_C_GATE = _C_VW + D_KV
_C_BG = _C_GATE + GATE_COLS
_C_CG = _C_BG + D_CONV
_C_XC = _C_CG + D_CONV
_C_END = _C_XC + D_CONV


def _rope_kernel(pos_ref, invf_ref, cos_ref, sin_ref):
    ang = pos_ref[0].astype(F32) * invf_ref[...]
    cos_ref[0] = jnp.cos(ang)
    sin_ref[0] = jnp.sin(ang)


def _rope_call(pos3, invf, tm):
    b, s, _ = pos3.shape
    row = lambda width: pl.BlockSpec((1, tm, width), lambda bi, i: (bi, i, 0))
    shape = jax.ShapeDtypeStruct((b, s, LANES), F32)
    return pl.pallas_call(
        _rope_kernel,
        grid=(b, s // tm),
        in_specs=[row(1), _const_spec((1, LANES))],
        out_specs=[row(LANES), row(LANES)],
        out_shape=[shape, shape],
        compiler_params=pltpu.CompilerParams(dimension_semantics=("arbitrary", "arbitrary"),
                                             vmem_limit_bytes=VMEM_LIMIT),
        name="rope_tables",
    )(pos3, invf)


def _inproj_kernel(nbh, x_ref, cos_ref, sin_ref, g_ref, w_ref, convw_ref, convg_ref,
                   qraw_ref, qrot_ref, kc_ref, vc_ref, ksaug_ref, vs_ref, kw_ref, vw_ref,
                   gates_ref, convn_ref, ubuf_ref):
    i = pl.program_id(1)
    tm = x_ref.shape[1]
    h = _rms(x_ref[0], g_ref[...]).astype(BF16)

    def proj(c0, c1):
        return jnp.dot(h, w_ref[:, c0:c1], preferred_element_type=F32)

    cos = cos_ref[0]
    sin = sin_ref[0]
    lane = lax.broadcasted_iota(jnp.int32, (1, LANES), 1) & (HEAD_DIM - 1)
    s_lo = jnp.where(lane < ROPE_HALF, -sin, 0.0)
    s_hi = jnp.where(lane >= ROPE_HALF, sin, 0.0)

    def rope(z):
        return (z * cos + pltpu.roll(z, LANES - ROPE_HALF, 1) * s_lo
                + pltpu.roll(z, ROPE_HALF, 1) * s_hi)

    for c in range(0, D_ATTN // LANES, 2):
        zq2 = proj(_C_Q + c * LANES, _C_Q + (c + 2) * LANES)
        for cc in (c, c + 1):
            zq = zq2[:, (cc - c) * LANES:(cc - c + 1) * LANES]
            qraw_ref[0, :, cc * LANES:(cc + 1) * LANES] = zq.astype(BF16)
            qrot_ref[0, :, cc * LANES:(cc + 1) * LANES] = rope(zq).astype(BF16)

    kvc = proj(_C_KC, _C_KS)
    kc_ref[0] = kvc[:, 0:D_KV]
    vc_ref[0] = kvc[:, D_KV:2 * D_KV]
    kvs = proj(_C_KS, _C_KW)
    ks = rope(kvs[:, 0:D_KV]).astype(BF16)
    vs = kvs[:, D_KV:2 * D_KV].astype(BF16)
    kvw = proj(_C_KW, _C_GATE)
    kw = rope(kvw[:, 0:D_KV]).astype(BF16)
    vw = kvw[:, D_KV:2 * D_KV].astype(BF16)
    row_blk = ((i * tm + lax.broadcasted_iota(jnp.int32, (tm, nbh), 0)) >> SEL_SHIFT) & (nbh - 1)
    onehot = jnp.where(row_blk == lax.broadcasted_iota(jnp.int32, (tm, nbh), 1), 1.0, 0.0).astype(BF16)
    tail = jnp.where(lax.broadcasted_iota(jnp.int32, (tm, HEAD_DIM), 1) < N_SHIFT_COLS, 1.0, 0.0).astype(BF16)
    for g in range(N_GROUPS):
        sl = slice(g * HEAD_DIM, (g + 1) * HEAD_DIM)
        ksaug_ref[0, g, :, 0:nbh] = onehot
        ksaug_ref[0, g, :, nbh:nbh + HEAD_DIM] = ks[:, sl]
        ksaug_ref[0, g, :, nbh + HEAD_DIM:nbh + 2 * HEAD_DIM] = tail
        vs_ref[0, g, :, 0:HEAD_DIM] = vs[:, sl]
        vs_ref[0, g, :, HEAD_DIM:2 * HEAD_DIM] = tail
        kw_ref[0, g, :, 0:HEAD_DIM] = kw[:, sl]
        kw_ref[0, g, :, HEAD_DIM:2 * HEAD_DIM] = tail
        vw_ref[0, g, :, 0:HEAD_DIM] = vw[:, sl]
        vw_ref[0, g, :, HEAD_DIM:2 * HEAD_DIM] = tail

    gates_ref[0] = jax.nn.sigmoid(proj(_C_GATE, _C_BG))

    u = proj(_C_CG, _C_XC) * proj(_C_XC, _C_END)

    @pl.when(i == 0)
    def _():
        ubuf_ref[0:8, :] = jnp.zeros((8, D_CONV), F32)

    @pl.when(i > 0)
    def _():
        ubuf_ref[0:8, :] = ubuf_ref[tm:tm + 8, :]

    ubuf_ref[8:tm + 8, :] = u
    w = convw_ref[...]
    y = (w[2:3, :] * u + w[1:2, :] * ubuf_ref[7:tm + 7, :] + w[0:1, :] * ubuf_ref[6:tm + 6, :])
    conv = proj(_C_BG, _C_CG) * y
    convn_ref[0] = _rms(conv, convg_ref[...]).astype(BF16)


def _inproj_call(x, cos, sin, g_pre, w_in, conv_w, conv_g, tm, nbh):
    b, s, _ = x.shape
    grid = (b, s // tm)
    row = lambda width: pl.BlockSpec((1, tm, width), lambda bi, i: (bi, i, 0))
    grp = lambda width: pl.BlockSpec((1, N_GROUPS, tm, width), lambda bi, i: (bi, 0, i, 0))
    kern = functools.partial(_inproj_kernel, nbh)
    return pl.pallas_call(
        kern,
        grid=grid,
        in_specs=[row(D_MODEL), row(LANES), row(LANES), _const_spec((1, D_MODEL)), _const_spec(w_in.shape),
                  _const_spec((CONV_WIDTH, D_CONV)), _const_spec((1, D_CONV))],
        out_specs=[row(D_ATTN), row(D_ATTN), row(D_KV), row(D_KV), grp(nbh + 2 * HEAD_DIM), grp(2 * HEAD_DIM),
                   grp(2 * HEAD_DIM), grp(2 * HEAD_DIM), row(GATE_COLS), row(D_CONV)],
        out_shape=[
            jax.ShapeDtypeStruct((b, s, D_ATTN), BF16),
            jax.ShapeDtypeStruct((b, s, D_ATTN), BF16),
            jax.ShapeDtypeStruct((b, s, D_KV), F32),
            jax.ShapeDtypeStruct((b, s, D_KV), F32),
            jax.ShapeDtypeStruct((b, N_GROUPS, s, nbh + 2 * HEAD_DIM), BF16),
            jax.ShapeDtypeStruct((b, N_GROUPS, s, 2 * HEAD_DIM), BF16),
            jax.ShapeDtypeStruct((b, N_GROUPS, s, 2 * HEAD_DIM), BF16),
            jax.ShapeDtypeStruct((b, N_GROUPS, s, 2 * HEAD_DIM), BF16),
            jax.ShapeDtypeStruct((b, s, GATE_COLS), F32),
            jax.ShapeDtypeStruct((b, s, D_CONV), BF16),
        ],
        scratch_shapes=[pltpu.VMEM((tm + 8, D_CONV), F32)],
        compiler_params=pltpu.CompilerParams(dimension_semantics=("arbitrary", "arbitrary"),
                                             vmem_limit_bytes=VMEM_LIMIT),
        name="inproj",
    )(x, cos, sin, g_pre, w_in, conv_w, conv_g)


def _compress_kernel(kc_ref, kcn_ref, vc_ref, vcn_ref, pek_ref, w1k_ref, w2k_ref,
                     pev_ref, w1v_ref, w2v_ref, ko_ref, vo_ref):
    tc = kc_ref.shape[1]
    hid_w = N_GROUPS * CMP_HIDDEN
    last = lax.broadcasted_iota(jnp.int32, (tc, 1), 0) == tc - 1

    def one(x_ref, xn_ref, pe_ref, w1_ref, w2_ref, o_ref):
        x = x_ref[0]
        top = jnp.dot((x + pe_ref[0:1, :]).astype(BF16), w1_ref[0], preferred_element_type=F32)
        xb = (x + pe_ref[1:2, :]).astype(BF16)
        bot = jnp.dot(xb, w1_ref[1], preferred_element_type=F32)
        xnb = (xn_ref[0] + pe_ref[1:2, :]).astype(BF16)
        botn = jnp.dot(xnb, w1_ref[1], preferred_element_type=F32)
        shifted = jnp.where(last, botn[0:1, :], pltpu.roll(bot, tc - 1, 0))
        hid = jax.nn.gelu(top + shifted).astype(BF16)
        out = jnp.dot(hid, w2_ref[...], preferred_element_type=F32)
        tail = jnp.where(lax.broadcasted_iota(jnp.int32, (tc, HEAD_DIM), 1) < N_SHIFT_COLS, 1.0, 0.0).astype(BF16)
        for g in range(N_GROUPS):
            o_ref[0, g, :, 0:HEAD_DIM] = out[:, g * HEAD_DIM:(g + 1) * HEAD_DIM].astype(BF16)
            o_ref[0, g, :, HEAD_DIM:2 * HEAD_DIM] = tail

    one(kc_ref, kcn_ref, pek_ref, w1k_ref, w2k_ref, ko_ref)
    one(vc_ref, vcn_ref, pev_ref, w1v_ref, w2v_ref, vo_ref)


def _compress_call(kc_in, vc_in, pek, w1k, w2k, pev, w1v, w2v, tc):
    b, nch, width = kc_in.shape
    nt = nch // tc
    last8 = nch // 8 - 1
    cur = pl.BlockSpec((1, tc, width), lambda bi, i: (bi, i, 0))
    nxt = pl.BlockSpec((1, 8, width), lambda bi, i: (bi, jnp.minimum((i + 1) * (tc // 8), last8), 0))
    out = pl.BlockSpec((1, N_GROUPS, tc, 2 * HEAD_DIM), lambda bi, i: (bi, 0, i, 0))
    oshape = jax.ShapeDtypeStruct((b, N_GROUPS, nch, 2 * HEAD_DIM), BF16)
    return pl.pallas_call(
        _compress_kernel,
        grid=(b, nt),
        in_specs=[cur, nxt, cur, nxt, _const_spec(pek.shape), _const_spec(w1k.shape), _const_spec(w2k.shape),
                  _const_spec(pev.shape), _const_spec(w1v.shape), _const_spec(w2v.shape)],
        out_specs=[out, out],
        out_shape=[oshape, oshape],
        compiler_params=pltpu.CompilerParams(dimension_semantics=("arbitrary", "arbitrary"),
                                             vmem_limit_bytes=VMEM_LIMIT),
        name="compress",
    )(kc_in, kc_in, vc_in, vc_in, pek, w1k, w2k, pev, w1v, w2v)


def _split3(x):
    hi = x.astype(BF16)
    r = x - hi.astype(F32)
    mid = r.astype(BF16)
    lo = (r - mid.astype(F32)).astype(BF16)
    return hi, mid, lo


def _tile_score_bound(qs, kmax):
    qf = qs.astype(F32)
    sq = qf * qf
    nrm2 = None
    for h in range(HPG):
        rows = jnp.sum(sq[:, h * HEAD_DIM:(h + 1) * HEAD_DIM], axis=-1, keepdims=True)
        top = jnp.max(rows, axis=0, keepdims=True)
        nrm2 = top if nrm2 is None else jnp.maximum(nrm2, top)
    return jnp.sqrt(nrm2) * kmax


def _shift_cols(shift):
    hi = shift.astype(BF16).astype(F32)
    lo = (shift - hi).astype(BF16).astype(F32)
    lane = lax.broadcasted_iota(jnp.int32, (1, HEAD_DIM), 1)
    return jnp.where(lane == 0, -hi, jnp.where(lane == 1, -lo, 0.0)).astype(BF16)


def _cmp_kernel(k_top, cw, q_ref, kaug_ref, vaug_ref, gates_ref, aggt_ref, o_ref, bias_ref,
                qaug_ref, e_ref, acc_ref, kmax_ref, pslc_ref):
    qt = pl.program_id(2)
    tq = q_ref.shape[1]
    nch = kaug_ref.shape[2]
    nsel = bias_ref.shape[3]
    m_rows = HPG * tq
    t0 = qt * tq
    tcol = slice(HEAD_DIM, 2 * HEAD_DIM)
    c_last = ((t0 + tq - CMP_BLOCK) >> CMP_SHIFT) // cw
    c_mask = jnp.maximum(c_last - 1, 0)

    @pl.when(qt == 0)
    def _():
        k = kaug_ref[0, 0, :, 0:HEAD_DIM].astype(F32)
        n_ok = lax.broadcasted_iota(jnp.int32, (nch, 1), 0) < nch - 1
        ksq = jnp.where(n_ok, jnp.sum(k * k, axis=-1, keepdims=True), 0.0)
        kmax_ref[...] = jnp.broadcast_to(jnp.sqrt(jnp.max(ksq, axis=0, keepdims=True)), kmax_ref.shape)

    qs = q_ref[0] * SCALE
    for h in range(HPG):
        qaug_ref[h * tq:(h + 1) * tq, 0:HEAD_DIM] = qs[:, h * HEAD_DIM:(h + 1) * HEAD_DIM]
    bound = _tile_score_bound(qs, kmax_ref[0:1, 0:1])
    safe = jnp.max(bound) <= SAFE_SCORE_BOUND

    def scores(c, masked):
        ka = kaug_ref[0, 0, pl.ds(pl.multiple_of(c * cw, cw), cw), :]
        s = lax.dot_general(qaug_ref[...], ka, (((1,), (1,)), ((), ())), preferred_element_type=F32)
        if masked:
            row_t = t0 + (lax.broadcasted_iota(jnp.int32, (m_rows, 1), 0) & (tq - 1))
            n_vis = (row_t - (CMP_BLOCK - 1)) >> CMP_SHIFT
            n = c * cw + lax.broadcasted_iota(jnp.int32, (m_rows, cw), 1)
            s = jnp.where(n <= n_vis, s, NEG_INF)
        return s

    def sweep(fn_full, fn_masked, init):
        carry = lax.fori_loop(0, c_mask, fn_full, init)
        return lax.fori_loop(c_mask, c_last + 1, fn_masked, carry)

    @pl.when(safe)
    def _():
        qaug_ref[:, tcol] = jnp.broadcast_to(_shift_cols(bound), (m_rows, HEAD_DIM))

    @pl.when(jnp.logical_not(safe))
    def _():
        qaug_ref[:, tcol] = jnp.zeros((m_rows, HEAD_DIM), BF16)
        step = lambda masked: (lambda c, m: jnp.maximum(m, jnp.max(scores(c, masked), axis=-1, keepdims=True)))
        mx = sweep(step(False), step(True), jnp.full((m_rows, 1), NEG_INF, F32))
        qaug_ref[:, tcol] = _shift_cols(jnp.where(mx > 0.5 * NEG_INF, mx, 0.0))

    acc_ref[...] = jnp.zeros(acc_ref.shape, F32)

    def chunk(masked):
        def body(c, carry):
            e = jnp.exp(scores(c, masked))
            start = pl.multiple_of(c * cw, cw)
            e_ref[c] = e
            acc_ref[...] += jnp.dot(e.astype(BF16), vaug_ref[0, 0, pl.ds(start, cw), :],
                                    preferred_element_type=F32)
            return carry
        return body

    sweep(chunk(False), chunk(True), 0)
    acc = acc_ref[...]
    l = acc[:, HEAD_DIM:HEAD_DIM + 1]
    rinv = jnp.where(l > 0.0, 1.0 / l, 0.0)
    o = acc[:, 0:HEAD_DIM] * rinv
    gates = gates_ref[0]
    o_ref[0] = jnp.concatenate(
        [o[h * tq:(h + 1) * tq] * gates[:, h:h + 1] for h in range(HPG)], axis=-1)

    pslc_ref[...] = jnp.zeros(pslc_ref.shape, F32)

    def agg_body(c, carry):
        imp = sum(e_ref[c, h * tq:(h + 1) * tq, :] * rinv[h * tq:(h + 1) * tq] for h in range(HPG))
        at = aggt_ref[c]
        pslc_ref[...] += sum(lax.dot_general(at, part, (((1,), (1,)), ((), ())), preferred_element_type=F32)
                             for part in _split3(imp))
        return carry

    lax.fori_loop(0, c_last + 1, agg_body, 0)

    n_slabs = nsel // SUBLANES
    sub = lax.broadcasted_iota(jnp.int32, (SUBLANES, LANES), 0)
    sub_f = sub.astype(F32)
    no_slab = float(n_slabs)
    def tree(fn, xs):
        while len(xs) > 1:
            xs = [fn(xs[i], xs[i + 1]) if i + 1 < len(xs) else xs[i] for i in range(0, len(xs), 2)]
        return xs[0]

    def first_slab(score, m):
        firsts = []
        for g0 in range(0, n_slabs, SUBLANES):
            slab = jnp.full((SUBLANES, LANES), no_slab, F32)
            for r in reversed(range(g0, min(g0 + SUBLANES, n_slabs))):
                slab = jnp.where(score[r] == m, float(r), slab)
            firsts.append(slab)
        return tree(jnp.minimum, firsts)

    for col in range(tq // LANES):
        lanes = slice(col * LANES, (col + 1) * LANES)
        cur = (t0 + col * LANES + lax.broadcasted_iota(jnp.int32, (1, LANES), 1)) >> SEL_SHIFT
        score = []
        for r in range(n_slabs):
            j = sub + r * SUBLANES
            free = (j <= cur - 2) & (j > 0)
            score.append(jnp.where(free, pslc_ref[r * SUBLANES:(r + 1) * SUBLANES, lanes], -jnp.inf))
        for _ in range(max(k_top - N_FORCED, 0)):
            m = jnp.max(tree(jnp.maximum, score), axis=0, keepdims=True)
            slab = first_slab(score, m)
            block = slab * SUBLANES + sub_f
            first = jnp.min(block, axis=0, keepdims=True)
            taken = jnp.where(block == first, slab, -1.0)
            score = [jnp.where(taken == float(r), -jnp.inf, score[r]) for r in range(n_slabs)]
        for r in range(n_slabs):
            pslc_ref[r * SUBLANES:(r + 1) * SUBLANES, lanes] = jnp.where(
                (sub + r * SUBLANES <= cur) & (score[r] == -jnp.inf), 0.0, NEG_INF)
    bias_ref[0, 0] = pslc_ref[...].T.astype(BF16)


def _cmp_call(q_raw, kaug, vaug, gates, aggt, tq, cw):
    b, s, _ = q_raw.shape
    nch = kaug.shape[2]
    nsel = aggt.shape[1]
    assert nch % cw == 0 and tq // CMP_STRIDE < cw
    k_top = min(N_SELECT, nsel)
    gw = HPG * HEAD_DIM
    m_rows = HPG * tq
    qspec = pl.BlockSpec((1, tq, gw), lambda bi, g, i: (bi, i, g))
    kvspec = pl.BlockSpec((1, 1, nch, 2 * HEAD_DIM), lambda bi, g, i: (bi, g, 0, 0))
    gspec = pl.BlockSpec((1, tq, LANES), lambda bi, g, i: (bi, i, g))
    return pl.pallas_call(
        functools.partial(_cmp_kernel, k_top, cw),
        grid=(b, N_GROUPS, s // tq),
        in_specs=[qspec, kvspec, kvspec, gspec, _const_spec(aggt.shape)],
        out_specs=[qspec, pl.BlockSpec((1, 1, tq, nsel), lambda bi, g, i: (bi, g, i, 0))],
        out_shape=[jax.ShapeDtypeStruct((b, s, D_ATTN), F32),
                   jax.ShapeDtypeStruct((b, N_GROUPS, s, nsel), BF16)],
        scratch_shapes=[pltpu.VMEM((m_rows, 2 * HEAD_DIM), BF16),
                        pltpu.VMEM((nch // cw, m_rows, cw), F32),
                        pltpu.VMEM((m_rows, 2 * HEAD_DIM), F32),
                        pltpu.VMEM((8, LANES), F32),
                        pltpu.VMEM((nsel, tq), F32)],
        compiler_params=pltpu.CompilerParams(dimension_semantics=("arbitrary",) * 3,
                                             vmem_limit_bytes=VMEM_LIMIT),
        name="cmp_attn_topk",
    )(q_raw, kaug, vaug, gates, aggt)


def _slc_kernel(nbh, tk, q_ref, bias_ref, kaug_ref, vaug_ref, gates_ref, o_ref,
                qaug_ref, acc_ref, kmax_ref):
    qt = pl.program_id(2)
    tq = q_ref.shape[1]
    m_rows = HPG * tq
    t0 = qt * tq
    n_wide = t0 // tk
    n_narrow = (t0 - n_wide * tk) // tq
    n_half = qaug_ref.shape[0]
    kcol = slice(nbh, nbh + HEAD_DIM)
    tcol = slice(nbh + HEAD_DIM, nbh + 2 * HEAD_DIM)

    @pl.when(qt == 0)
    def _():
        def body(c, mx):
            k = kaug_ref[0, 0, pl.ds(pl.multiple_of(c * tk, tk), tk), kcol].astype(F32)
            return jnp.maximum(mx, jnp.sum(k * k, axis=-1, keepdims=True))
        mx = lax.fori_loop(0, kaug_ref.shape[2] // tk, body, jnp.zeros((tk, 1), F32))
        kmax_ref[...] = jnp.broadcast_to(jnp.sqrt(jnp.max(mx, axis=0, keepdims=True)), kmax_ref.shape)

    qs = q_ref[0] * SCALE
    bias = bias_ref[0, 0]
    for h in range(HPG):
        rows = slice(h * tq, (h + 1) * tq)
        for hf in range(n_half):
            qaug_ref[hf, rows, 0:nbh] = bias[:, hf * nbh:(hf + 1) * nbh]
            qaug_ref[hf, rows, kcol] = qs[:, h * HEAD_DIM:(h + 1) * HEAD_DIM]
    bound = _tile_score_bound(qs, kmax_ref[0:1, 0:1])
    safe = jnp.max(bound) <= SAFE_SCORE_BOUND

    def scores(start, width, diagonal):
        ka = kaug_ref[0, 0, pl.ds(start, width), :]
        qa = qaug_ref[start // (nbh * SEL_BLOCK)]
        s = lax.dot_general(qa, ka, (((1,), (1,)), ((), ())), preferred_element_type=F32)
        if diagonal:
            r = lax.broadcasted_iota(jnp.int32, (m_rows, width), 0) & (tq - 1)
            c = lax.broadcasted_iota(jnp.int32, (m_rows, width), 1)
            s = jnp.where(c <= r, s, NEG_INF)
        return s

    def sweep(fn, carry):
        carry = lax.fori_loop(
            0, n_wide, lambda i, c: fn(pl.multiple_of(i * tk, tk), tk, False, c), carry)
        carry = lax.fori_loop(
            0, n_narrow, lambda i, c: fn(pl.multiple_of(n_wide * tk + i * tq, tq), tq, False, c), carry)
        return fn(pl.multiple_of(t0, tq), tq, True, carry)

    @pl.when(safe)
    def _():
        tail = jnp.broadcast_to(_shift_cols(bound), (m_rows, HEAD_DIM))
        for hf in range(n_half):
            qaug_ref[hf, :, tcol] = tail

    @pl.when(jnp.logical_not(safe))
    def _():
        for hf in range(n_half):
            qaug_ref[hf, :, tcol] = jnp.zeros((m_rows, HEAD_DIM), BF16)
        mx = sweep(lambda st, w, dg, m: jnp.maximum(m, jnp.max(scores(st, w, dg), axis=-1, keepdims=True)),
                   jnp.full((m_rows, 1), NEG_INF, F32))
        tail = _shift_cols(mx)
        for hf in range(n_half):
            qaug_ref[hf, :, tcol] = tail

    acc_ref[...] = jnp.zeros(acc_ref.shape, F32)

    def tile(start, width, diagonal, carry):
        p = jnp.exp(scores(start, width, diagonal)).astype(BF16)
        acc_ref[...] += jnp.dot(p, vaug_ref[0, 0, pl.ds(start, width), :], preferred_element_type=F32)
        return carry

    sweep(tile, 0)
    acc = acc_ref[...]
    o = acc[:, 0:HEAD_DIM] / acc[:, HEAD_DIM:HEAD_DIM + 1]
    gates = gates_ref[0]
    o_ref[0] = jnp.concatenate(
        [o[h * tq:(h + 1) * tq] * gates[:, HPG + h:HPG + h + 1] for h in range(HPG)], axis=-1)


def _slc_call(q_rot, bias, kaug, v, gates, tq, tk, nbh):
    b, s, _ = q_rot.shape
    assert tk % tq == 0 and (nbh * SEL_BLOCK) % tk == 0 and s % tk == 0
    nsel = bias.shape[3]
    gw = HPG * HEAD_DIM
    aug = nbh + 2 * HEAD_DIM
    qspec = pl.BlockSpec((1, tq, gw), lambda bi, g, i: (bi, i, g))
    return pl.pallas_call(
        functools.partial(_slc_kernel, nbh, tk),
        grid=(b, N_GROUPS, s // tq),
        in_specs=[qspec,
                  pl.BlockSpec((1, 1, tq, nsel), lambda bi, g, i: (bi, g, i, 0)),
                  pl.BlockSpec((1, 1, s, aug), lambda bi, g, i: (bi, g, 0, 0)),
                  pl.BlockSpec((1, 1, s, 2 * HEAD_DIM), lambda bi, g, i: (bi, g, 0, 0)),
                  pl.BlockSpec((1, tq, LANES), lambda bi, g, i: (bi, i, g))],
        out_specs=qspec,
        out_shape=jax.ShapeDtypeStruct((b, s, D_ATTN), F32),
        scratch_shapes=[pltpu.VMEM((nsel // nbh, HPG * tq, aug), BF16),
                        pltpu.VMEM((HPG * tq, 2 * HEAD_DIM), F32),
                        pltpu.VMEM((8, LANES), F32)],
        compiler_params=pltpu.CompilerParams(dimension_semantics=("arbitrary",) * 3,
                                             vmem_limit_bytes=VMEM_LIMIT),
        name="slc_attn",
    )(q_rot, bias, kaug, v, gates)


def _win_kernel(tq, q_ref, kaug_ref, vaug_ref, gates_ref, o_ref, qaug_ref, kmax_ref):
    qb = pl.program_id(2)
    tb = q_ref.shape[1]
    n_sub = tb // tq
    span = WINDOW + tq
    m_rows = HPG * tq
    tcol = slice(HEAD_DIM, 2 * HEAD_DIM)

    @pl.when(qb == 0)
    def _():
        def body(c, mx):
            k = kaug_ref[0, 0, pl.ds(pl.multiple_of(c * tq, tq), tq), 0:HEAD_DIM].astype(F32)
            return jnp.maximum(mx, jnp.sum(k * k, axis=-1, keepdims=True))
        mx = lax.fori_loop(0, kaug_ref.shape[2] // tq, body, jnp.zeros((tq, 1), F32))
        kmax_ref[...] = jnp.broadcast_to(jnp.sqrt(jnp.max(mx, axis=0, keepdims=True)), kmax_ref.shape)

    qs = q_ref[0] * SCALE
    for j in range(n_sub):
        for h in range(HPG):
            qaug_ref[j, h * tq:(h + 1) * tq, 0:HEAD_DIM] = qs[j * tq:(j + 1) * tq, h * HEAD_DIM:(h + 1) * HEAD_DIM]
    bound = _tile_score_bound(qs, kmax_ref[0:1, 0:1])
    safe = jnp.max(bound) <= SAFE_SCORE_BOUND

    def band_start(j):
        return pl.multiple_of(jnp.maximum(qb * tb + j * tq - WINDOW, 0), tq)

    def scores(j):
        t0 = qb * tb + j * tq
        start = band_start(j)
        ka = kaug_ref[0, 0, pl.ds(start, span), :]
        s = lax.dot_general(qaug_ref[j], ka, (((1,), (1,)), ((), ())), preferred_element_type=F32)
        d = (lax.broadcasted_iota(jnp.int32, (m_rows, tq), 1)
             - (lax.broadcasted_iota(jnp.int32, (m_rows, tq), 0) & (tq - 1)))
        blocks = []
        for blk in range(span // tq):
            off = t0 - start - blk * tq
            ok = lax.bitcast_convert_type(off - d, jnp.uint32) < jnp.uint32(WINDOW)
            blocks.append(jnp.where(ok, s[:, blk * tq:(blk + 1) * tq], NEG_INF))
        return jnp.concatenate(blocks, axis=-1)

    @pl.when(safe)
    def _():
        for j in range(n_sub):
            qaug_ref[j, :, tcol] = jnp.broadcast_to(_shift_cols(bound), (m_rows, HEAD_DIM))

    @pl.when(jnp.logical_not(safe))
    def _():
        for j in range(n_sub):
            qaug_ref[j, :, tcol] = jnp.zeros((m_rows, HEAD_DIM), BF16)
            qaug_ref[j, :, tcol] = _shift_cols(jnp.max(scores(j), axis=-1, keepdims=True))

    gates = gates_ref[0]
    for j in range(n_sub):
        p = jnp.exp(scores(j)).astype(BF16)
        acc = jnp.dot(p, vaug_ref[0, 0, pl.ds(band_start(j), span), :], preferred_element_type=F32)
        o = acc[:, 0:HEAD_DIM] / acc[:, HEAD_DIM:HEAD_DIM + 1]
        gj = gates[j * tq:(j + 1) * tq]
        o_ref[0, j * tq:(j + 1) * tq, :] = jnp.concatenate(
            [o[h * tq:(h + 1) * tq] * gj[:, 2 * HPG + h:2 * HPG + h + 1] for h in range(HPG)], axis=-1)


def _win_call(q_rot, k, v, gates, tb, tq):
    b, s, _ = q_rot.shape
    assert tb % tq == 0 and WINDOW % tq == 0
    gw = HPG * HEAD_DIM
    qspec = pl.BlockSpec((1, tb, gw), lambda bi, g, i: (bi, i, g))
    kvspec = pl.BlockSpec((1, 1, s, 2 * HEAD_DIM), lambda bi, g, i: (bi, g, 0, 0))
    return pl.pallas_call(
        functools.partial(_win_kernel, tq),
        grid=(b, N_GROUPS, s // tb),
        in_specs=[qspec, kvspec, kvspec, pl.BlockSpec((1, tb, LANES), lambda bi, g, i: (bi, i, g))],
        out_specs=qspec,
        out_shape=jax.ShapeDtypeStruct((b, s, D_ATTN), F32),
        scratch_shapes=[pltpu.VMEM((tb // tq, HPG * tq, 2 * HEAD_DIM), BF16),
                        pltpu.VMEM((8, LANES), F32)],
        compiler_params=pltpu.CompilerParams(dimension_semantics=("arbitrary",) * 3,
                                             vmem_limit_bytes=VMEM_LIMIT),
        name="win_attn",
    )(q_rot, k, v, gates)


def _out_kernel(x_ref, oc_ref, os_ref, ow_ref, convn_ref, ga_ref, wout_ref, gpost_ref,
                gpre2_ref, wg_ref, wu_ref, wd_ref, gpost2_ref, o_ref):
    attn = oc_ref[...] + os_ref[...] + ow_ref[...]
    an = _rms(attn, ga_ref[...]).astype(BF16)
    h = (jnp.dot(an, wout_ref[0:D_ATTN, :], preferred_element_type=F32)
         + jnp.dot(convn_ref[...], wout_ref[D_ATTN:D_MODEL, :], preferred_element_type=F32))
    x1 = x_ref[...] + _rms(h, gpost_ref[...])
    o_ref[...] = _ffn(x1, gpre2_ref[...], wg_ref, wu_ref, wd_ref, gpost2_ref[...])


def _out_call(x2d, oc, osl, ow, convn, ga, wout, gpost, gpre2, wg, wu, wd, gpost2, tm):
    t = x2d.shape[0]
    row = lambda width: pl.BlockSpec((tm, width), lambda i: (i, 0))
    return pl.pallas_call(
        _out_kernel,
        grid=(t // tm,),
        in_specs=[row(D_MODEL), row(D_ATTN), row(D_ATTN), row(D_ATTN), row(D_CONV),
                  _const_spec((1, D_ATTN)), _const_spec(wout.shape), _const_spec((1, D_MODEL)),
                  _const_spec((1, D_MODEL)), _const_spec(wg.shape), _const_spec(wu.shape),
                  _const_spec(wd.shape), _const_spec((1, D_MODEL))],
        out_specs=row(D_MODEL),
        out_shape=jax.ShapeDtypeStruct(x2d.shape, F32),
        compiler_params=pltpu.CompilerParams(dimension_semantics=("arbitrary",),
                                             vmem_limit_bytes=VMEM_LIMIT),
        name="outproj_ffn2",
    )(x2d, oc, osl, ow, convn, ga, wout, gpost, gpre2, wg, wu, wd, gpost2)


def _prep_w_in(w_in):
    sizes = [D_ATTN] + [D_KV] * 6 + [N_GATES] + [D_CONV] * 3
    cuts = np.cumsum([0] + sizes)
    q, kc, vc, ks, vs, kw, vw, gt, bg, cg, xc = [w_in[:, cuts[i]:cuts[i + 1]] for i in range(len(sizes))]
    gt = gt.reshape(D_MODEL, N_GROUPS, HPG, 3).transpose(0, 1, 3, 2).reshape(D_MODEL, N_GROUPS, 3 * HPG)
    gt = jnp.pad(gt, ((0, 0), (0, 0), (0, LANES - 3 * HPG))).reshape(D_MODEL, GATE_COLS)
    return jnp.concatenate([q, kc, vc, ks, vs, kw, vw, gt, bg, cg, xc], axis=1).astype(BF16)


def _prep_compress(pe, w1, w2):
    half = CMP_BLOCK // 2
    eye = jnp.eye(N_GROUPS, dtype=F32)
    w1r = w1.reshape(2, half, HEAD_DIM, CMP_HIDDEN)
    w1big = jnp.einsum("ptdc,gh->ptgdhc", w1r, eye).reshape(2, half * D_KV, N_GROUPS * CMP_HIDDEN)
    w2big = jnp.einsum("cd,gh->gchd", w2, eye).reshape(N_GROUPS * CMP_HIDDEN, D_KV)
    pe_rows = jnp.broadcast_to(pe.reshape(2, half, 1, HEAD_DIM), (2, half, N_GROUPS, HEAD_DIM))
    return pe_rows.reshape(2, half * D_KV), w1big.astype(BF16), w2big.astype(BF16)


def _agg_matrix(nch, nsel, cw):
    agg_w = np.convolve(np.ones(SEL_RATIO), np.ones(CMP_BLOCK // CMP_STRIDE))
    a = np.zeros((nch, nsel), np.float32)
    for j in range(nsel):
        for o, wgt in enumerate(agg_w):
            c = SEL_RATIO * j + o - (CMP_BLOCK // CMP_STRIDE - 1)
            if 0 <= c < nch - 1:
                a[c, j] = wgt
    a = a.T.reshape(nsel, nch // cw, cw).transpose(1, 0, 2)
    return jnp.asarray(a, BF16)


def _rope_inv_freq_row():
    inv = ROPE_THETA ** (-np.arange(ROPE_HALF, dtype=np.float32) * 2.0 / ROPE_DIM)
    lane = np.arange(LANES) % HEAD_DIM
    row = np.where(lane < ROPE_DIM, inv[lane % ROPE_HALF], 0.0).astype(np.float32)
    return jnp.asarray(row.reshape(1, LANES))


def _forward(x, positions, p, *, tm, tq, tc, nbh, tqs, tks, tqc, cw, tbw):
    b, s, _ = x.shape
    depth = p["w_in"].shape[0]
    nch = s // CMP_STRIDE
    nsel = s // SEL_BLOCK
    cos, sin = _rope_call(positions.reshape(b, s, 1), _rope_inv_freq_row(), tm)
    agg = _agg_matrix(nch, nsel, cw)
    row = lambda v: v.reshape(1, -1)
    for l in range(depth):
        x2d = x.reshape(b * s, D_MODEL)
        x2d = _ffn_call(x2d, row(p["ffn1_norm_pre"][l]), p["ffn1_w_gate"][l].astype(BF16),
                        p["ffn1_w_up"][l].astype(BF16), p["ffn1_w_down"][l].astype(BF16),
                        row(p["ffn1_norm_post"][l]), tm)
        (q_raw, q_rot, kc_in, vc_in, ksaug, vs, kw, vw, gates, convn) = _inproj_call(
            x2d.reshape(b, s, D_MODEL), cos, sin, row(p["mix_norm_pre"][l]), _prep_w_in(p["w_in"][l]),
            p["conv_w"][l], row(p["conv_out_norm"][l]), tm, nbh)
        pek, w1k, w2k = _prep_compress(p["cmp_pe_k"][l], p["cmp_w1_k"][l], p["cmp_w2_k"][l])
        pev, w1v, w2v = _prep_compress(p["cmp_pe_v"][l], p["cmp_w1_v"][l], p["cmp_w2_v"][l])
        kcmp, vcmp = _compress_call(kc_in.reshape(b, nch, CMP_STRIDE * D_KV),
                                    vc_in.reshape(b, nch, CMP_STRIDE * D_KV),
                                    pek, w1k, w2k, pev, w1v, w2v, tc)
        o_cmp, bias = _cmp_call(q_raw, kcmp, vcmp, gates, agg, tqc, cw)
        o_slc = _slc_call(q_rot, bias, ksaug, vs, gates, tqs, tks, nbh)
        o_win = _win_call(q_rot, kw, vw, gates, tbw, tq)
        flat = lambda a: a.reshape(b * s, a.shape[-1])
        x2d = _out_call(x2d, flat(o_cmp), flat(o_slc), flat(o_win), flat(convn),
                        row(p["attn_out_norm"][l]), p["w_out"][l].astype(BF16), row(p["mix_norm_post"][l]),
                        row(p["ffn2_norm_pre"][l]), p["ffn2_w_gate"][l].astype(BF16),
                        p["ffn2_w_up"][l].astype(BF16), p["ffn2_w_down"][l].astype(BF16),
                        row(p["ffn2_norm_post"][l]), tm)
        x = x2d.reshape(b, s, D_MODEL)
    return x


def kernel(x, positions, ffn1_norm_pre, ffn1_w_gate, ffn1_w_up, ffn1_w_down, ffn1_norm_post, mix_norm_pre, w_in, cmp_pe_k, cmp_w1_k, cmp_w2_k, cmp_pe_v, cmp_w1_v, cmp_w2_v, conv_w, attn_out_norm, conv_out_norm, w_out, mix_norm_post, ffn2_norm_pre, ffn2_w_gate, ffn2_w_up, ffn2_w_down, ffn2_norm_post):
    params = dict(
        ffn1_norm_pre=ffn1_norm_pre, ffn1_w_gate=ffn1_w_gate, ffn1_w_up=ffn1_w_up, ffn1_w_down=ffn1_w_down,
        ffn1_norm_post=ffn1_norm_post, mix_norm_pre=mix_norm_pre, w_in=w_in,
        cmp_pe_k=cmp_pe_k, cmp_w1_k=cmp_w1_k, cmp_w2_k=cmp_w2_k,
        cmp_pe_v=cmp_pe_v, cmp_w1_v=cmp_w1_v, cmp_w2_v=cmp_w2_v,
        conv_w=conv_w, attn_out_norm=attn_out_norm, conv_out_norm=conv_out_norm, w_out=w_out,
        mix_norm_post=mix_norm_post, ffn2_norm_pre=ffn2_norm_pre, ffn2_w_gate=ffn2_w_gate,
        ffn2_w_up=ffn2_w_up, ffn2_w_down=ffn2_w_down, ffn2_norm_post=ffn2_norm_post)
    return _forward(x, positions, params, tm=512, tq=256, tc=256, nbh=128, tqs=512, tks=1024, tqc=512, cw=256, tbw=512)
```

```python
import functools
import math

import numpy as np
import jax
import jax.numpy as jnp
from jax import lax
from jax.experimental import pallas as pl
from jax.experimental.pallas import tpu as pltpu

D_MODEL = 1024
N_HEADS = 8
HEAD_DIM = 64
N_GROUPS = 2
HPG = N_HEADS // N_GROUPS
D_ATTN = N_HEADS * HEAD_DIM
D_KV = N_GROUPS * HEAD_DIM
D_CONV = D_MODEL - D_ATTN
CONV_WIDTH = 3
CMP_BLOCK = 32
CMP_STRIDE = 16
CMP_SHIFT = 4
CMP_HIDDEN = 256
SEL_BLOCK = 64
SEL_SHIFT = 6
SEL_RATIO = SEL_BLOCK // CMP_STRIDE
N_SELECT = 16
N_FORCED = 3
WINDOW = 512
ROPE_THETA = 500000.0
ROPE_DIM = HEAD_DIM // 4
ROPE_HALF = ROPE_DIM // 2
D_FF = 2816
N_GATES = 3 * N_HEADS
EPS = 1e-6
NEG_INF = -1e30
FORCE_SCORE = 1e9
SCALE = 1.0 / math.sqrt(HEAD_DIM)

LANES = 128
SUBLANES = 8
GATE_COLS = N_GROUPS * LANES
VMEM_LIMIT = 56 * 1024 * 1024
FF_CHUNKS = ((0, 768), (768, 1536), (1536, 2304), (2304, 2816))
N_SHIFT_COLS = 2
SAFE_SCORE_BOUND = 40.0

F32 = jnp.float32
BF16 = jnp.bfloat16


def _const_spec(shape):
    nd = len(shape)
    return pl.BlockSpec(shape, lambda *_: (0,) * nd, pipeline_mode=pl.Buffered(1))


def _rms(x, g):
    ms = jnp.mean(x * x, axis=-1, keepdims=True)
    return x * lax.rsqrt(ms + EPS) * g


def _ffn(x, g_pre, wg_ref, wu_ref, wd_ref, g_post):
    h = _rms(x, g_pre).astype(BF16)
    d = None
    for c0, c1 in FF_CHUNKS:
        gate = jnp.dot(h, wg_ref[:, c0:c1], preferred_element_type=F32)
        up = jnp.dot(h, wu_ref[:, c0:c1], preferred_element_type=F32)
        a = (gate * jax.nn.sigmoid(gate) * up).astype(BF16)
        part = jnp.dot(a, wd_ref[c0:c1, :], preferred_element_type=F32)
        d = part if d is None else d + part
    return x + 0.5 * _rms(d, g_post)


def _ffn_kernel(x_ref, gpre_ref, wg_ref, wu_ref, wd_ref, gpost_ref, o_ref):
    o_ref[...] = _ffn(x_ref[...], gpre_ref[...], wg_ref, wu_ref, wd_ref, gpost_ref[...])


def _ffn_call(x2d, g_pre, wg, wu, wd, g_post, tm):
    t = x2d.shape[0]
    row = pl.BlockSpec((tm, D_MODEL), lambda i: (i, 0))
    return pl.pallas_call(
        _ffn_kernel,
        grid=(t // tm,),
        in_specs=[row, _const_spec((1, D_MODEL)), _const_spec(wg.shape), _const_spec(wu.shape),
                  _const_spec(wd.shape), _const_spec((1, D_MODEL))],
        out_specs=row,
        out_shape=jax.ShapeDtypeStruct(x2d.shape, F32),
        compiler_params=pltpu.CompilerParams(dimension_semantics=("arbitrary",),
                                             vmem_limit_bytes=VMEM_LIMIT),
        name="ffn1",
    )(x2d, g_pre, wg, wu, wd, g_post)


_C_Q = 0
_C_KC = _C_Q + D_ATTN
_C_VC = _C_KC + D_KV
_C_KS = _C_VC + D_KV
_C_VS = _C_KS + D_KV
_C_KW = _C_VS + D_KV
_C_VW = _C_KW + D_KV
_C_GATE = _C_VW + D_KV
_C_BG = _C_GATE + GATE_COLS
_C_CG = _C_BG + D_CONV
_C_XC = _C_CG + D_CONV
_C_END = _C_XC + D_CONV


def _rope_kernel(pos_ref, invf_ref, cos_ref, sin_ref):
    ang = pos_ref[0].astype(F32) * invf_ref[...]
    cos_ref[0] = jnp.cos(ang)
    sin_ref[0] = jnp.sin(ang)


def _rope_call(pos3, invf, tm):
    b, s, _ = pos3.shape
    row = lambda width: pl.BlockSpec((1, tm, width), lambda bi, i: (bi, i, 0))
    shape = jax.ShapeDtypeStruct((b, s, LANES), F32)
    return pl.pallas_call(
        _rope_kernel,
        grid=(b, s // tm),
        in_specs=[row(1), _const_spec((1, LANES))],
        out_specs=[row(LANES), row(LANES)],
        out_shape=[shape, shape],
        compiler_params=pltpu.CompilerParams(dimension_semantics=("arbitrary", "arbitrary"),
                                             vmem_limit_bytes=VMEM_LIMIT),
        name="rope_tables",
    )(pos3, invf)


def _inproj_kernel(nbh, x_ref, cos_ref, sin_ref, g_ref, w_ref, convw_ref, convg_ref,
                   qraw_ref, qrot_ref, kc_ref, vc_ref, ksaug_ref, vs_ref, kw_ref, vw_ref,
                   gates_ref, convn_ref, ubuf_ref):
    i = pl.program_id(1)
    tm = x_ref.shape[1]
    h = _rms(x_ref[0], g_ref[...]).astype(BF16)

    def proj(c0, c1):
        return jnp.dot(h, w_ref[:, c0:c1], preferred_element_type=F32)

    cos = cos_ref[0]
    sin = sin_ref[0]
    lane = lax.broadcasted_iota(jnp.int32, (1, LANES), 1) & (HEAD_DIM - 1)
    s_lo = jnp.where(lane < ROPE_HALF, -sin, 0.0)
    s_hi = jnp.where(lane >= ROPE_HALF, sin, 0.0)

    def rope(z):
        return (z * cos + pltpu.roll(z, LANES - ROPE_HALF, 1) * s_lo
                + pltpu.roll(z, ROPE_HALF, 1) * s_hi)

    for c in range(0, D_ATTN // LANES, 2):
        zq2 = proj(_C_Q + c * LANES, _C_Q + (c + 2) * LANES)
        for cc in (c, c + 1):
            zq = zq2[:, (cc - c) * LANES:(cc - c + 1) * LANES]
            qraw_ref[0, :, cc * LANES:(cc + 1) * LANES] = zq.astype(BF16)
            qrot_ref[0, :, cc * LANES:(cc + 1) * LANES] = rope(zq).astype(BF16)

    kvc = proj(_C_KC, _C_KS)
    kc_ref[0] = kvc[:, 0:D_KV]
    vc_ref[0] = kvc[:, D_KV:2 * D_KV]
    kvs = proj(_C_KS, _C_KW)
    ks = rope(kvs[:, 0:D_KV]).astype(BF16)
    vs = kvs[:, D_KV:2 * D_KV].astype(BF16)
    kvw = proj(_C_KW, _C_GATE)
    kw = rope(kvw[:, 0:D_KV]).astype(BF16)
    vw = kvw[:, D_KV:2 * D_KV].astype(BF16)
    row_blk = ((i * tm + lax.broadcasted_iota(jnp.int32, (tm, nbh), 0)) >> SEL_SHIFT) & (nbh - 1)
    onehot = jnp.where(row_blk == lax.broadcasted_iota(jnp.int32, (tm, nbh), 1), 1.0, 0.0).astype(BF16)
    tail = jnp.where(lax.broadcasted_iota(jnp.int32, (tm, HEAD_DIM), 1) < N_SHIFT_COLS, 1.0, 0.0).astype(BF16)
    for g in range(N_GROUPS):
        sl = slice(g * HEAD_DIM, (g + 1) * HEAD_DIM)
        ksaug_ref[0, g, :, 0:nbh] = onehot
        ksaug_ref[0, g, :, nbh:nbh + HEAD_DIM] = ks[:, sl]
        ksaug_ref[0, g, :, nbh + HEAD_DIM:nbh + 2 * HEAD_DIM] = tail
        vs_ref[0, g, :, 0:HEAD_DIM] = vs[:, sl]
        vs_ref[0, g, :, HEAD_DIM:2 * HEAD_DIM] = tail
        kw_ref[0, g, :, 0:HEAD_DIM] = kw[:, sl]
        kw_ref[0, g, :, HEAD_DIM:2 * HEAD_DIM] = tail
        vw_ref[0, g, :, 0:HEAD_DIM] = vw[:, sl]
        vw_ref[0, g, :, HEAD_DIM:2 * HEAD_DIM] = tail

    gates_ref[0] = jax.nn.sigmoid(proj(_C_GATE, _C_BG))

    u = proj(_C_CG, _C_XC) * proj(_C_XC, _C_END)

    @pl.when(i == 0)
    def _():
        ubuf_ref[0:8, :] = jnp.zeros((8, D_CONV), F32)

    @pl.when(i > 0)
    def _():
        ubuf_ref[0:8, :] = ubuf_ref[tm:tm + 8, :]

    ubuf_ref[8:tm + 8, :] = u
    w = convw_ref[...]
    y = (w[2:3, :] * u + w[1:2, :] * ubuf_ref[7:tm + 7, :] + w[0:1, :] * ubuf_ref[6:tm + 6, :])
    conv = proj(_C_BG, _C_CG) * y
    convn_ref[0] = _rms(conv, convg_ref[...]).astype(BF16)


def _inproj_call(x, cos, sin, g_pre, w_in, conv_w, conv_g, tm, nbh):
    b, s, _ = x.shape
    grid = (b, s // tm)
    row = lambda width: pl.BlockSpec((1, tm, width), lambda bi, i: (bi, i, 0))
    grp = lambda width: pl.BlockSpec((1, N_GROUPS, tm, width), lambda bi, i: (bi, 0, i, 0))
    kern = functools.partial(_inproj_kernel, nbh)
    return pl.pallas_call(
        kern,
        grid=grid,
        in_specs=[row(D_MODEL), row(LANES), row(LANES), _const_spec((1, D_MODEL)), _const_spec(w_in.shape),
                  _const_spec((CONV_WIDTH, D_CONV)), _const_spec((1, D_CONV))],
        out_specs=[row(D_ATTN), row(D_ATTN), row(D_KV), row(D_KV), grp(nbh + 2 * HEAD_DIM), grp(2 * HEAD_DIM),
                   grp(2 * HEAD_DIM), grp(2 * HEAD_DIM), row(GATE_COLS), row(D_CONV)],
        out_shape=[
            jax.ShapeDtypeStruct((b, s, D_ATTN), BF16),
            jax.ShapeDtypeStruct((b, s, D_ATTN), BF16),
            jax.ShapeDtypeStruct((b, s, D_KV), F32),
            jax.ShapeDtypeStruct((b, s, D_KV), F32),
            jax.ShapeDtypeStruct((b, N_GROUPS, s, nbh + 2 * HEAD_DIM), BF16),
            jax.ShapeDtypeStruct((b, N_GROUPS, s, 2 * HEAD_DIM), BF16),
            jax.ShapeDtypeStruct((b, N_GROUPS, s, 2 * HEAD_DIM), BF16),
            jax.ShapeDtypeStruct((b, N_GROUPS, s, 2 * HEAD_DIM), BF16),
            jax.ShapeDtypeStruct((b, s, GATE_COLS), F32),
            jax.ShapeDtypeStruct((b, s, D_CONV), BF16),
        ],
        scratch_shapes=[pltpu.VMEM((tm + 8, D_CONV), F32)],
        compiler_params=pltpu.CompilerParams(dimension_semantics=("arbitrary", "arbitrary"),
                                             vmem_limit_bytes=VMEM_LIMIT),
        name="inproj",
    )(x, cos, sin, g_pre, w_in, conv_w, conv_g)


def _compress_kernel(kc_ref, kcn_ref, vc_ref, vcn_ref, pek_ref, w1k_ref, w2k_ref,
                     pev_ref, w1v_ref, w2v_ref, ko_ref, vo_ref):
    tc = kc_ref.shape[1]
    hid_w = N_GROUPS * CMP_HIDDEN
    last = lax.broadcasted_iota(jnp.int32, (tc, 1), 0) == tc - 1

    def one(x_ref, xn_ref, pe_ref, w1_ref, w2_ref, o_ref):
        x = x_ref[0]
        top = jnp.dot((x + pe_ref[0:1, :]).astype(BF16), w1_ref[0], preferred_element_type=F32)
        xb = (x + pe_ref[1:2, :]).astype(BF16)
        bot = jnp.dot(xb, w1_ref[1], preferred_element_type=F32)
        xnb = (xn_ref[0] + pe_ref[1:2, :]).astype(BF16)
        botn = jnp.dot(xnb, w1_ref[1], preferred_element_type=F32)
        shifted = jnp.where(last, botn[0:1, :], pltpu.roll(bot, tc - 1, 0))
        hid = jax.nn.gelu(top + shifted).astype(BF16)
        out = jnp.dot(hid, w2_ref[...], preferred_element_type=F32)
        tail = jnp.where(lax.broadcasted_iota(jnp.int32, (tc, HEAD_DIM), 1) < N_SHIFT_COLS, 1.0, 0.0).astype(BF16)
        for g in range(N_GROUPS):
            o_ref[0, g, :, 0:HEAD_DIM] = out[:, g * HEAD_DIM:(g + 1) * HEAD_DIM].astype(BF16)
            o_ref[0, g, :, HEAD_DIM:2 * HEAD_DIM] = tail

    one(kc_ref, kcn_ref, pek_ref, w1k_ref, w2k_ref, ko_ref)
    one(vc_ref, vcn_ref, pev_ref, w1v_ref, w2v_ref, vo_ref)


def _compress_call(kc_in, vc_in, pek, w1k, w2k, pev, w1v, w2v, tc):
    b, nch, width = kc_in.shape
    nt = nch // tc
    last8 = nch // 8 - 1
    cur = pl.BlockSpec((1, tc, width), lambda bi, i: (bi, i, 0))
    nxt = pl.BlockSpec((1, 8, width), lambda bi, i: (bi, jnp.minimum((i + 1) * (tc // 8), last8), 0))
    out = pl.BlockSpec((1, N_GROUPS, tc, 2 * HEAD_DIM), lambda bi, i: (bi, 0, i, 0))
    oshape = jax.ShapeDtypeStruct((b, N_GROUPS, nch, 2 * HEAD_DIM), BF16)
    return pl.pallas_call(
        _compress_kernel,
        grid=(b, nt),
        in_specs=[cur, nxt, cur, nxt, _const_spec(pek.shape), _const_spec(w1k.shape), _const_spec(w2k.shape),
                  _const_spec(pev.shape), _const_spec(w1v.shape), _const_spec(w2v.shape)],
        out_specs=[out, out],
        out_shape=[oshape, oshape],
        compiler_params=pltpu.CompilerParams(dimension_semantics=("arbitrary", "arbitrary"),
                                             vmem_limit_bytes=VMEM_LIMIT),
        name="compress",
    )(kc_in, kc_in, vc_in, vc_in, pek, w1k, w2k, pev, w1v, w2v)


def _split3(x):
    hi = x.astype(BF16)
    r = x - hi.astype(F32)
    mid = r.astype(BF16)
    lo = (r - mid.astype(F32)).astype(BF16)
    return hi, mid, lo


def _tile_score_bound(qs, kmax):
    qf = qs.astype(F32)
    sq = qf * qf
    nrm2 = None
    for h in range(HPG):
        rows = jnp.sum(sq[:, h * HEAD_DIM:(h + 1) * HEAD_DIM], axis=-1, keepdims=True)
        top = jnp.max(rows, axis=0, keepdims=True)
        nrm2 = top if nrm2 is None else jnp.maximum(nrm2, top)
    return jnp.sqrt(nrm2) * kmax


def _shift_cols(shift):
    hi = shift.astype(BF16).astype(F32)
    lo = (shift - hi).astype(BF16).astype(F32)
    lane = lax.broadcasted_iota(jnp.int32, (1, HEAD_DIM), 1)
    return jnp.where(lane == 0, -hi, jnp.where(lane == 1, -lo, 0.0)).astype(BF16)


def _cmp_kernel(k_top, cw, q_ref, kaug_ref, vaug_ref, gates_ref, aggt_ref, o_ref, bias_ref,
                qaug_ref, e_ref, acc_ref, kmax_ref, pslc_ref):
    qt = pl.program_id(2)
    tq = q_ref.shape[1]
    nch = kaug_ref.shape[2]
    nsel = bias_ref.shape[3]
    m_rows = HPG * tq
    t0 = qt * tq
    tcol = slice(HEAD_DIM, 2 * HEAD_DIM)
    c_last = ((t0 + tq - CMP_BLOCK) >> CMP_SHIFT) // cw
    c_mask = jnp.maximum(c_last - 1, 0)

    @pl.when(qt == 0)
    def _():
        k = kaug_ref[0, 0, :, 0:HEAD_DIM].astype(F32)
        n_ok = lax.broadcasted_iota(jnp.int32, (nch, 1), 0) < nch - 1
        ksq = jnp.where(n_ok, jnp.sum(k * k, axis=-1, keepdims=True), 0.0)
        kmax_ref[...] = jnp.broadcast_to(jnp.sqrt(jnp.max(ksq, axis=0, keepdims=True)), kmax_ref.shape)

    qs = q_ref[0] * SCALE
    for h in range(HPG):
        qaug_ref[h * tq:(h + 1) * tq, 0:HEAD_DIM] = qs[:, h * HEAD_DIM:(h + 1) * HEAD_DIM]
    bound = _tile_score_bound(qs, kmax_ref[0:1, 0:1])
    safe = jnp.max(bound) <= SAFE_SCORE_BOUND

    def scores(c, masked):
        ka = kaug_ref[0, 0, pl.ds(pl.multiple_of(c * cw, cw), cw), :]
        s = lax.dot_general(qaug_ref[...], ka, (((1,), (1,)), ((), ())), preferred_element_type=F32)
        if masked:
            row_t = t0 + (lax.broadcasted_iota(jnp.int32, (m_rows, 1), 0) & (tq - 1))
            n_vis = (row_t - (CMP_BLOCK - 1)) >> CMP_SHIFT
            n = c * cw + lax.broadcasted_iota(jnp.int32, (m_rows, cw), 1)
            s = jnp.where(n <= n_vis, s, NEG_INF)
        return s

    def sweep(fn_full, fn_masked, init):
        carry = lax.fori_loop(0, c_mask, fn_full, init)
        return lax.fori_loop(c_mask, c_last + 1, fn_masked, carry)

    @pl.when(safe)
    def _():
        qaug_ref[:, tcol] = jnp.broadcast_to(_shift_cols(bound), (m_rows, HEAD_DIM))

    @pl.when(jnp.logical_not(safe))
    def _():
        qaug_ref[:, tcol] = jnp.zeros((m_rows, HEAD_DIM), BF16)
        step = lambda masked: (lambda c, m: jnp.maximum(m, jnp.max(scores(c, masked), axis=-1, keepdims=True)))
        mx = sweep(step(False), step(True), jnp.full((m_rows, 1), NEG_INF, F32))
        qaug_ref[:, tcol] = _shift_cols(jnp.where(mx > 0.5 * NEG_INF, mx, 0.0))

    acc_ref[...] = jnp.zeros(acc_ref.shape, F32)

    def chunk(masked):
        def body(c, carry):
            e = jnp.exp(scores(c, masked))
            start = pl.multiple_of(c * cw, cw)
            e_ref[c] = e
            acc_ref[...] += jnp.dot(e.astype(BF16), vaug_ref[0, 0, pl.ds(start, cw), :],
                                    preferred_element_type=F32)
            return carry
        return body

    sweep(chunk(False), chunk(True), 0)
    acc = acc_ref[...]
    l = acc[:, HEAD_DIM:HEAD_DIM + 1]
    rinv = jnp.where(l > 0.0, 1.0 / l, 0.0)
    o = acc[:, 0:HEAD_DIM] * rinv
    gates = gates_ref[0]
    o_ref[0] = jnp.concatenate(
        [o[h * tq:(h + 1) * tq] * gates[:, h:h + 1] for h in range(HPG)], axis=-1)

    pslc_ref[...] = jnp.zeros(pslc_ref.shape, F32)

    def agg_body(c, carry):
        imp = sum(e_ref[c, h * tq:(h + 1) * tq, :] * rinv[h * tq:(h + 1) * tq] for h in range(HPG))
        at = aggt_ref[c]
        pslc_ref[...] += sum(lax.dot_general(at, part, (((1,), (1,)), ((), ())), preferred_element_type=F32)
                             for part in _split3(imp))
        return carry

    lax.fori_loop(0, c_last + 1, agg_body, 0)

    n_slabs = nsel // SUBLANES
    sub = lax.broadcasted_iota(jnp.int32, (SUBLANES, LANES), 0)
    sub_f = sub.astype(F32)
    no_slab = float(n_slabs)
    def tree(fn, xs):
        while len(xs) > 1:
            xs = [fn(xs[i], xs[i + 1]) if i + 1 < len(xs) else xs[i] for i in range(0, len(xs), 2)]
        return xs[0]

    def first_slab(score, m):
        firsts = []
        for g0 in range(0, n_slabs, SUBLANES):
            slab = jnp.full((SUBLANES, LANES), no_slab, F32)
            for r in reversed(range(g0, min(g0 + SUBLANES, n_slabs))):
                slab = jnp.where(score[r] == m, float(r), slab)
            firsts.append(slab)
        return tree(jnp.minimum, firsts)

    for col in range(tq // LANES):
        lanes = slice(col * LANES, (col + 1) * LANES)
        cur = (t0 + col * LANES + lax.broadcasted_iota(jnp.int32, (1, LANES), 1)) >> SEL_SHIFT
        score = []
        for r in range(n_slabs):
            j = sub + r * SUBLANES
            free = (j <= cur - 2) & (j > 0)
            score.append(jnp.where(free, pslc_ref[r * SUBLANES:(r + 1) * SUBLANES, lanes], -jnp.inf))
        for _ in range(max(k_top - N_FORCED, 0)):
            m = jnp.max(tree(jnp.maximum, score), axis=0, keepdims=True)
            slab = first_slab(score, m)
            block = slab * SUBLANES + sub_f
            first = jnp.min(block, axis=0, keepdims=True)
            taken = jnp.where(block == first, slab, -1.0)
            score = [jnp.where(taken == float(r), -jnp.inf, score[r]) for r in range(n_slabs)]
        for r in range(n_slabs):
            pslc_ref[r * SUBLANES:(r + 1) * SUBLANES, lanes] = jnp.where(
                (sub + r * SUBLANES <= cur) & (score[r] == -jnp.inf), 0.0, NEG_INF)
    bias_ref[0, 0] = pslc_ref[...].T.astype(BF16)


def _cmp_call(q_raw, kaug, vaug, gates, aggt, tq, cw):
    b, s, _ = q_raw.shape
    nch = kaug.shape[2]
    nsel = aggt.shape[1]
    assert nch % cw == 0 and tq // CMP_STRIDE < cw
    k_top = min(N_SELECT, nsel)
    gw = HPG * HEAD_DIM
    m_rows = HPG * tq
    qspec = pl.BlockSpec((1, tq, gw), lambda bi, g, i: (bi, i, g))
    kvspec = pl.BlockSpec((1, 1, nch, 2 * HEAD_DIM), lambda bi, g, i: (bi, g, 0, 0))
    gspec = pl.BlockSpec((1, tq, LANES), lambda bi, g, i: (bi, i, g))
    return pl.pallas_call(
        functools.partial(_cmp_kernel, k_top, cw),
        grid=(b, N_GROUPS, s // tq),
        in_specs=[qspec, kvspec, kvspec, gspec, _const_spec(aggt.shape)],
        out_specs=[qspec, pl.BlockSpec((1, 1, tq, nsel), lambda bi, g, i: (bi, g, i, 0))],
        out_shape=[jax.ShapeDtypeStruct((b, s, D_ATTN), F32),
                   jax.ShapeDtypeStruct((b, N_GROUPS, s, nsel), BF16)],
        scratch_shapes=[pltpu.VMEM((m_rows, 2 * HEAD_DIM), BF16),
                        pltpu.VMEM((nch // cw, m_rows, cw), F32),
                        pltpu.VMEM((m_rows, 2 * HEAD_DIM), F32),
                        pltpu.VMEM((8, LANES), F32),
                        pltpu.VMEM((nsel, tq), F32)],
        compiler_params=pltpu.CompilerParams(dimension_semantics=("arbitrary",) * 3,
                                             vmem_limit_bytes=VMEM_LIMIT),
        name="cmp_attn_topk",
    )(q_raw, kaug, vaug, gates, aggt)


def _slc_kernel(nbh, tk, q_ref, bias_ref, kaug_ref, vaug_ref, gates_ref, o_ref,
                qaug_ref, acc_ref, kmax_ref):
    qt = pl.program_id(2)
    tq = q_ref.shape[1]
    m_rows = HPG * tq
    t0 = qt * tq
    n_wide = t0 // tk
    n_narrow = (t0 - n_wide * tk) // tq
    n_half = qaug_ref.shape[0]
    kcol = slice(nbh, nbh + HEAD_DIM)
    tcol = slice(nbh + HEAD_DIM, nbh + 2 * HEAD_DIM)

    @pl.when(qt == 0)
    def _():
        def body(c, mx):
            k = kaug_ref[0, 0, pl.ds(pl.multiple_of(c * tk, tk), tk), kcol].astype(F32)
            return jnp.maximum(mx, jnp.sum(k * k, axis=-1, keepdims=True))
        mx = lax.fori_loop(0, kaug_ref.shape[2] // tk, body, jnp.zeros((tk, 1), F32))
        kmax_ref[...] = jnp.broadcast_to(jnp.sqrt(jnp.max(mx, axis=0, keepdims=True)), kmax_ref.shape)

    qs = q_ref[0] * SCALE
    bias = bias_ref[0, 0]
    for h in range(HPG):
        rows = slice(h * tq, (h + 1) * tq)
        for hf in range(n_half):
            qaug_ref[hf, rows, 0:nbh] = bias[:, hf * nbh:(hf + 1) * nbh]
            qaug_ref[hf, rows, kcol] = qs[:, h * HEAD_DIM:(h + 1) * HEAD_DIM]
    bound = _tile_score_bound(qs, kmax_ref[0:1, 0:1])
    safe = jnp.max(bound) <= SAFE_SCORE_BOUND

    def scores(start, width, diagonal):
        ka = kaug_ref[0, 0, pl.ds(start, width), :]
        qa = qaug_ref[start // (nbh * SEL_BLOCK)]
        s = lax.dot_general(qa, ka, (((1,), (1,)), ((), ())), preferred_element_type=F32)
        if diagonal:
            r = lax.broadcasted_iota(jnp.int32, (m_rows, width), 0) & (tq - 1)
            c = lax.broadcasted_iota(jnp.int32, (m_rows, width), 1)
            s = jnp.where(c <= r, s, NEG_INF)
        return s

    def sweep(fn, carry):
        carry = lax.fori_loop(
            0, n_wide, lambda i, c: fn(pl.multiple_of(i * tk, tk), tk, False, c), carry)
        carry = lax.fori_loop(
            0, n_narrow, lambda i, c: fn(pl.multiple_of(n_wide * tk + i * tq, tq), tq, False, c), carry)
        return fn(pl.multiple_of(t0, tq), tq, True, carry)

    @pl.when(safe)
    def _():
        tail = jnp.broadcast_to(_shift_cols(bound), (m_rows, HEAD_DIM))
        for hf in range(n_half):
            qaug_ref[hf, :, tcol] = tail

    @pl.when(jnp.logical_not(safe))
    def _():
        for hf in range(n_half):
            qaug_ref[hf, :, tcol] = jnp.zeros((m_rows, HEAD_DIM), BF16)
        mx = sweep(lambda st, w, dg, m: jnp.maximum(m, jnp.max(scores(st, w, dg), axis=-1, keepdims=True)),
                   jnp.full((m_rows, 1), NEG_INF, F32))
        tail = _shift_cols(mx)
        for hf in range(n_half):
            qaug_ref[hf, :, tcol] = tail

    acc_ref[...] = jnp.zeros(acc_ref.shape, F32)

    def tile(start, width, diagonal, carry):
        p = jnp.exp(scores(start, width, diagonal)).astype(BF16)
        acc_ref[...] += jnp.dot(p, vaug_ref[0, 0, pl.ds(start, width), :], preferred_element_type=F32)
        return carry

    sweep(tile, 0)
    acc = acc_ref[...]
    o = acc[:, 0:HEAD_DIM] / acc[:, HEAD_DIM:HEAD_DIM + 1]
    gates = gates_ref[0]
    o_ref[0] = jnp.concatenate(
        [o[h * tq:(h + 1) * tq] * gates[:, HPG + h:HPG + h + 1] for h in range(HPG)], axis=-1)


def _slc_call(q_rot, bias, kaug, v, gates, tq, tk, nbh):
    b, s, _ = q_rot.shape
    assert tk % tq == 0 and (nbh * SEL_BLOCK) % tk == 0 and s % tk == 0
    nsel = bias.shape[3]
    gw = HPG * HEAD_DIM
    aug = nbh + 2 * HEAD_DIM
    qspec = pl.BlockSpec((1, tq, gw), lambda bi, g, i: (bi, i, g))
    return pl.pallas_call(
        functools.partial(_slc_kernel, nbh, tk),
        grid=(b, N_GROUPS, s // tq),
        in_specs=[qspec,
                  pl.BlockSpec((1, 1, tq, nsel), lambda bi, g, i: (bi, g, i, 0)),
                  pl.BlockSpec((1, 1, s, aug), lambda bi, g, i: (bi, g, 0, 0)),
                  pl.BlockSpec((1, 1, s, 2 * HEAD_DIM), lambda bi, g, i: (bi, g, 0, 0)),
                  pl.BlockSpec((1, tq, LANES), lambda bi, g, i: (bi, i, g))],
        out_specs=qspec,
        out_shape=jax.ShapeDtypeStruct((b, s, D_ATTN), F32),
        scratch_shapes=[pltpu.VMEM((nsel // nbh, HPG * tq, aug), BF16),
                        pltpu.VMEM((HPG * tq, 2 * HEAD_DIM), F32),
                        pltpu.VMEM((8, LANES), F32)],
        compiler_params=pltpu.CompilerParams(dimension_semantics=("arbitrary",) * 3,
                                             vmem_limit_bytes=VMEM_LIMIT),
        name="slc_attn",
    )(q_rot, bias, kaug, v, gates)


def _win_kernel(tq, q_ref, kaug_ref, vaug_ref, gates_ref, o_ref, qaug_ref, kmax_ref):
    qb = pl.program_id(2)
    tb = q_ref.shape[1]
    n_sub = tb // tq
    span = WINDOW + tq
    m_rows = HPG * tq
    tcol = slice(HEAD_DIM, 2 * HEAD_DIM)

    @pl.when(qb == 0)
    def _():
        def body(c, mx):
            k = kaug_ref[0, 0, pl.ds(pl.multiple_of(c * tq, tq), tq), 0:HEAD_DIM].astype(F32)
            return jnp.maximum(mx, jnp.sum(k * k, axis=-1, keepdims=True))
        mx = lax.fori_loop(0, kaug_ref.shape[2] // tq, body, jnp.zeros((tq, 1), F32))
        kmax_ref[...] = jnp.broadcast_to(jnp.sqrt(jnp.max(mx, axis=0, keepdims=True)), kmax_ref.shape)

    qs = q_ref[0] * SCALE
    for j in range(n_sub):
        for h in range(HPG):
            qaug_ref[j, h * tq:(h + 1) * tq, 0:HEAD_DIM] = qs[j * tq:(j + 1) * tq, h * HEAD_DIM:(h + 1) * HEAD_DIM]
    bound = _tile_score_bound(qs, kmax_ref[0:1, 0:1])
    safe = jnp.max(bound) <= SAFE_SCORE_BOUND

    def band_start(j):
        return pl.multiple_of(jnp.maximum(qb * tb + j * tq - WINDOW, 0), tq)

    def scores(j):
        t0 = qb * tb + j * tq
        start = band_start(j)
        ka = kaug_ref[0, 0, pl.ds(start, span), :]
        s = lax.dot_general(qaug_ref[j], ka, (((1,), (1,)), ((), ())), preferred_element_type=F32)
        d = (lax.broadcasted_iota(jnp.int32, (m_rows, tq), 1)
             - (lax.broadcasted_iota(jnp.int32, (m_rows, tq), 0) & (tq - 1)))
        blocks = []
        for blk in range(span // tq):
            off = t0 - start - blk * tq
            ok = lax.bitcast_convert_type(off - d, jnp.uint32) < jnp.uint32(WINDOW)
            blocks.append(jnp.where(ok, s[:, blk * tq:(blk + 1) * tq], NEG_INF))
        return jnp.concatenate(blocks, axis=-1)

    @pl.when(safe)
    def _():
        for j in range(n_sub):
            qaug_ref[j, :, tcol] = jnp.broadcast_to(_shift_cols(bound), (m_rows, HEAD_DIM))

    @pl.when(jnp.logical_not(safe))
    def _():
        for j in range(n_sub):
            qaug_ref[j, :, tcol] = jnp.zeros((m_rows, HEAD_DIM), BF16)
            qaug_ref[j, :, tcol] = _shift_cols(jnp.max(scores(j), axis=-1, keepdims=True))

    gates = gates_ref[0]
    for j in range(n_sub):
        p = jnp.exp(scores(j)).astype(BF16)
        acc = jnp.dot(p, vaug_ref[0, 0, pl.ds(band_start(j), span), :], preferred_element_type=F32)
        o = acc[:, 0:HEAD_DIM] / acc[:, HEAD_DIM:HEAD_DIM + 1]
        gj = gates[j * tq:(j + 1) * tq]
        o_ref[0, j * tq:(j + 1) * tq, :] = jnp.concatenate(
            [o[h * tq:(h + 1) * tq] * gj[:, 2 * HPG + h:2 * HPG + h + 1] for h in range(HPG)], axis=-1)


def _win_call(q_rot, k, v, gates, tb, tq):
    b, s, _ = q_rot.shape
    assert tb % tq == 0 and WINDOW % tq == 0
    gw = HPG * HEAD_DIM
    qspec = pl.BlockSpec((1, tb, gw), lambda bi, g, i: (bi, i, g))
    kvspec = pl.BlockSpec((1, 1, s, 2 * HEAD_DIM), lambda bi, g, i: (bi, g, 0, 0))
    return pl.pallas_call(
        functools.partial(_win_kernel, tq),
        grid=(b, N_GROUPS, s // tb),
        in_specs=[qspec, kvspec, kvspec, pl.BlockSpec((1, tb, LANES), lambda bi, g, i: (bi, i, g))],
        out_specs=qspec,
        out_shape=jax.ShapeDtypeStruct((b, s, D_ATTN), F32),
        scratch_shapes=[pltpu.VMEM((tb // tq, HPG * tq, 2 * HEAD_DIM), BF16),
                        pltpu.VMEM((8, LANES), F32)],
        compiler_params=pltpu.CompilerParams(dimension_semantics=("arbitrary",) * 3,
                                             vmem_limit_bytes=VMEM_LIMIT),
        name="win_attn",
    )(q_rot, k, v, gates)


def _out_kernel(x_ref, oc_ref, os_ref, ow_ref, convn_ref, ga_ref, wout_ref, gpost_ref,
                gpre2_ref, wg_ref, wu_ref, wd_ref, gpost2_ref, o_ref):
    attn = oc_ref[...] + os_ref[...] + ow_ref[...]
    an = _rms(attn, ga_ref[...]).astype(BF16)
    h = (jnp.dot(an, wout_ref[0:D_ATTN, :], preferred_element_type=F32)
         + jnp.dot(convn_ref[...], wout_ref[D_ATTN:D_MODEL, :], preferred_element_type=F32))
    x1 = x_ref[...] + _rms(h, gpost_ref[...])
    o_ref[...] = _ffn(x1, gpre2_ref[...], wg_ref, wu_ref, wd_ref, gpost2_ref[...])


def _out_call(x2d, oc, osl, ow, convn, ga, wout, gpost, gpre2, wg, wu, wd, gpost2, tm):
    t = x2d.shape[0]
    row = lambda width: pl.BlockSpec((tm, width), lambda i: (i, 0))
    return pl.pallas_call(
        _out_kernel,
        grid=(t // tm,),
        in_specs=[row(D_MODEL), row(D_ATTN), row(D_ATTN), row(D_ATTN), row(D_CONV),
                  _const_spec((1, D_ATTN)), _const_spec(wout.shape), _const_spec((1, D_MODEL)),
                  _const_spec((1, D_MODEL)), _const_spec(wg.shape), _const_spec(wu.shape),
                  _const_spec(wd.shape), _const_spec((1, D_MODEL))],
        out_specs=row(D_MODEL),
        out_shape=jax.ShapeDtypeStruct(x2d.shape, F32),
        compiler_params=pltpu.CompilerParams(dimension_semantics=("arbitrary",),
                                             vmem_limit_bytes=VMEM_LIMIT),
        name="outproj_ffn2",
    )(x2d, oc, osl, ow, convn, ga, wout, gpost, gpre2, wg, wu, wd, gpost2)


def _prep_w_in(w_in):
    sizes = [D_ATTN] + [D_KV] * 6 + [N_GATES] + [D_CONV] * 3
    cuts = np.cumsum([0] + sizes)
    q, kc, vc, ks, vs, kw, vw, gt, bg, cg, xc = [w_in[:, cuts[i]:cuts[i + 1]] for i in range(len(sizes))]
    gt = gt.reshape(D_MODEL, N_GROUPS, HPG, 3).transpose(0, 1, 3, 2).reshape(D_MODEL, N_GROUPS, 3 * HPG)
    gt = jnp.pad(gt, ((0, 0), (0, 0), (0, LANES - 3 * HPG))).reshape(D_MODEL, GATE_COLS)
    return jnp.concatenate([q, kc, vc, ks, vs, kw, vw, gt, bg, cg, xc], axis=1).astype(BF16)


def _prep_compress(pe, w1, w2):
    half = CMP_BLOCK // 2
    eye = jnp.eye(N_GROUPS, dtype=F32)
    w1r = w1.reshape(2, half, HEAD_DIM, CMP_HIDDEN)
    w1big = jnp.einsum("ptdc,gh->ptgdhc", w1r, eye).reshape(2, half * D_KV, N_GROUPS * CMP_HIDDEN)
    w2big = jnp.einsum("cd,gh->gchd", w2, eye).reshape(N_GROUPS * CMP_HIDDEN, D_KV)
    pe_rows = jnp.broadcast_to(pe.reshape(2, half, 1, HEAD_DIM), (2, half, N_GROUPS, HEAD_DIM))
    return pe_rows.reshape(2, half * D_KV), w1big.astype(BF16), w2big.astype(BF16)


def _agg_matrix(nch, nsel, cw):
    agg_w = np.convolve(np.ones(SEL_RATIO), np.ones(CMP_BLOCK // CMP_STRIDE))
    a = np.zeros((nch, nsel), np.float32)
    for j in range(nsel):
        for o, wgt in enumerate(agg_w):
            c = SEL_RATIO * j + o - (CMP_BLOCK // CMP_STRIDE - 1)
            if 0 <= c < nch - 1:
                a[c, j] = wgt
    a = a.T.reshape(nsel, nch // cw, cw).transpose(1, 0, 2)
    return jnp.asarray(a, BF16)


def _rope_inv_freq_row():
    inv = ROPE_THETA ** (-np.arange(ROPE_HALF, dtype=np.float32) * 2.0 / ROPE_DIM)
    lane = np.arange(LANES) % HEAD_DIM
    row = np.where(lane < ROPE_DIM, inv[lane % ROPE_HALF], 0.0).astype(np.float32)
    return jnp.asarray(row.reshape(1, LANES))


def _forward(x, positions, p, *, tm, tq, tc, nbh, tqs, tks, tqc, cw, tbw):
    b, s, _ = x.shape
    depth = p["w_in"].shape[0]
    nch = s // CMP_STRIDE
    nsel = s // SEL_BLOCK
    cos, sin = _rope_call(positions.reshape(b, s, 1), _rope_inv_freq_row(), tm)
    agg = _agg_matrix(nch, nsel, cw)
    row = lambda v: v.reshape(1, -1)
    for l in range(depth):
        x2d = x.reshape(b * s, D_MODEL)
        x2d = _ffn_call(x2d, row(p["ffn1_norm_pre"][l]), p["ffn1_w_gate"][l].astype(BF16),
                        p["ffn1_w_up"][l].astype(BF16), p["ffn1_w_down"][l].astype(BF16),
                        row(p["ffn1_norm_post"][l]), tm)
        (q_raw, q_rot, kc_in, vc_in, ksaug, vs, kw, vw, gates, convn) = _inproj_call(
            x2d.reshape(b, s, D_MODEL), cos, sin, row(p["mix_norm_pre"][l]), _prep_w_in(p["w_in"][l]),
            p["conv_w"][l], row(p["conv_out_norm"][l]), tm, nbh)
        pek, w1k, w2k = _prep_compress(p["cmp_pe_k"][l], p["cmp_w1_k"][l], p["cmp_w2_k"][l])
        pev, w1v, w2v = _prep_compress(p["cmp_pe_v"][l], p["cmp_w1_v"][l], p["cmp_w2_v"][l])
        kcmp, vcmp = _compress_call(kc_in.reshape(b, nch, CMP_STRIDE * D_KV),
                                    vc_in.reshape(b, nch, CMP_STRIDE * D_KV),
                                    pek, w1k, w2k, pev, w1v, w2v, tc)
        o_cmp, bias = _cmp_call(q_raw, kcmp, vcmp, gates, agg, tqc, cw)
        o_slc = _slc_call(q_rot, bias, ksaug, vs, gates, tqs, tks, nbh)
        o_win = _win_call(q_rot, kw, vw, gates, tbw, tq)
        flat = lambda a: a.reshape(b * s, a.shape[-1])
        x2d = _out_call(x2d, flat(o_cmp), flat(o_slc), flat(o_win), flat(convn),
                        row(p["attn_out_norm"][l]), p["w_out"][l].astype(BF16), row(p["mix_norm_post"][l]),
                        row(p["ffn2_norm_pre"][l]), p["ffn2_w_gate"][l].astype(BF16),
                        p["ffn2_w_up"][l].astype(BF16), p["ffn2_w_down"][l].astype(BF16),
                        row(p["ffn2_norm_post"][l]), tm)
        x = x2d.reshape(b, s, D_MODEL)
    return x


def kernel(x, positions, ffn1_norm_pre, ffn1_w_gate, ffn1_w_up, ffn1_w_down, ffn1_norm_post, mix_norm_pre, w_in, cmp_pe_k, cmp_w1_k, cmp_w2_k, cmp_pe_v, cmp_w1_v, cmp_w2_v, conv_w, attn_out_norm, conv_out_norm, w_out, mix_norm_post, ffn2_norm_pre, ffn2_w_gate, ffn2_w_up, ffn2_w_down, ffn2_norm_post):
    params = dict(
        ffn1_norm_pre=ffn1_norm_pre, ffn1_w_gate=ffn1_w_gate, ffn1_w_up=ffn1_w_up, ffn1_w_down=ffn1_w_down,
        ffn1_norm_post=ffn1_norm_post, mix_norm_pre=mix_norm_pre, w_in=w_in,
        cmp_pe_k=cmp_pe_k, cmp_w1_k=cmp_w1_k, cmp_w2_k=cmp_w2_k,
        cmp_pe_v=cmp_pe_v, cmp_w1_v=cmp_w1_v, cmp_w2_v=cmp_w2_v,
        conv_w=conv_w, attn_out_norm=attn_out_norm, conv_out_norm=conv_out_norm, w_out=w_out,
        mix_norm_post=mix_norm_post, ffn2_norm_pre=ffn2_norm_pre, ffn2_w_gate=ffn2_w_gate,
        ffn2_w_up=ffn2_w_up, ffn2_w_down=ffn2_w_down, ffn2_norm_post=ffn2_norm_post)
    return _forward(x, positions, params, tm=512, tq=256, tc=256, nbh=128, tqs=512, tks=1024, tqc=1024, cw=256, tbw=1024)
```

```python
import functools
import math

import numpy as np
import jax
import jax.numpy as jnp
from jax import lax
from jax.experimental import pallas as pl
from jax.experimental.pallas import tpu as pltpu

D_MODEL = 1024
N_HEADS = 8
HEAD_DIM = 64
N_GROUPS = 2
HPG = N_HEADS // N_GROUPS
D_ATTN = N_HEADS * HEAD_DIM
D_KV = N_GROUPS * HEAD_DIM
D_CONV = D_MODEL - D_ATTN
CONV_WIDTH = 3
CMP_BLOCK = 32
CMP_STRIDE = 16
CMP_SHIFT = 4
CMP_HIDDEN = 256
SEL_BLOCK = 64
SEL_SHIFT = 6
SEL_RATIO = SEL_BLOCK // CMP_STRIDE
N_SELECT = 16
N_FORCED = 3
WINDOW = 512
ROPE_THETA = 500000.0
ROPE_DIM = HEAD_DIM // 4
ROPE_HALF = ROPE_DIM // 2
D_FF = 2816
N_GATES = 3 * N_HEADS
EPS = 1e-6
NEG_INF = -1e30
FORCE_SCORE = 1e9
SCALE = 1.0 / math.sqrt(HEAD_DIM)

LANES = 128
SUBLANES = 8
GATE_COLS = N_GROUPS * LANES
VMEM_LIMIT = 56 * 1024 * 1024
FF_CHUNKS = ((0, 768), (768, 1536), (1536, 2304), (2304, 2816))
N_SHIFT_COLS = 2
SAFE_SCORE_BOUND = 40.0

F32 = jnp.float32
BF16 = jnp.bfloat16


def _const_spec(shape):
    nd = len(shape)
    return pl.BlockSpec(shape, lambda *_: (0,) * nd, pipeline_mode=pl.Buffered(1))


def _rms(x, g):
    ms = jnp.mean(x * x, axis=-1, keepdims=True)
    return x * lax.rsqrt(ms + EPS) * g


def _ffn(x, g_pre, wg_ref, wu_ref, wd_ref, g_post):
    h = _rms(x, g_pre).astype(BF16)
    d = None
    for c0, c1 in FF_CHUNKS:
        gate = jnp.dot(h, wg_ref[:, c0:c1], preferred_element_type=F32)
        up = jnp.dot(h, wu_ref[:, c0:c1], preferred_element_type=F32)
        a = (gate * jax.nn.sigmoid(gate) * up).astype(BF16)
        part = jnp.dot(a, wd_ref[c0:c1, :], preferred_element_type=F32)
        d = part if d is None else d + part
    return x + 0.5 * _rms(d, g_post)


def _ffn_kernel(x_ref, gpre_ref, wg_ref, wu_ref, wd_ref, gpost_ref, o_ref):
    o_ref[...] = _ffn(x_ref[...], gpre_ref[...], wg_ref, wu_ref, wd_ref, gpost_ref[...])


def _ffn_call(x2d, g_pre, wg, wu, wd, g_post, tm):
    t = x2d.shape[0]
    row = pl.BlockSpec((tm, D_MODEL), lambda i: (i, 0))
    return pl.pallas_call(
        _ffn_kernel,
        grid=(t // tm,),
        in_specs=[row, _const_spec((1, D_MODEL)), _const_spec(wg.shape), _const_spec(wu.shape),
                  _const_spec(wd.shape), _const_spec((1, D_MODEL))],
        out_specs=row,
        out_shape=jax.ShapeDtypeStruct(x2d.shape, F32),
        compiler_params=pltpu.CompilerParams(dimension_semantics=("arbitrary",),
                                             vmem_limit_bytes=VMEM_LIMIT),
        name="ffn1",
    )(x2d, g_pre, wg, wu, wd, g_post)


_C_Q = 0
_C_KC = _C_Q + D_ATTN
_C_VC = _C_KC + D_KV
_C_KS = _C_VC + D_KV
_C_VS = _C_KS + D_KV
_C_KW = _C_VS + D_KV
_C_VW = _C_KW + D_KV
_C_GATE = _C_VW + D_KV
_C_BG = _C_GATE + GATE_COLS
_C_CG = _C_BG + D_CONV
_C_XC = _C_CG + D_CONV
_C_END = _C_XC + D_CONV


def _rope_kernel(pos_ref, invf_ref, cos_ref, sin_ref):
    ang = pos_ref[0].astype(F32) * invf_ref[...]
    cos_ref[0] = jnp.cos(ang)
    sin_ref[0] = jnp.sin(ang)


def _rope_call(pos3, invf, tm):
    b, s, _ = pos3.shape
    row = lambda width: pl.BlockSpec((1, tm, width), lambda bi, i: (bi, i, 0))
    shape = jax.ShapeDtypeStruct((b, s, LANES), F32)
    return pl.pallas_call(
        _rope_kernel,
        grid=(b, s // tm),
        in_specs=[row(1), _const_spec((1, LANES))],
        out_specs=[row(LANES), row(LANES)],
        out_shape=[shape, shape],
        compiler_params=pltpu.CompilerParams(dimension_semantics=("arbitrary", "arbitrary"),
                                             vmem_limit_bytes=VMEM_LIMIT),
        name="rope_tables",
    )(pos3, invf)


def _inproj_kernel(nbh, x_ref, cos_ref, sin_ref, g_ref, w_ref, convw_ref, convg_ref,
                   qraw_ref, qrot_ref, kc_ref, vc_ref, ksaug_ref, vs_ref, kw_ref, vw_ref,
                   gates_ref, convn_ref, ubuf_ref):
    i = pl.program_id(1)
    tm = x_ref.shape[1]
    h = _rms(x_ref[0], g_ref[...]).astype(BF16)

    def proj(c0, c1):
        return jnp.dot(h, w_ref[:, c0:c1], preferred_element_type=F32)

    cos = cos_ref[0]
    sin = sin_ref[0]
    lane = lax.broadcasted_iota(jnp.int32, (1, LANES), 1) & (HEAD_DIM - 1)
    s_lo = jnp.where(lane < ROPE_HALF, -sin, 0.0)
    s_hi = jnp.where(lane >= ROPE_HALF, sin, 0.0)

    def rope(z):
        return (z * cos + pltpu.roll(z, LANES - ROPE_HALF, 1) * s_lo
                + pltpu.roll(z, ROPE_HALF, 1) * s_hi)

    for c in range(0, D_ATTN // LANES, 2):
        zq2 = proj(_C_Q + c * LANES, _C_Q + (c + 2) * LANES)
        for cc in (c, c + 1):
            zq = zq2[:, (cc - c) * LANES:(cc - c + 1) * LANES]
            qraw_ref[0, :, cc * LANES:(cc + 1) * LANES] = zq.astype(BF16)
            qrot_ref[0, :, cc * LANES:(cc + 1) * LANES] = rope(zq).astype(BF16)

    kvc = proj(_C_KC, _C_KS)
    kc_ref[0] = kvc[:, 0:D_KV]
    vc_ref[0] = kvc[:, D_KV:2 * D_KV]
    kvs = proj(_C_KS, _C_KW)
    ks = rope(kvs[:, 0:D_KV]).astype(BF16)
    vs = kvs[:, D_KV:2 * D_KV].astype(BF16)
    kvw = proj(_C_KW, _C_GATE)
    kw = rope(kvw[:, 0:D_KV]).astype(BF16)
    vw = kvw[:, D_KV:2 * D_KV].astype(BF16)
    row_blk = ((i * tm + lax.broadcasted_iota(jnp.int32, (tm, nbh), 0)) >> SEL_SHIFT) & (nbh - 1)
    onehot = jnp.where(row_blk == lax.broadcasted_iota(jnp.int32, (tm, nbh), 1), 1.0, 0.0).astype(BF16)
    tail = jnp.where(lax.broadcasted_iota(jnp.int32, (tm, HEAD_DIM), 1) < N_SHIFT_COLS, 1.0, 0.0).astype(BF16)
    for g in range(N_GROUPS):
        sl = slice(g * HEAD_DIM, (g + 1) * HEAD_DIM)
        ksaug_ref[0, g, :, 0:nbh] = onehot
        ksaug_ref[0, g, :, nbh:nbh + HEAD_DIM] = ks[:, sl]
        ksaug_ref[0, g, :, nbh + HEAD_DIM:nbh + 2 * HEAD_DIM] = tail
        vs_ref[0, g, :, 0:HEAD_DIM] = vs[:, sl]
        vs_ref[0, g, :, HEAD_DIM:2 * HEAD_DIM] = tail
        kw_ref[0, g, :, 0:HEAD_DIM] = kw[:, sl]
        kw_ref[0, g, :, HEAD_DIM:2 * HEAD_DIM] = tail
        vw_ref[0, g, :, 0:HEAD_DIM] = vw[:, sl]
        vw_ref[0, g, :, HEAD_DIM:2 * HEAD_DIM] = tail

    gates_ref[0] = jax.nn.sigmoid(proj(_C_GATE, _C_BG))

    u = proj(_C_CG, _C_XC) * proj(_C_XC, _C_END)

    @pl.when(i == 0)
    def _():
        ubuf_ref[0:8, :] = jnp.zeros((8, D_CONV), F32)

    @pl.when(i > 0)
    def _():
        ubuf_ref[0:8, :] = ubuf_ref[tm:tm + 8, :]

    ubuf_ref[8:tm + 8, :] = u
    w = convw_ref[...]
    y = (w[2:3, :] * u + w[1:2, :] * ubuf_ref[7:tm + 7, :] + w[0:1, :] * ubuf_ref[6:tm + 6, :])
    conv = proj(_C_BG, _C_CG) * y
    convn_ref[0] = _rms(conv, convg_ref[...]).astype(BF16)


def _inproj_call(x, cos, sin, g_pre, w_in, conv_w, conv_g, tm, nbh):
    b, s, _ = x.shape
    grid = (b, s // tm)
    row = lambda width: pl.BlockSpec((1, tm, width), lambda bi, i: (bi, i, 0))
    grp = lambda width: pl.BlockSpec((1, N_GROUPS, tm, width), lambda bi, i: (bi, 0, i, 0))
    kern = functools.partial(_inproj_kernel, nbh)
    return pl.pallas_call(
        kern,
        grid=grid,
        in_specs=[row(D_MODEL), row(LANES), row(LANES), _const_spec((1, D_MODEL)), _const_spec(w_in.shape),
                  _const_spec((CONV_WIDTH, D_CONV)), _const_spec((1, D_CONV))],
        out_specs=[row(D_ATTN), row(D_ATTN), row(D_KV), row(D_KV), grp(nbh + 2 * HEAD_DIM), grp(2 * HEAD_DIM),
                   grp(2 * HEAD_DIM), grp(2 * HEAD_DIM), row(GATE_COLS), row(D_CONV)],
        out_shape=[
            jax.ShapeDtypeStruct((b, s, D_ATTN), BF16),
            jax.ShapeDtypeStruct((b, s, D_ATTN), BF16),
            jax.ShapeDtypeStruct((b, s, D_KV), F32),
            jax.ShapeDtypeStruct((b, s, D_KV), F32),
            jax.ShapeDtypeStruct((b, N_GROUPS, s, nbh + 2 * HEAD_DIM), BF16),
            jax.ShapeDtypeStruct((b, N_GROUPS, s, 2 * HEAD_DIM), BF16),
            jax.ShapeDtypeStruct((b, N_GROUPS, s, 2 * HEAD_DIM), BF16),
            jax.ShapeDtypeStruct((b, N_GROUPS, s, 2 * HEAD_DIM), BF16),
            jax.ShapeDtypeStruct((b, s, GATE_COLS), F32),
            jax.ShapeDtypeStruct((b, s, D_CONV), BF16),
        ],
        scratch_shapes=[pltpu.VMEM((tm + 8, D_CONV), F32)],
        compiler_params=pltpu.CompilerParams(dimension_semantics=("arbitrary", "arbitrary"),
                                             vmem_limit_bytes=VMEM_LIMIT),
        name="inproj",
    )(x, cos, sin, g_pre, w_in, conv_w, conv_g)


def _compress_kernel(kc_ref, kcn_ref, vc_ref, vcn_ref, pek_ref, w1k_ref, w2k_ref,
                     pev_ref, w1v_ref, w2v_ref, ko_ref, vo_ref):
    tc = kc_ref.shape[1]
    hid_w = N_GROUPS * CMP_HIDDEN
    last = lax.broadcasted_iota(jnp.int32, (tc, 1), 0) == tc - 1

    def one(x_ref, xn_ref, pe_ref, w1_ref, w2_ref, o_ref):
        x = x_ref[0]
        top = jnp.dot((x + pe_ref[0:1, :]).astype(BF16), w1_ref[0], preferred_element_type=F32)
        xb = (x + pe_ref[1:2, :]).astype(BF16)
        bot = jnp.dot(xb, w1_ref[1], preferred_element_type=F32)
        xnb = (xn_ref[0] + pe_ref[1:2, :]).astype(BF16)
        botn = jnp.dot(xnb, w1_ref[1], preferred_element_type=F32)
        shifted = jnp.where(last, botn[0:1, :], pltpu.roll(bot, tc - 1, 0))
        hid = jax.nn.gelu(top + shifted).astype(BF16)
        out = jnp.dot(hid, w2_ref[...], preferred_element_type=F32)
        tail = jnp.where(lax.broadcasted_iota(jnp.int32, (tc, HEAD_DIM), 1) < N_SHIFT_COLS, 1.0, 0.0).astype(BF16)
        for g in range(N_GROUPS):
            o_ref[0, g, :, 0:HEAD_DIM] = out[:, g * HEAD_DIM:(g + 1) * HEAD_DIM].astype(BF16)
            o_ref[0, g, :, HEAD_DIM:2 * HEAD_DIM] = tail

    one(kc_ref, kcn_ref, pek_ref, w1k_ref, w2k_ref, ko_ref)
    one(vc_ref, vcn_ref, pev_ref, w1v_ref, w2v_ref, vo_ref)


def _compress_call(kc_in, vc_in, pek, w1k, w2k, pev, w1v, w2v, tc):
    b, nch, width = kc_in.shape
    nt = nch // tc
    last8 = nch // 8 - 1
    cur = pl.BlockSpec((1, tc, width), lambda bi, i: (bi, i, 0))
    nxt = pl.BlockSpec((1, 8, width), lambda bi, i: (bi, jnp.minimum((i + 1) * (tc // 8), last8), 0))
    out = pl.BlockSpec((1, N_GROUPS, tc, 2 * HEAD_DIM), lambda bi, i: (bi, 0, i, 0))
    oshape = jax.ShapeDtypeStruct((b, N_GROUPS, nch, 2 * HEAD_DIM), BF16)
    return pl.pallas_call(
        _compress_kernel,
        grid=(b, nt),
        in_specs=[cur, nxt, cur, nxt, _const_spec(pek.shape), _const_spec(w1k.shape), _const_spec(w2k.shape),
                  _const_spec(pev.shape), _const_spec(w1v.shape), _const_spec(w2v.shape)],
        out_specs=[out, out],
        out_shape=[oshape, oshape],
        compiler_params=pltpu.CompilerParams(dimension_semantics=("arbitrary", "arbitrary"),
                                             vmem_limit_bytes=VMEM_LIMIT),
        name="compress",
    )(kc_in, kc_in, vc_in, vc_in, pek, w1k, w2k, pev, w1v, w2v)


def _split3(x):
    hi = x.astype(BF16)
    r = x - hi.astype(F32)
    mid = r.astype(BF16)
    lo = (r - mid.astype(F32)).astype(BF16)
    return hi, mid, lo


def _tile_score_bound(qs, kmax):
    qf = qs.astype(F32)
    sq = qf * qf
    nrm2 = None
    for h in range(HPG):
        rows = jnp.sum(sq[:, h * HEAD_DIM:(h + 1) * HEAD_DIM], axis=-1, keepdims=True)
        top = jnp.max(rows, axis=0, keepdims=True)
        nrm2 = top if nrm2 is None else jnp.maximum(nrm2, top)
    return jnp.sqrt(nrm2) * kmax


def _shift_cols(shift):
    hi = shift.astype(BF16).astype(F32)
    lo = (shift - hi).astype(BF16).astype(F32)
    lane = lax.broadcasted_iota(jnp.int32, (1, HEAD_DIM), 1)
    return jnp.where(lane == 0, -hi, jnp.where(lane == 1, -lo, 0.0)).astype(BF16)


def _cmp_kernel(k_top, cw, q_ref, kaug_ref, vaug_ref, gates_ref, aggt_ref, o_ref, bias_ref,
                qaug_ref, e_ref, acc_ref, kmax_ref, pslc_ref):
    qt = pl.program_id(2)
    tq = q_ref.shape[1]
    nch = kaug_ref.shape[2]
    nsel = bias_ref.shape[3]
    m_rows = HPG * tq
    t0 = qt * tq
    tcol = slice(HEAD_DIM, 2 * HEAD_DIM)
    c_last = ((t0 + tq - CMP_BLOCK) >> CMP_SHIFT) // cw
    c_mask = jnp.maximum(c_last - 1, 0)

    @pl.when(qt == 0)
    def _():
        k = kaug_ref[0, 0, :, 0:HEAD_DIM].astype(F32)
        n_ok = lax.broadcasted_iota(jnp.int32, (nch, 1), 0) < nch - 1
        ksq = jnp.where(n_ok, jnp.sum(k * k, axis=-1, keepdims=True), 0.0)
        kmax_ref[...] = jnp.broadcast_to(jnp.sqrt(jnp.max(ksq, axis=0, keepdims=True)), kmax_ref.shape)

    qs = q_ref[0] * SCALE
    for h in range(HPG):
        qaug_ref[h * tq:(h + 1) * tq, 0:HEAD_DIM] = qs[:, h * HEAD_DIM:(h + 1) * HEAD_DIM]
    bound = _tile_score_bound(qs, kmax_ref[0:1, 0:1])
    safe = jnp.max(bound) <= SAFE_SCORE_BOUND

    def scores(c, masked):
        ka = kaug_ref[0, 0, pl.ds(pl.multiple_of(c * cw, cw), cw), :]
        s = lax.dot_general(qaug_ref[...], ka, (((1,), (1,)), ((), ())), preferred_element_type=F32)
        if masked:
            row_t = t0 + (lax.broadcasted_iota(jnp.int32, (m_rows, 1), 0) & (tq - 1))
            n_vis = (row_t - (CMP_BLOCK - 1)) >> CMP_SHIFT
            n = c * cw + lax.broadcasted_iota(jnp.int32, (m_rows, cw), 1)
            s = jnp.where(n <= n_vis, s, NEG_INF)
        return s

    def sweep(fn_full, fn_masked, init):
        carry = lax.fori_loop(0, c_mask, fn_full, init)
        return lax.fori_loop(c_mask, c_last + 1, fn_masked, carry)

    @pl.when(safe)
    def _():
        qaug_ref[:, tcol] = jnp.broadcast_to(_shift_cols(bound), (m_rows, HEAD_DIM))

    @pl.when(jnp.logical_not(safe))
    def _():
        qaug_ref[:, tcol] = jnp.zeros((m_rows, HEAD_DIM), BF16)
        step = lambda masked: (lambda c, m: jnp.maximum(m, jnp.max(scores(c, masked), axis=-1, keepdims=True)))
        mx = sweep(step(False), step(True), jnp.full((m_rows, 1), NEG_INF, F32))
        qaug_ref[:, tcol] = _shift_cols(jnp.where(mx > 0.5 * NEG_INF, mx, 0.0))

    acc_ref[...] = jnp.zeros(acc_ref.shape, F32)

    def chunk(masked):
        def body(c, carry):
            e = jnp.exp(scores(c, masked))
            start = pl.multiple_of(c * cw, cw)
            e_ref[c] = e
            acc_ref[...] += jnp.dot(e.astype(BF16), vaug_ref[0, 0, pl.ds(start, cw), :],
                                    preferred_element_type=F32)
            return carry
        return body

    sweep(chunk(False), chunk(True), 0)
    acc = acc_ref[...]
    l = acc[:, HEAD_DIM:HEAD_DIM + 1]
    rinv = jnp.where(l > 0.0, 1.0 / l, 0.0)
    o = acc[:, 0:HEAD_DIM] * rinv
    gates = gates_ref[0]
    o_ref[0] = jnp.concatenate(
        [o[h * tq:(h + 1) * tq] * gates[:, h:h + 1] for h in range(HPG)], axis=-1)

    pslc_ref[...] = jnp.zeros(pslc_ref.shape, F32)

    def agg_body(c, carry):
        imp = sum(e_ref[c, h * tq:(h + 1) * tq, :] * rinv[h * tq:(h + 1) * tq] for h in range(HPG))
        at = aggt_ref[c]
        pslc_ref[...] += sum(lax.dot_general(at, part, (((1,), (1,)), ((), ())), preferred_element_type=F32)
                             for part in _split3(imp))
        return carry

    lax.fori_loop(0, c_last + 1, agg_body, 0)

    n_slabs = nsel // SUBLANES
    sub = lax.broadcasted_iota(jnp.int32, (SUBLANES, LANES), 0)
    sub_f = sub.astype(F32)
    no_slab = float(n_slabs)
    n_cols = tq // LANES

    def tree(fn, xs):
        while len(xs) > 1:
            xs = [fn(xs[i], xs[i + 1]) if i + 1 < len(xs) else xs[i] for i in range(0, len(xs), 2)]
        return xs[0]

    def first_slab(score, m):
        firsts = []
        for g0 in range(0, len(score), SUBLANES):
            slab = jnp.full((SUBLANES, LANES), no_slab, F32)
            for r in reversed(range(g0, min(g0 + SUBLANES, len(score)))):
                slab = jnp.where(score[r] == m, float(r), slab)
            firsts.append(slab)
        return tree(jnp.minimum, firsts)

    def select_blocks(n_act, group):
        for c0 in range(0, n_cols, group):
            cols = range(c0, min(c0 + group, n_cols))
            lanes = {c: slice(c * LANES, (c + 1) * LANES) for c in cols}
            cur = {c: (t0 + c * LANES + lax.broadcasted_iota(jnp.int32, (1, LANES), 1)) >> SEL_SHIFT
                   for c in cols}
            score = {}
            for c in cols:
                score[c] = []
                for r in range(n_act):
                    j = sub + r * SUBLANES
                    free = (j <= cur[c] - 2) & (j > 0)
                    score[c].append(jnp.where(
                        free, pslc_ref[r * SUBLANES:(r + 1) * SUBLANES, lanes[c]], -jnp.inf))
            for _ in range(max(k_top - N_FORCED, 0)):
                for c in cols:
                    m = jnp.max(tree(jnp.maximum, score[c]), axis=0, keepdims=True)
                    slab = first_slab(score[c], m)
                    block = slab * SUBLANES + sub_f
                    first = jnp.min(block, axis=0, keepdims=True)
                    taken = jnp.where(block == first, slab, -1.0)
                    score[c] = [jnp.where(taken == float(r), -jnp.inf, score[c][r]) for r in range(n_act)]
            for c in cols:
                for r in range(n_act):
                    pslc_ref[r * SUBLANES:(r + 1) * SUBLANES, lanes[c]] = jnp.where(
                        (sub + r * SUBLANES <= cur[c]) & (score[c][r] == -jnp.inf), 0.0, NEG_INF)
        if n_act < n_slabs:
            pslc_ref[n_act * SUBLANES:, :] = jnp.full(((n_slabs - n_act) * SUBLANES, tq), NEG_INF, F32)

    if n_slabs % 2 == 0:
        half = n_slabs // 2
        few = t0 + tq <= half * SUBLANES * SEL_BLOCK
        pl.when(few)(lambda: select_blocks(half, 2))
        pl.when(jnp.logical_not(few))(lambda: select_blocks(n_slabs, 1))
    else:
        select_blocks(n_slabs, 1)
    bias_ref[0, 0] = pslc_ref[...].T.astype(BF16)


def _cmp_call(q_raw, kaug, vaug, gates, aggt, tq, cw):
    b, s, _ = q_raw.shape
    nch = kaug.shape[2]
    nsel = aggt.shape[1]
    assert nch % cw == 0 and tq // CMP_STRIDE < cw
    k_top = min(N_SELECT, nsel)
    gw = HPG * HEAD_DIM
    m_rows = HPG * tq
    qspec = pl.BlockSpec((1, tq, gw), lambda bi, g, i: (bi, i, g))
    kvspec = pl.BlockSpec((1, 1, nch, 2 * HEAD_DIM), lambda bi, g, i: (bi, g, 0, 0))
    gspec = pl.BlockSpec((1, tq, LANES), lambda bi, g, i: (bi, i, g))
    return pl.pallas_call(
        functools.partial(_cmp_kernel, k_top, cw),
        grid=(b, N_GROUPS, s // tq),
        in_specs=[qspec, kvspec, kvspec, gspec, _const_spec(aggt.shape)],
        out_specs=[qspec, pl.BlockSpec((1, 1, tq, nsel), lambda bi, g, i: (bi, g, i, 0))],
        out_shape=[jax.ShapeDtypeStruct((b, s, D_ATTN), F32),
                   jax.ShapeDtypeStruct((b, N_GROUPS, s, nsel), BF16)],
        scratch_shapes=[pltpu.VMEM((m_rows, 2 * HEAD_DIM), BF16),
                        pltpu.VMEM((nch // cw, m_rows, cw), F32),
                        pltpu.VMEM((m_rows, 2 * HEAD_DIM), F32),
                        pltpu.VMEM((8, LANES), F32),
                        pltpu.VMEM((nsel, tq), F32)],
        compiler_params=pltpu.CompilerParams(dimension_semantics=("arbitrary",) * 3,
                                             vmem_limit_bytes=VMEM_LIMIT),
        name="cmp_attn_topk",
    )(q_raw, kaug, vaug, gates, aggt)


def _slc_kernel(nbh, tk, q_ref, bias_ref, kaug_ref, vaug_ref, gates_ref, o_ref,
                qaug_ref, acc_ref, kmax_ref):
    qt = pl.program_id(2)
    tq = q_ref.shape[1]
    m_rows = HPG * tq
    t0 = qt * tq
    n_wide = t0 // tk
    n_narrow = (t0 - n_wide * tk) // tq
    n_half = qaug_ref.shape[0]
    kcol = slice(nbh, nbh + HEAD_DIM)
    tcol = slice(nbh + HEAD_DIM, nbh + 2 * HEAD_DIM)

    @pl.when(qt == 0)
    def _():
        def body(c, mx):
            k = kaug_ref[0, 0, pl.ds(pl.multiple_of(c * tk, tk), tk), kcol].astype(F32)
            return jnp.maximum(mx, jnp.sum(k * k, axis=-1, keepdims=True))
        mx = lax.fori_loop(0, kaug_ref.shape[2] // tk, body, jnp.zeros((tk, 1), F32))
        kmax_ref[...] = jnp.broadcast_to(jnp.sqrt(jnp.max(mx, axis=0, keepdims=True)), kmax_ref.shape)

    qs = q_ref[0] * SCALE
    bias = bias_ref[0, 0]
    for h in range(HPG):
        rows = slice(h * tq, (h + 1) * tq)
        for hf in range(n_half):
            qaug_ref[hf, rows, 0:nbh] = bias[:, hf * nbh:(hf + 1) * nbh]
            qaug_ref[hf, rows, kcol] = qs[:, h * HEAD_DIM:(h + 1) * HEAD_DIM]
    bound = _tile_score_bound(qs, kmax_ref[0:1, 0:1])
    safe = jnp.max(bound) <= SAFE_SCORE_BOUND

    def scores(start, width, diagonal):
        ka = kaug_ref[0, 0, pl.ds(start, width), :]
        qa = qaug_ref[start // (nbh * SEL_BLOCK)]
        s = lax.dot_general(qa, ka, (((1,), (1,)), ((), ())), preferred_element_type=F32)
        if diagonal:
            r = lax.broadcasted_iota(jnp.int32, (m_rows, width), 0) & (tq - 1)
            c = lax.broadcasted_iota(jnp.int32, (m_rows, width), 1)
            s = jnp.where(c <= r, s, NEG_INF)
        return s

    def sweep(fn, carry):
        carry = lax.fori_loop(
            0, n_wide, lambda i, c: fn(pl.multiple_of(i * tk, tk), tk, False, c), carry)
        carry = lax.fori_loop(
            0, n_narrow, lambda i, c: fn(pl.multiple_of(n_wide * tk + i * tq, tq), tq, False, c), carry)
        return fn(pl.multiple_of(t0, tq), tq, True, carry)

    @pl.when(safe)
    def _():
        tail = jnp.broadcast_to(_shift_cols(bound), (m_rows, HEAD_DIM))
        for hf in range(n_half):
            qaug_ref[hf, :, tcol] = tail

    @pl.when(jnp.logical_not(safe))
    def _():
        for hf in range(n_half):
            qaug_ref[hf, :, tcol] = jnp.zeros((m_rows, HEAD_DIM), BF16)
        mx = sweep(lambda st, w, dg, m: jnp.maximum(m, jnp.max(scores(st, w, dg), axis=-1, keepdims=True)),
                   jnp.full((m_rows, 1), NEG_INF, F32))
        tail = _shift_cols(mx)
        for hf in range(n_half):
            qaug_ref[hf, :, tcol] = tail

    acc_ref[...] = jnp.zeros(acc_ref.shape, F32)

    def tile(start, width, diagonal, carry):
        p = jnp.exp(scores(start, width, diagonal)).astype(BF16)
        acc_ref[...] += jnp.dot(p, vaug_ref[0, 0, pl.ds(start, width), :], preferred_element_type=F32)
        return carry

    sweep(tile, 0)
    acc = acc_ref[...]
    o = acc[:, 0:HEAD_DIM] / acc[:, HEAD_DIM:HEAD_DIM + 1]
    gates = gates_ref[0]
    o_ref[0] = jnp.concatenate(
        [o[h * tq:(h + 1) * tq] * gates[:, HPG + h:HPG + h + 1] for h in range(HPG)], axis=-1)


def _slc_call(q_rot, bias, kaug, v, gates, tq, tk, nbh):
    b, s, _ = q_rot.shape
    assert tk % tq == 0 and (nbh * SEL_BLOCK) % tk == 0 and s % tk == 0
    nsel = bias.shape[3]
    gw = HPG * HEAD_DIM
    aug = nbh + 2 * HEAD_DIM
    qspec = pl.BlockSpec((1, tq, gw), lambda bi, g, i: (bi, i, g))
    return pl.pallas_call(
        functools.partial(_slc_kernel, nbh, tk),
        grid=(b, N_GROUPS, s // tq),
        in_specs=[qspec,
                  pl.BlockSpec((1, 1, tq, nsel), lambda bi, g, i: (bi, g, i, 0)),
                  pl.BlockSpec((1, 1, s, aug), lambda bi, g, i: (bi, g, 0, 0)),
                  pl.BlockSpec((1, 1, s, 2 * HEAD_DIM), lambda bi, g, i: (bi, g, 0, 0)),
                  pl.BlockSpec((1, tq, LANES), lambda bi, g, i: (bi, i, g))],
        out_specs=qspec,
        out_shape=jax.ShapeDtypeStruct((b, s, D_ATTN), F32),
        scratch_shapes=[pltpu.VMEM((nsel // nbh, HPG * tq, aug), BF16),
                        pltpu.VMEM((HPG * tq, 2 * HEAD_DIM), F32),
                        pltpu.VMEM((8, LANES), F32)],
        compiler_params=pltpu.CompilerParams(dimension_semantics=("arbitrary",) * 3,
                                             vmem_limit_bytes=VMEM_LIMIT),
        name="slc_attn",
    )(q_rot, bias, kaug, v, gates)


def _win_kernel(tq, q_ref, kaug_ref, vaug_ref, gates_ref, o_ref, qaug_ref, kmax_ref):
    qb = pl.program_id(2)
    tb = q_ref.shape[1]
    n_sub = tb // tq
    span = WINDOW + tq
    m_rows = HPG * tq
    tcol = slice(HEAD_DIM, 2 * HEAD_DIM)

    @pl.when(qb == 0)
    def _():
        def body(c, mx):
            k = kaug_ref[0, 0, pl.ds(pl.multiple_of(c * tq, tq), tq), 0:HEAD_DIM].astype(F32)
            return jnp.maximum(mx, jnp.sum(k * k, axis=-1, keepdims=True))
        mx = lax.fori_loop(0, kaug_ref.shape[2] // tq, body, jnp.zeros((tq, 1), F32))
        kmax_ref[...] = jnp.broadcast_to(jnp.sqrt(jnp.max(mx, axis=0, keepdims=True)), kmax_ref.shape)

    qs = q_ref[0] * SCALE
    for j in range(n_sub):
        for h in range(HPG):
            qaug_ref[j, h * tq:(h + 1) * tq, 0:HEAD_DIM] = qs[j * tq:(j + 1) * tq, h * HEAD_DIM:(h + 1) * HEAD_DIM]
    bound = _tile_score_bound(qs, kmax_ref[0:1, 0:1])
    safe = jnp.max(bound) <= SAFE_SCORE_BOUND

    def band_start(j):
        return pl.multiple_of(jnp.maximum(qb * tb + j * tq - WINDOW, 0), tq)

    def scores(j):
        t0 = qb * tb + j * tq
        start = band_start(j)
        ka = kaug_ref[0, 0, pl.ds(start, span), :]
        s = lax.dot_general(qaug_ref[j], ka, (((1,), (1,)), ((), ())), preferred_element_type=F32)
        d = (lax.broadcasted_iota(jnp.int32, (m_rows, tq), 1)
             - (lax.broadcasted_iota(jnp.int32, (m_rows, tq), 0) & (tq - 1)))
        blocks = []
        for blk in range(span // tq):
            off = t0 - start - blk * tq
            ok = lax.bitcast_convert_type(off - d, jnp.uint32) < jnp.uint32(WINDOW)
            blocks.append(jnp.where(ok, s[:, blk * tq:(blk + 1) * tq], NEG_INF))
        return jnp.concatenate(blocks, axis=-1)

    @pl.when(safe)
    def _():
        for j in range(n_sub):
            qaug_ref[j, :, tcol] = jnp.broadcast_to(_shift_cols(bound), (m_rows, HEAD_DIM))

    @pl.when(jnp.logical_not(safe))
    def _():
        for j in range(n_sub):
            qaug_ref[j, :, tcol] = jnp.zeros((m_rows, HEAD_DIM), BF16)
            qaug_ref[j, :, tcol] = _shift_cols(jnp.max(scores(j), axis=-1, keepdims=True))

    gates = gates_ref[0]
    for j in range(n_sub):
        p = jnp.exp(scores(j)).astype(BF16)
        acc = jnp.dot(p, vaug_ref[0, 0, pl.ds(band_start(j), span), :], preferred_element_type=F32)
        o = acc[:, 0:HEAD_DIM] / acc[:, HEAD_DIM:HEAD_DIM + 1]
        gj = gates[j * tq:(j + 1) * tq]
        o_ref[0, j * tq:(j + 1) * tq, :] = jnp.concatenate(
            [o[h * tq:(h + 1) * tq] * gj[:, 2 * HPG + h:2 * HPG + h + 1] for h in range(HPG)], axis=-1)


def _win_call(q_rot, k, v, gates, tb, tq):
    b, s, _ = q_rot.shape
    assert tb % tq == 0 and WINDOW % tq == 0
    gw = HPG * HEAD_DIM
    qspec = pl.BlockSpec((1, tb, gw), lambda bi, g, i: (bi, i, g))
    kvspec = pl.BlockSpec((1, 1, s, 2 * HEAD_DIM), lambda bi, g, i: (bi, g, 0, 0))
    return pl.pallas_call(
        functools.partial(_win_kernel, tq),
        grid=(b, N_GROUPS, s // tb),
        in_specs=[qspec, kvspec, kvspec, pl.BlockSpec((1, tb, LANES), lambda bi, g, i: (bi, i, g))],
        out_specs=qspec,
        out_shape=jax.ShapeDtypeStruct((b, s, D_ATTN), F32),
        scratch_shapes=[pltpu.VMEM((tb // tq, HPG * tq, 2 * HEAD_DIM), BF16),
                        pltpu.VMEM((8, LANES), F32)],
        compiler_params=pltpu.CompilerParams(dimension_semantics=("arbitrary",) * 3,
                                             vmem_limit_bytes=VMEM_LIMIT),
        name="win_attn",
    )(q_rot, k, v, gates)


def _out_kernel(x_ref, oc_ref, os_ref, ow_ref, convn_ref, ga_ref, wout_ref, gpost_ref,
                gpre2_ref, wg_ref, wu_ref, wd_ref, gpost2_ref, o_ref):
    attn = oc_ref[...] + os_ref[...] + ow_ref[...]
    an = _rms(attn, ga_ref[...]).astype(BF16)
    h = (jnp.dot(an, wout_ref[0:D_ATTN, :], preferred_element_type=F32)
         + jnp.dot(convn_ref[...], wout_ref[D_ATTN:D_MODEL, :], preferred_element_type=F32))
    x1 = x_ref[...] + _rms(h, gpost_ref[...])
    o_ref[...] = _ffn(x1, gpre2_ref[...], wg_ref, wu_ref, wd_ref, gpost2_ref[...])


def _out_call(x2d, oc, osl, ow, convn, ga, wout, gpost, gpre2, wg, wu, wd, gpost2, tm):
    t = x2d.shape[0]
    row = lambda width: pl.BlockSpec((tm, width), lambda i: (i, 0))
    return pl.pallas_call(
        _out_kernel,
        grid=(t // tm,),
        in_specs=[row(D_MODEL), row(D_ATTN), row(D_ATTN), row(D_ATTN), row(D_CONV),
                  _const_spec((1, D_ATTN)), _const_spec(wout.shape), _const_spec((1, D_MODEL)),
                  _const_spec((1, D_MODEL)), _const_spec(wg.shape), _const_spec(wu.shape),
                  _const_spec(wd.shape), _const_spec((1, D_MODEL))],
        out_specs=row(D_MODEL),
        out_shape=jax.ShapeDtypeStruct(x2d.shape, F32),
        compiler_params=pltpu.CompilerParams(dimension_semantics=("arbitrary",),
                                             vmem_limit_bytes=VMEM_LIMIT),
        name="outproj_ffn2",
    )(x2d, oc, osl, ow, convn, ga, wout, gpost, gpre2, wg, wu, wd, gpost2)


def _prep_w_in(w_in):
    sizes = [D_ATTN] + [D_KV] * 6 + [N_GATES] + [D_CONV] * 3
    cuts = np.cumsum([0] + sizes)
    q, kc, vc, ks, vs, kw, vw, gt, bg, cg, xc = [w_in[:, cuts[i]:cuts[i + 1]] for i in range(len(sizes))]
    gt = gt.reshape(D_MODEL, N_GROUPS, HPG, 3).transpose(0, 1, 3, 2).reshape(D_MODEL, N_GROUPS, 3 * HPG)
    gt = jnp.pad(gt, ((0, 0), (0, 0), (0, LANES - 3 * HPG))).reshape(D_MODEL, GATE_COLS)
    return jnp.concatenate([q, kc, vc, ks, vs, kw, vw, gt, bg, cg, xc], axis=1).astype(BF16)


def _prep_compress(pe, w1, w2):
    half = CMP_BLOCK // 2
    eye = jnp.eye(N_GROUPS, dtype=F32)
    w1r = w1.reshape(2, half, HEAD_DIM, CMP_HIDDEN)
    w1big = jnp.einsum("ptdc,gh->ptgdhc", w1r, eye).reshape(2, half * D_KV, N_GROUPS * CMP_HIDDEN)
    w2big = jnp.einsum("cd,gh->gchd", w2, eye).reshape(N_GROUPS * CMP_HIDDEN, D_KV)
    pe_rows = jnp.broadcast_to(pe.reshape(2, half, 1, HEAD_DIM), (2, half, N_GROUPS, HEAD_DIM))
    return pe_rows.reshape(2, half * D_KV), w1big.astype(BF16), w2big.astype(BF16)


def _agg_matrix(nch, nsel, cw):
    agg_w = np.convolve(np.ones(SEL_RATIO), np.ones(CMP_BLOCK // CMP_STRIDE))
    a = np.zeros((nch, nsel), np.float32)
    for j in range(nsel):
        for o, wgt in enumerate(agg_w):
            c = SEL_RATIO * j + o - (CMP_BLOCK // CMP_STRIDE - 1)
            if 0 <= c < nch - 1:
                a[c, j] = wgt
    a = a.T.reshape(nsel, nch // cw, cw).transpose(1, 0, 2)
    return jnp.asarray(a, BF16)


def _rope_inv_freq_row():
    inv = ROPE_THETA ** (-np.arange(ROPE_HALF, dtype=np.float32) * 2.0 / ROPE_DIM)
    lane = np.arange(LANES) % HEAD_DIM
    row = np.where(lane < ROPE_DIM, inv[lane % ROPE_HALF], 0.0).astype(np.float32)
    return jnp.asarray(row.reshape(1, LANES))


def _forward(x, positions, p, *, tm, tq, tc, nbh, tqs, tks, tqc, cw, tbw):
    b, s, _ = x.shape
    depth = p["w_in"].shape[0]
    nch = s // CMP_STRIDE
    nsel = s // SEL_BLOCK
    cos, sin = _rope_call(positions.reshape(b, s, 1), _rope_inv_freq_row(), tm)
    agg = _agg_matrix(nch, nsel, cw)
    row = lambda v: v.reshape(1, -1)
    for l in range(depth):
        x2d = x.reshape(b * s, D_MODEL)
        x2d = _ffn_call(x2d, row(p["ffn1_norm_pre"][l]), p["ffn1_w_gate"][l].astype(BF16),
                        p["ffn1_w_up"][l].astype(BF16), p["ffn1_w_down"][l].astype(BF16),
                        row(p["ffn1_norm_post"][l]), tm)
        (q_raw, q_rot, kc_in, vc_in, ksaug, vs, kw, vw, gates, convn) = _inproj_call(
            x2d.reshape(b, s, D_MODEL), cos, sin, row(p["mix_norm_pre"][l]), _prep_w_in(p["w_in"][l]),
            p["conv_w"][l], row(p["conv_out_norm"][l]), tm, nbh)
        pek, w1k, w2k = _prep_compress(p["cmp_pe_k"][l], p["cmp_w1_k"][l], p["cmp_w2_k"][l])
        pev, w1v, w2v = _prep_compress(p["cmp_pe_v"][l], p["cmp_w1_v"][l], p["cmp_w2_v"][l])
        kcmp, vcmp = _compress_call(kc_in.reshape(b, nch, CMP_STRIDE * D_KV),
                                    vc_in.reshape(b, nch, CMP_STRIDE * D_KV),
                                    pek, w1k, w2k, pev, w1v, w2v, tc)
        o_cmp, bias = _cmp_call(q_raw, kcmp, vcmp, gates, agg, tqc, cw)
        o_slc = _slc_call(q_rot, bias, ksaug, vs, gates, tqs, tks, nbh)
        o_win = _win_call(q_rot, kw, vw, gates, tbw, tq)
        flat = lambda a: a.reshape(b * s, a.shape[-1])
        x2d = _out_call(x2d, flat(o_cmp), flat(o_slc), flat(o_win), flat(convn),
                        row(p["attn_out_norm"][l]), p["w_out"][l].astype(BF16), row(p["mix_norm_post"][l]),
                        row(p["ffn2_norm_pre"][l]), p["ffn2_w_gate"][l].astype(BF16),
                        p["ffn2_w_up"][l].astype(BF16), p["ffn2_w_down"][l].astype(BF16),
                        row(p["ffn2_norm_post"][l]), tm)
        x = x2d.reshape(b, s, D_MODEL)
    return x


def kernel(x, positions, ffn1_norm_pre, ffn1_w_gate, ffn1_w_up, ffn1_w_down, ffn1_norm_post, mix_norm_pre, w_in, cmp_pe_k, cmp_w1_k, cmp_w2_k, cmp_pe_v, cmp_w1_v, cmp_w2_v, conv_w, attn_out_norm, conv_out_norm, w_out, mix_norm_post, ffn2_norm_pre, ffn2_w_gate, ffn2_w_up, ffn2_w_down, ffn2_norm_post):
    params = dict(
        ffn1_norm_pre=ffn1_norm_pre, ffn1_w_gate=ffn1_w_gate, ffn1_w_up=ffn1_w_up, ffn1_w_down=ffn1_w_down,
        ffn1_norm_post=ffn1_norm_post, mix_norm_pre=mix_norm_pre, w_in=w_in,
        cmp_pe_k=cmp_pe_k, cmp_w1_k=cmp_w1_k, cmp_w2_k=cmp_w2_k,
        cmp_pe_v=cmp_pe_v, cmp_w1_v=cmp_w1_v, cmp_w2_v=cmp_w2_v,
        conv_w=conv_w, attn_out_norm=attn_out_norm, conv_out_norm=conv_out_norm, w_out=w_out,
        mix_norm_post=mix_norm_post, ffn2_norm_pre=ffn2_norm_pre, ffn2_w_gate=ffn2_w_gate,
        ffn2_w_up=ffn2_w_up, ffn2_w_down=ffn2_w_down, ffn2_norm_post=ffn2_norm_post)
    return _forward(x, positions, params, tm=512, tq=256, tc=256, nbh=128, tqs=512, tks=1024, tqc=1024, cw=256, tbw=1024)
```

```python
import functools
import math

import numpy as np
import jax
import jax.numpy as jnp
from jax import lax
from jax.experimental import pallas as pl
from jax.experimental.pallas import tpu as pltpu

D_MODEL = 1024
N_HEADS = 8
HEAD_DIM = 64
N_GROUPS = 2
HPG = N_HEADS // N_GROUPS
D_ATTN = N_HEADS * HEAD_DIM
D_KV = N_GROUPS * HEAD_DIM
D_CONV = D_MODEL - D_ATTN
CONV_WIDTH = 3
CMP_BLOCK = 32
CMP_STRIDE = 16
CMP_SHIFT = 4
CMP_HIDDEN = 256
SEL_BLOCK = 64
SEL_SHIFT = 6
SEL_RATIO = SEL_BLOCK // CMP_STRIDE
N_SELECT = 16
N_FORCED = 3
WINDOW = 512
ROPE_THETA = 500000.0
ROPE_DIM = HEAD_DIM // 4
ROPE_HALF = ROPE_DIM // 2
D_FF = 2816
N_GATES = 3 * N_HEADS
EPS = 1e-6
NEG_INF = -1e30
FORCE_SCORE = 1e9
SCALE = 1.0 / math.sqrt(HEAD_DIM)

LANES = 128
SUBLANES = 8
GATE_COLS = N_GROUPS * LANES
VMEM_LIMIT = 56 * 1024 * 1024
FF_CHUNKS = ((0, 768), (768, 1536), (1536, 2304), (2304, 2816))
N_SHIFT_COLS = 2
SAFE_SCORE_BOUND = 0.0

F32 = jnp.float32
BF16 = jnp.bfloat16


def _const_spec(shape):
    nd = len(shape)
    return pl.BlockSpec(shape, lambda *_: (0,) * nd, pipeline_mode=pl.Buffered(1))


def _rms(x, g):
    ms = jnp.mean(x * x, axis=-1, keepdims=True)
    return x * lax.rsqrt(ms + EPS) * g


def _ffn(x, g_pre, wg_ref, wu_ref, wd_ref, g_post):
    h = _rms(x, g_pre).astype(BF16)
    d = None
    for c0, c1 in FF_CHUNKS:
        gate = jnp.dot(h, wg_ref[:, c0:c1], preferred_element_type=F32)
        up = jnp.dot(h, wu_ref[:, c0:c1], preferred_element_type=F32)
        a = (gate * jax.nn.sigmoid(gate) * up).astype(BF16)
        part = jnp.dot(a, wd_ref[c0:c1, :], preferred_element_type=F32)
        d = part if d is None else d + part
    return x + 0.5 * _rms(d, g_post)


def _ffn_kernel(x_ref, gpre_ref, wg_ref, wu_ref, wd_ref, gpost_ref, o_ref):
    o_ref[...] = _ffn(x_ref[...], gpre_ref[...], wg_ref, wu_ref, wd_ref, gpost_ref[...])


def _ffn_call(x2d, g_pre, wg, wu, wd, g_post, tm):
    t = x2d.shape[0]
    row = pl.BlockSpec((tm, D_MODEL), lambda i: (i, 0))
    return pl.pallas_call(
        _ffn_kernel,
        grid=(t // tm,),
        in_specs=[row, _const_spec((1, D_MODEL)), _const_spec(wg.shape), _const_spec(wu.shape),
                  _const_spec(wd.shape), _const_spec((1, D_MODEL))],
        out_specs=row,
        out_shape=jax.ShapeDtypeStruct(x2d.shape, F32),
        compiler_params=pltpu.CompilerParams(dimension_semantics=("arbitrary",),
                                             vmem_limit_bytes=VMEM_LIMIT),
        name="ffn1",
    )(x2d, g_pre, wg, wu, wd, g_post)


_C_Q = 0
_C_KC = _C_Q + D_ATTN
_C_VC = _C_KC + D_KV
_C_KS = _C_VC + D_KV
_C_VS = _C_KS + D_KV
_C_KW = _C_VS + D_KV
_C_VW = _C_KW + D_KV
_C_GATE = _C_VW + D_KV
_C_BG = _C_GATE + GATE_COLS
_C_CG = _C_BG + D_CONV
_C_XC = _C_CG + D_CONV
_C_END = _C_XC + D_CONV


def _rope_kernel(pos_ref, invf_ref, cos_ref, sin_ref):
    ang = pos_ref[0].astype(F32) * invf_ref[...]
    cos_ref[0] = jnp.cos(ang)
    sin_ref[0] = jnp.sin(ang)


def _rope_call(pos3, invf, tm):
    b, s, _ = pos3.shape
    row = lambda width: pl.BlockSpec((1, tm, width), lambda bi, i: (bi, i, 0))
    shape = jax.ShapeDtypeStruct((b, s, LANES), F32)
    return pl.pallas_call(
        _rope_kernel,
        grid=(b, s // tm),
        in_specs=[row(1), _const_spec((1, LANES))],
        out_specs=[row(LANES), row(LANES)],
        out_shape=[shape, shape],
        compiler_params=pltpu.CompilerParams(dimension_semantics=("arbitrary", "arbitrary"),
                                             vmem_limit_bytes=VMEM_LIMIT),
        name="rope_tables",
    )(pos3, invf)


def _inproj_kernel(nbh, x_ref, cos_ref, sin_ref, g_ref, w_ref, convw_ref, convg_ref,
                   qraw_ref, qrot_ref, kc_ref, vc_ref, ksaug_ref, vs_ref, kw_ref, vw_ref,
                   gates_ref, convn_ref, ubuf_ref):
    i = pl.program_id(1)
    tm = x_ref.shape[1]
    h = _rms(x_ref[0], g_ref[...]).astype(BF16)

    def proj(c0, c1):
        return jnp.dot(h, w_ref[:, c0:c1], preferred_element_type=F32)

    cos = cos_ref[0]
    sin = sin_ref[0]
    lane = lax.broadcasted_iota(jnp.int32, (1, LANES), 1) & (HEAD_DIM - 1)
    s_lo = jnp.where(lane < ROPE_HALF, -sin, 0.0)
    s_hi = jnp.where(lane >= ROPE_HALF, sin, 0.0)

    def rope(z):
        return (z * cos + pltpu.roll(z, LANES - ROPE_HALF, 1) * s_lo
                + pltpu.roll(z, ROPE_HALF, 1) * s_hi)

    for c in range(0, D_ATTN // LANES, 2):
        zq2 = proj(_C_Q + c * LANES, _C_Q + (c + 2) * LANES)
        for cc in (c, c + 1):
            zq = zq2[:, (cc - c) * LANES:(cc - c + 1) * LANES]
            qraw_ref[0, :, cc * LANES:(cc + 1) * LANES] = zq.astype(BF16)
            qrot_ref[0, :, cc * LANES:(cc + 1) * LANES] = rope(zq).astype(BF16)

    kvc = proj(_C_KC, _C_KS)
    kc_ref[0] = kvc[:, 0:D_KV]
    vc_ref[0] = kvc[:, D_KV:2 * D_KV]
    kvs = proj(_C_KS, _C_KW)
    ks = rope(kvs[:, 0:D_KV]).astype(BF16)
    vs = kvs[:, D_KV:2 * D_KV].astype(BF16)
    kvw = proj(_C_KW, _C_GATE)
    kw = rope(kvw[:, 0:D_KV]).astype(BF16)
    vw = kvw[:, D_KV:2 * D_KV].astype(BF16)
    row_blk = ((i * tm + lax.broadcasted_iota(jnp.int32, (tm, nbh), 0)) >> SEL_SHIFT) & (nbh - 1)
    onehot = jnp.where(row_blk == lax.broadcasted_iota(jnp.int32, (tm, nbh), 1), 1.0, 0.0).astype(BF16)
    tail = jnp.where(lax.broadcasted_iota(jnp.int32, (tm, HEAD_DIM), 1) < N_SHIFT_COLS, 1.0, 0.0).astype(BF16)
    for g in range(N_GROUPS):
        sl = slice(g * HEAD_DIM, (g + 1) * HEAD_DIM)
        ksaug_ref[0, g, :, 0:nbh] = onehot
        ksaug_ref[0, g, :, nbh:nbh + HEAD_DIM] = ks[:, sl]
        ksaug_ref[0, g, :, nbh + HEAD_DIM:nbh + 2 * HEAD_DIM] = tail
        vs_ref[0, g, :, 0:HEAD_DIM] = vs[:, sl]
        vs_ref[0, g, :, HEAD_DIM:2 * HEAD_DIM] = tail
        kw_ref[0, g, :, 0:HEAD_DIM] = kw[:, sl]
        kw_ref[0, g, :, HEAD_DIM:2 * HEAD_DIM] = tail
        vw_ref[0, g, :, 0:HEAD_DIM] = vw[:, sl]
        vw_ref[0, g, :, HEAD_DIM:2 * HEAD_DIM] = tail

    gates_ref[0] = jax.nn.sigmoid(proj(_C_GATE, _C_BG))

    u = proj(_C_CG, _C_XC) * proj(_C_XC, _C_END)

    @pl.when(i == 0)
    def _():
        ubuf_ref[0:8, :] = jnp.zeros((8, D_CONV), F32)

    @pl.when(i > 0)
    def _():
        ubuf_ref[0:8, :] = ubuf_ref[tm:tm + 8, :]

    ubuf_ref[8:tm + 8, :] = u
    w = convw_ref[...]
    y = (w[2:3, :] * u + w[1:2, :] * ubuf_ref[7:tm + 7, :] + w[0:1, :] * ubuf_ref[6:tm + 6, :])
    conv = proj(_C_BG, _C_CG) * y
    convn_ref[0] = _rms(conv, convg_ref[...]).astype(BF16)


def _inproj_call(x, cos, sin, g_pre, w_in, conv_w, conv_g, tm, nbh):
    b, s, _ = x.shape
    grid = (b, s // tm)
    row = lambda width: pl.BlockSpec((1, tm, width), lambda bi, i: (bi, i, 0))
    grp = lambda width: pl.BlockSpec((1, N_GROUPS, tm, width), lambda bi, i: (bi, 0, i, 0))
    kern = functools.partial(_inproj_kernel, nbh)
    return pl.pallas_call(
        kern,
        grid=grid,
        in_specs=[row(D_MODEL), row(LANES), row(LANES), _const_spec((1, D_MODEL)), _const_spec(w_in.shape),
                  _const_spec((CONV_WIDTH, D_CONV)), _const_spec((1, D_CONV))],
        out_specs=[row(D_ATTN), row(D_ATTN), row(D_KV), row(D_KV), grp(nbh + 2 * HEAD_DIM), grp(2 * HEAD_DIM),
                   grp(2 * HEAD_DIM), grp(2 * HEAD_DIM), row(GATE_COLS), row(D_CONV)],
        out_shape=[
            jax.ShapeDtypeStruct((b, s, D_ATTN), BF16),
            jax.ShapeDtypeStruct((b, s, D_ATTN), BF16),
            jax.ShapeDtypeStruct((b, s, D_KV), F32),
            jax.ShapeDtypeStruct((b, s, D_KV), F32),
            jax.ShapeDtypeStruct((b, N_GROUPS, s, nbh + 2 * HEAD_DIM), BF16),
            jax.ShapeDtypeStruct((b, N_GROUPS, s, 2 * HEAD_DIM), BF16),
            jax.ShapeDtypeStruct((b, N_GROUPS, s, 2 * HEAD_DIM), BF16),
            jax.ShapeDtypeStruct((b, N_GROUPS, s, 2 * HEAD_DIM), BF16),
            jax.ShapeDtypeStruct((b, s, GATE_COLS), F32),
            jax.ShapeDtypeStruct((b, s, D_CONV), BF16),
        ],
        scratch_shapes=[pltpu.VMEM((tm + 8, D_CONV), F32)],
        compiler_params=pltpu.CompilerParams(dimension_semantics=("arbitrary", "arbitrary"),
                                             vmem_limit_bytes=VMEM_LIMIT),
        name="inproj",
    )(x, cos, sin, g_pre, w_in, conv_w, conv_g)


def _compress_kernel(kc_ref, kcn_ref, vc_ref, vcn_ref, pek_ref, w1k_ref, w2k_ref,
                     pev_ref, w1v_ref, w2v_ref, ko_ref, vo_ref):
    tc = kc_ref.shape[1]
    hid_w = N_GROUPS * CMP_HIDDEN
    last = lax.broadcasted_iota(jnp.int32, (tc, 1), 0) == tc - 1

    def one(x_ref, xn_ref, pe_ref, w1_ref, w2_ref, o_ref):
        x = x_ref[0]
        top = jnp.dot((x + pe_ref[0:1, :]).astype(BF16), w1_ref[0], preferred_element_type=F32)
        xb = (x + pe_ref[1:2, :]).astype(BF16)
        bot = jnp.dot(xb, w1_ref[1], preferred_element_type=F32)
        xnb = (xn_ref[0] + pe_ref[1:2, :]).astype(BF16)
        botn = jnp.dot(xnb, w1_ref[1], preferred_element_type=F32)
        shifted = jnp.where(last, botn[0:1, :], pltpu.roll(bot, tc - 1, 0))
        hid = jax.nn.gelu(top + shifted).astype(BF16)
        out = jnp.dot(hid, w2_ref[...], preferred_element_type=F32)
        tail = jnp.where(lax.broadcasted_iota(jnp.int32, (tc, HEAD_DIM), 1) < N_SHIFT_COLS, 1.0, 0.0).astype(BF16)
        for g in range(N_GROUPS):
            o_ref[0, g, :, 0:HEAD_DIM] = out[:, g * HEAD_DIM:(g + 1) * HEAD_DIM].astype(BF16)
            o_ref[0, g, :, HEAD_DIM:2 * HEAD_DIM] = tail

    one(kc_ref, kcn_ref, pek_ref, w1k_ref, w2k_ref, ko_ref)
    one(vc_ref, vcn_ref, pev_ref, w1v_ref, w2v_ref, vo_ref)


def _compress_call(kc_in, vc_in, pek, w1k, w2k, pev, w1v, w2v, tc):
    b, nch, width = kc_in.shape
    nt = nch // tc
    last8 = nch // 8 - 1
    cur = pl.BlockSpec((1, tc, width), lambda bi, i: (bi, i, 0))
    nxt = pl.BlockSpec((1, 8, width), lambda bi, i: (bi, jnp.minimum((i + 1) * (tc // 8), last8), 0))
    out = pl.BlockSpec((1, N_GROUPS, tc, 2 * HEAD_DIM), lambda bi, i: (bi, 0, i, 0))
    oshape = jax.ShapeDtypeStruct((b, N_GROUPS, nch, 2 * HEAD_DIM), BF16)
    return pl.pallas_call(
        _compress_kernel,
        grid=(b, nt),
        in_specs=[cur, nxt, cur, nxt, _const_spec(pek.shape), _const_spec(w1k.shape), _const_spec(w2k.shape),
                  _const_spec(pev.shape), _const_spec(w1v.shape), _const_spec(w2v.shape)],
        out_specs=[out, out],
        out_shape=[oshape, oshape],
        compiler_params=pltpu.CompilerParams(dimension_semantics=("arbitrary", "arbitrary"),
                                             vmem_limit_bytes=VMEM_LIMIT),
        name="compress",
    )(kc_in, kc_in, vc_in, vc_in, pek, w1k, w2k, pev, w1v, w2v)


def _split3(x):
    hi = x.astype(BF16)
    r = x - hi.astype(F32)
    mid = r.astype(BF16)
    lo = (r - mid.astype(F32)).astype(BF16)
    return hi, mid, lo


def _tile_score_bound(qs, kmax):
    qf = qs.astype(F32)
    sq = qf * qf
    nrm2 = None
    for h in range(HPG):
        rows = jnp.sum(sq[:, h * HEAD_DIM:(h + 1) * HEAD_DIM], axis=-1, keepdims=True)
        top = jnp.max(rows, axis=0, keepdims=True)
        nrm2 = top if nrm2 is None else jnp.maximum(nrm2, top)
    return jnp.sqrt(nrm2) * kmax


def _shift_cols(shift):
    hi = shift.astype(BF16).astype(F32)
    lo = (shift - hi).astype(BF16).astype(F32)
    lane = lax.broadcasted_iota(jnp.int32, (1, HEAD_DIM), 1)
    return jnp.where(lane == 0, -hi, jnp.where(lane == 1, -lo, 0.0)).astype(BF16)


def _cmp_kernel(k_top, cw, q_ref, kaug_ref, vaug_ref, gates_ref, aggt_ref, o_ref, bias_ref,
                qaug_ref, e_ref, acc_ref, kmax_ref, pslc_ref):
    qt = pl.program_id(2)
    tq = q_ref.shape[1]
    nch = kaug_ref.shape[2]
    nsel = bias_ref.shape[3]
    m_rows = HPG * tq
    t0 = qt * tq
    tcol = slice(HEAD_DIM, 2 * HEAD_DIM)
    c_last = ((t0 + tq - CMP_BLOCK) >> CMP_SHIFT) // cw
    c_mask = jnp.maximum(c_last - 1, 0)

    @pl.when(qt == 0)
    def _():
        k = kaug_ref[0, 0, :, 0:HEAD_DIM].astype(F32)
        n_ok = lax.broadcasted_iota(jnp.int32, (nch, 1), 0) < nch - 1
        ksq = jnp.where(n_ok, jnp.sum(k * k, axis=-1, keepdims=True), 0.0)
        kmax_ref[...] = jnp.broadcast_to(jnp.sqrt(jnp.max(ksq, axis=0, keepdims=True)), kmax_ref.shape)

    qs = q_ref[0] * SCALE
    for h in range(HPG):
        qaug_ref[h * tq:(h + 1) * tq, 0:HEAD_DIM] = qs[:, h * HEAD_DIM:(h + 1) * HEAD_DIM]
    bound = _tile_score_bound(qs, kmax_ref[0:1, 0:1])
    safe = jnp.max(bound) <= SAFE_SCORE_BOUND

    def scores(c, masked):
        ka = kaug_ref[0, 0, pl.ds(pl.multiple_of(c * cw, cw), cw), :]
        s = lax.dot_general(qaug_ref[...], ka, (((1,), (1,)), ((), ())), preferred_element_type=F32)
        if masked:
            row_t = t0 + (lax.broadcasted_iota(jnp.int32, (m_rows, 1), 0) & (tq - 1))
            n_vis = (row_t - (CMP_BLOCK - 1)) >> CMP_SHIFT
            n = c * cw + lax.broadcasted_iota(jnp.int32, (m_rows, cw), 1)
            s = jnp.where(n <= n_vis, s, NEG_INF)
        return s

    def sweep(fn_full, fn_masked, init):
        carry = lax.fori_loop(0, c_mask, fn_full, init)
        return lax.fori_loop(c_mask, c_last + 1, fn_masked, carry)

    @pl.when(safe)
    def _():
        qaug_ref[:, tcol] = jnp.broadcast_to(_shift_cols(bound), (m_rows, HEAD_DIM))

    @pl.when(jnp.logical_not(safe))
    def _():
        qaug_ref[:, tcol] = jnp.zeros((m_rows, HEAD_DIM), BF16)
        step = lambda masked: (lambda c, m: jnp.maximum(m, jnp.max(scores(c, masked), axis=-1, keepdims=True)))
        mx = sweep(step(False), step(True), jnp.full((m_rows, 1), NEG_INF, F32))
        qaug_ref[:, tcol] = _shift_cols(jnp.where(mx > 0.5 * NEG_INF, mx, 0.0))

    acc_ref[...] = jnp.zeros(acc_ref.shape, F32)

    def chunk(masked):
        def body(c, carry):
            e = jnp.exp(scores(c, masked))
            start = pl.multiple_of(c * cw, cw)
            e_ref[c] = e
            acc_ref[...] += jnp.dot(e.astype(BF16), vaug_ref[0, 0, pl.ds(start, cw), :],
                                    preferred_element_type=F32)
            return carry
        return body

    sweep(chunk(False), chunk(True), 0)
    acc = acc_ref[...]
    l = acc[:, HEAD_DIM:HEAD_DIM + 1]
    rinv = jnp.where(l > 0.0, 1.0 / l, 0.0)
    o = acc[:, 0:HEAD_DIM] * rinv
    gates = gates_ref[0]
    o_ref[0] = jnp.concatenate(
        [o[h * tq:(h + 1) * tq] * gates[:, h:h + 1] for h in range(HPG)], axis=-1)

    pslc_ref[...] = jnp.zeros(pslc_ref.shape, F32)

    def agg_body(c, carry):
        imp = sum(e_ref[c, h * tq:(h + 1) * tq, :] * rinv[h * tq:(h + 1) * tq] for h in range(HPG))
        at = aggt_ref[c]
        pslc_ref[...] += sum(lax.dot_general(at, part, (((1,), (1,)), ((), ())), preferred_element_type=F32)
                             for part in _split3(imp))
        return carry

    lax.fori_loop(0, c_last + 1, agg_body, 0)

    n_slabs = nsel // SUBLANES
    sub = lax.broadcasted_iota(jnp.int32, (SUBLANES, LANES), 0)
    sub_f = sub.astype(F32)
    no_slab = float(n_slabs)
    n_cols = tq // LANES

    def tree(fn, xs):
        while len(xs) > 1:
            xs = [fn(xs[i], xs[i + 1]) if i + 1 < len(xs) else xs[i] for i in range(0, len(xs), 2)]
        return xs[0]

    def first_slab(score, m):
        firsts = []
        for g0 in range(0, len(score), SUBLANES):
            slab = jnp.full((SUBLANES, LANES), no_slab, F32)
            for r in reversed(range(g0, min(g0 + SUBLANES, len(score)))):
                slab = jnp.where(score[r] == m, float(r), slab)
            firsts.append(slab)
        return tree(jnp.minimum, firsts)

    def select_blocks(n_act, group):
        for c0 in range(0, n_cols, group):
            cols = range(c0, min(c0 + group, n_cols))
            lanes = {c: slice(c * LANES, (c + 1) * LANES) for c in cols}
            cur = {c: (t0 + c * LANES + lax.broadcasted_iota(jnp.int32, (1, LANES), 1)) >> SEL_SHIFT
                   for c in cols}
            score = {}
            for c in cols:
                score[c] = []
                for r in range(n_act):
                    j = sub + r * SUBLANES
                    free = (j <= cur[c] - 2) & (j > 0)
                    score[c].append(jnp.where(
                        free, pslc_ref[r * SUBLANES:(r + 1) * SUBLANES, lanes[c]], -jnp.inf))
            for _ in range(max(k_top - N_FORCED, 0)):
                for c in cols:
                    m = jnp.max(tree(jnp.maximum, score[c]), axis=0, keepdims=True)
                    slab = first_slab(score[c], m)
                    block = slab * SUBLANES + sub_f
                    first = jnp.min(block, axis=0, keepdims=True)
                    taken = jnp.where(block == first, slab, -1.0)
                    score[c] = [jnp.where(taken == float(r), -jnp.inf, score[c][r]) for r in range(n_act)]
            for c in cols:
                for r in range(n_act):
                    pslc_ref[r * SUBLANES:(r + 1) * SUBLANES, lanes[c]] = jnp.where(
                        (sub + r * SUBLANES <= cur[c]) & (score[c][r] == -jnp.inf), 0.0, NEG_INF)
        if n_act < n_slabs:
            pslc_ref[n_act * SUBLANES:, :] = jnp.full(((n_slabs - n_act) * SUBLANES, tq), NEG_INF, F32)

    if n_slabs % 2 == 0:
        half = n_slabs // 2
        few = t0 + tq <= half * SUBLANES * SEL_BLOCK
        pl.when(few)(lambda: select_blocks(half, 2))
        pl.when(jnp.logical_not(few))(lambda: select_blocks(n_slabs, 1))
    else:
        select_blocks(n_slabs, 1)
    bias_ref[0, 0] = pslc_ref[...].T.astype(BF16)


def _cmp_call(q_raw, kaug, vaug, gates, aggt, tq, cw):
    b, s, _ = q_raw.shape
    nch = kaug.shape[2]
    nsel = aggt.shape[1]
    assert nch % cw == 0 and tq // CMP_STRIDE < cw
    k_top = min(N_SELECT, nsel)
    gw = HPG * HEAD_DIM
    m_rows = HPG * tq
    qspec = pl.BlockSpec((1, tq, gw), lambda bi, g, i: (bi, i, g))
    kvspec = pl.BlockSpec((1, 1, nch, 2 * HEAD_DIM), lambda bi, g, i: (bi, g, 0, 0))
    gspec = pl.BlockSpec((1, tq, LANES), lambda bi, g, i: (bi, i, g))
    return pl.pallas_call(
        functools.partial(_cmp_kernel, k_top, cw),
        grid=(b, N_GROUPS, s // tq),
        in_specs=[qspec, kvspec, kvspec, gspec, _const_spec(aggt.shape)],
        out_specs=[qspec, pl.BlockSpec((1, 1, tq, nsel), lambda bi, g, i: (bi, g, i, 0))],
        out_shape=[jax.ShapeDtypeStruct((b, s, D_ATTN), F32),
                   jax.ShapeDtypeStruct((b, N_GROUPS, s, nsel), BF16)],
        scratch_shapes=[pltpu.VMEM((m_rows, 2 * HEAD_DIM), BF16),
                        pltpu.VMEM((nch // cw, m_rows, cw), F32),
                        pltpu.VMEM((m_rows, 2 * HEAD_DIM), F32),
                        pltpu.VMEM((8, LANES), F32),
                        pltpu.VMEM((nsel, tq), F32)],
        compiler_params=pltpu.CompilerParams(dimension_semantics=("arbitrary",) * 3,
                                             vmem_limit_bytes=VMEM_LIMIT),
        name="cmp_attn_topk",
    )(q_raw, kaug, vaug, gates, aggt)


def _slc_kernel(nbh, tk, q_ref, bias_ref, kaug_ref, vaug_ref, gates_ref, o_ref,
                qaug_ref, acc_ref, kmax_ref):
    qt = pl.program_id(2)
    tq = q_ref.shape[1]
    m_rows = HPG * tq
    t0 = qt * tq
    n_wide = t0 // tk
    n_narrow = (t0 - n_wide * tk) // tq
    n_half = qaug_ref.shape[0]
    kcol = slice(nbh, nbh + HEAD_DIM)
    tcol = slice(nbh + HEAD_DIM, nbh + 2 * HEAD_DIM)

    @pl.when(qt == 0)
    def _():
        def body(c, mx):
            k = kaug_ref[0, 0, pl.ds(pl.multiple_of(c * tk, tk), tk), kcol].astype(F32)
            return jnp.maximum(mx, jnp.sum(k * k, axis=-1, keepdims=True))
        mx = lax.fori_loop(0, kaug_ref.shape[2] // tk, body, jnp.zeros((tk, 1), F32))
        kmax_ref[...] = jnp.broadcast_to(jnp.sqrt(jnp.max(mx, axis=0, keepdims=True)), kmax_ref.shape)

    qs = q_ref[0] * SCALE
    bias = bias_ref[0, 0]
    for h in range(HPG):
        rows = slice(h * tq, (h + 1) * tq)
        for hf in range(n_half):
            qaug_ref[hf, rows, 0:nbh] = bias[:, hf * nbh:(hf + 1) * nbh]
            qaug_ref[hf, rows, kcol] = qs[:, h * HEAD_DIM:(h + 1) * HEAD_DIM]
    bound = _tile_score_bound(qs, kmax_ref[0:1, 0:1])
    safe = jnp.max(bound) <= SAFE_SCORE_BOUND

    def scores(start, width, diagonal):
        ka = kaug_ref[0, 0, pl.ds(start, width), :]
        qa = qaug_ref[start // (nbh * SEL_BLOCK)]
        s = lax.dot_general(qa, ka, (((1,), (1,)), ((), ())), preferred_element_type=F32)
        if diagonal:
            r = lax.broadcasted_iota(jnp.int32, (m_rows, width), 0) & (tq - 1)
            c = lax.broadcasted_iota(jnp.int32, (m_rows, width), 1)
            s = jnp.where(c <= r, s, NEG_INF)
        return s

    def sweep(fn, carry):
        carry = lax.fori_loop(
            0, n_wide, lambda i, c: fn(pl.multiple_of(i * tk, tk), tk, False, c), carry)
        carry = lax.fori_loop(
            0, n_narrow, lambda i, c: fn(pl.multiple_of(n_wide * tk + i * tq, tq), tq, False, c), carry)
        return fn(pl.multiple_of(t0, tq), tq, True, carry)

    @pl.when(safe)
    def _():
        tail = jnp.broadcast_to(_shift_cols(bound), (m_rows, HEAD_DIM))
        for hf in range(n_half):
            qaug_ref[hf, :, tcol] = tail

    @pl.when(jnp.logical_not(safe))
    def _():
        for hf in range(n_half):
            qaug_ref[hf, :, tcol] = jnp.zeros((m_rows, HEAD_DIM), BF16)
        mx = sweep(lambda st, w, dg, m: jnp.maximum(m, jnp.max(scores(st, w, dg), axis=-1, keepdims=True)),
                   jnp.full((m_rows, 1), NEG_INF, F32))
        tail = _shift_cols(mx)
        for hf in range(n_half):
            qaug_ref[hf, :, tcol] = tail

    acc_ref[...] = jnp.zeros(acc_ref.shape, F32)

    def tile(start, width, diagonal, carry):
        p = jnp.exp(scores(start, width, diagonal)).astype(BF16)
        acc_ref[...] += jnp.dot(p, vaug_ref[0, 0, pl.ds(start, width), :], preferred_element_type=F32)
        return carry

    sweep(tile, 0)
    acc = acc_ref[...]
    o = acc[:, 0:HEAD_DIM] / acc[:, HEAD_DIM:HEAD_DIM + 1]
    gates = gates_ref[0]
    o_ref[0] = jnp.concatenate(
        [o[h * tq:(h + 1) * tq] * gates[:, HPG + h:HPG + h + 1] for h in range(HPG)], axis=-1)


def _slc_call(q_rot, bias, kaug, v, gates, tq, tk, nbh):
    b, s, _ = q_rot.shape
    assert tk % tq == 0 and (nbh * SEL_BLOCK) % tk == 0 and s % tk == 0
    nsel = bias.shape[3]
    gw = HPG * HEAD_DIM
    aug = nbh + 2 * HEAD_DIM
    qspec = pl.BlockSpec((1, tq, gw), lambda bi, g, i: (bi, i, g))
    return pl.pallas_call(
        functools.partial(_slc_kernel, nbh, tk),
        grid=(b, N_GROUPS, s // tq),
        in_specs=[qspec,
                  pl.BlockSpec((1, 1, tq, nsel), lambda bi, g, i: (bi, g, i, 0)),
                  pl.BlockSpec((1, 1, s, aug), lambda bi, g, i: (bi, g, 0, 0)),
                  pl.BlockSpec((1, 1, s, 2 * HEAD_DIM), lambda bi, g, i: (bi, g, 0, 0)),
                  pl.BlockSpec((1, tq, LANES), lambda bi, g, i: (bi, i, g))],
        out_specs=qspec,
        out_shape=jax.ShapeDtypeStruct((b, s, D_ATTN), F32),
        scratch_shapes=[pltpu.VMEM((nsel // nbh, HPG * tq, aug), BF16),
                        pltpu.VMEM((HPG * tq, 2 * HEAD_DIM), F32),
                        pltpu.VMEM((8, LANES), F32)],
        compiler_params=pltpu.CompilerParams(dimension_semantics=("arbitrary",) * 3,
                                             vmem_limit_bytes=VMEM_LIMIT),
        name="slc_attn",
    )(q_rot, bias, kaug, v, gates)


def _win_kernel(tq, q_ref, kaug_ref, vaug_ref, gates_ref, o_ref, qaug_ref, kmax_ref):
    qb = pl.program_id(2)
    tb = q_ref.shape[1]
    n_sub = tb // tq
    span = WINDOW + tq
    m_rows = HPG * tq
    tcol = slice(HEAD_DIM, 2 * HEAD_DIM)

    @pl.when(qb == 0)
    def _():
        def body(c, mx):
            k = kaug_ref[0, 0, pl.ds(pl.multiple_of(c * tq, tq), tq), 0:HEAD_DIM].astype(F32)
            return jnp.maximum(mx, jnp.sum(k * k, axis=-1, keepdims=True))
        mx = lax.fori_loop(0, kaug_ref.shape[2] // tq, body, jnp.zeros((tq, 1), F32))
        kmax_ref[...] = jnp.broadcast_to(jnp.sqrt(jnp.max(mx, axis=0, keepdims=True)), kmax_ref.shape)

    qs = q_ref[0] * SCALE
    for j in range(n_sub):
        for h in range(HPG):
            qaug_ref[j, h * tq:(h + 1) * tq, 0:HEAD_DIM] = qs[j * tq:(j + 1) * tq, h * HEAD_DIM:(h + 1) * HEAD_DIM]
    bound = _tile_score_bound(qs, kmax_ref[0:1, 0:1])
    safe = jnp.max(bound) <= SAFE_SCORE_BOUND

    def band_start(j):
        return pl.multiple_of(jnp.maximum(qb * tb + j * tq - WINDOW, 0), tq)

    def scores(j):
        t0 = qb * tb + j * tq
        start = band_start(j)
        ka = kaug_ref[0, 0, pl.ds(start, span), :]
        s = lax.dot_general(qaug_ref[j], ka, (((1,), (1,)), ((), ())), preferred_element_type=F32)
        d = (lax.broadcasted_iota(jnp.int32, (m_rows, tq), 1)
             - (lax.broadcasted_iota(jnp.int32, (m_rows, tq), 0) & (tq - 1)))
        blocks = []
        for blk in range(span // tq):
            off = t0 - start - blk * tq
            ok = lax.bitcast_convert_type(off - d, jnp.uint32) < jnp.uint32(WINDOW)
            blocks.append(jnp.where(ok, s[:, blk * tq:(blk + 1) * tq], NEG_INF))
        return jnp.concatenate(blocks, axis=-1)

    @pl.when(safe)
    def _():
        for j in range(n_sub):
            qaug_ref[j, :, tcol] = jnp.broadcast_to(_shift_cols(bound), (m_rows, HEAD_DIM))

    @pl.when(jnp.logical_not(safe))
    def _():
        for j in range(n_sub):
            qaug_ref[j, :, tcol] = jnp.zeros((m_rows, HEAD_DIM), BF16)
            qaug_ref[j, :, tcol] = _shift_cols(jnp.max(scores(j), axis=-1, keepdims=True))

    gates = gates_ref[0]
    for j in range(n_sub):
        p = jnp.exp(scores(j)).astype(BF16)
        acc = jnp.dot(p, vaug_ref[0, 0, pl.ds(band_start(j), span), :], preferred_element_type=F32)
        o = acc[:, 0:HEAD_DIM] / acc[:, HEAD_DIM:HEAD_DIM + 1]
        gj = gates[j * tq:(j + 1) * tq]
        o_ref[0, j * tq:(j + 1) * tq, :] = jnp.concatenate(
            [o[h * tq:(h + 1) * tq] * gj[:, 2 * HPG + h:2 * HPG + h + 1] for h in range(HPG)], axis=-1)


def _win_call(q_rot, k, v, gates, tb, tq):
    b, s, _ = q_rot.shape
    assert tb % tq == 0 and WINDOW % tq == 0
    gw = HPG * HEAD_DIM
    qspec = pl.BlockSpec((1, tb, gw), lambda bi, g, i: (bi, i, g))
    kvspec = pl.BlockSpec((1, 1, s, 2 * HEAD_DIM), lambda bi, g, i: (bi, g, 0, 0))
    return pl.pallas_call(
        functools.partial(_win_kernel, tq),
        grid=(b, N_GROUPS, s // tb),
        in_specs=[qspec, kvspec, kvspec, pl.BlockSpec((1, tb, LANES), lambda bi, g, i: (bi, i, g))],
        out_specs=qspec,
        out_shape=jax.ShapeDtypeStruct((b, s, D_ATTN), F32),
        scratch_shapes=[pltpu.VMEM((tb // tq, HPG * tq, 2 * HEAD_DIM), BF16),
                        pltpu.VMEM((8, LANES), F32)],
        compiler_params=pltpu.CompilerParams(dimension_semantics=("arbitrary",) * 3,
                                             vmem_limit_bytes=VMEM_LIMIT),
        name="win_attn",
    )(q_rot, k, v, gates)


def _out_kernel(x_ref, oc_ref, os_ref, ow_ref, convn_ref, ga_ref, wout_ref, gpost_ref,
                gpre2_ref, wg_ref, wu_ref, wd_ref, gpost2_ref, o_ref):
    attn = oc_ref[...] + os_ref[...] + ow_ref[...]
    an = _rms(attn, ga_ref[...]).astype(BF16)
    h = (jnp.dot(an, wout_ref[0:D_ATTN, :], preferred_element_type=F32)
         + jnp.dot(convn_ref[...], wout_ref[D_ATTN:D_MODEL, :], preferred_element_type=F32))
    x1 = x_ref[...] + _rms(h, gpost_ref[...])
    o_ref[...] = _ffn(x1, gpre2_ref[...], wg_ref, wu_ref, wd_ref, gpost2_ref[...])


def _out_call(x2d, oc, osl, ow, convn, ga, wout, gpost, gpre2, wg, wu, wd, gpost2, tm):
    t = x2d.shape[0]
    row = lambda width: pl.BlockSpec((tm, width), lambda i: (i, 0))
    return pl.pallas_call(
        _out_kernel,
        grid=(t // tm,),
        in_specs=[row(D_MODEL), row(D_ATTN), row(D_ATTN), row(D_ATTN), row(D_CONV),
                  _const_spec((1, D_ATTN)), _const_spec(wout.shape), _const_spec((1, D_MODEL)),
                  _const_spec((1, D_MODEL)), _const_spec(wg.shape), _const_spec(wu.shape),
                  _const_spec(wd.shape), _const_spec((1, D_MODEL))],
        out_specs=row(D_MODEL),
        out_shape=jax.ShapeDtypeStruct(x2d.shape, F32),
        compiler_params=pltpu.CompilerParams(dimension_semantics=("arbitrary",),
                                             vmem_limit_bytes=VMEM_LIMIT),
        name="outproj_ffn2",
    )(x2d, oc, osl, ow, convn, ga, wout, gpost, gpre2, wg, wu, wd, gpost2)


def _prep_w_in(w_in):
    sizes = [D_ATTN] + [D_KV] * 6 + [N_GATES] + [D_CONV] * 3
    cuts = np.cumsum([0] + sizes)
    q, kc, vc, ks, vs, kw, vw, gt, bg, cg, xc = [w_in[:, cuts[i]:cuts[i + 1]] for i in range(len(sizes))]
    gt = gt.reshape(D_MODEL, N_GROUPS, HPG, 3).transpose(0, 1, 3, 2).reshape(D_MODEL, N_GROUPS, 3 * HPG)
    gt = jnp.pad(gt, ((0, 0), (0, 0), (0, LANES - 3 * HPG))).reshape(D_MODEL, GATE_COLS)
    return jnp.concatenate([q, kc, vc, ks, vs, kw, vw, gt, bg, cg, xc], axis=1).astype(BF16)


def _prep_compress(pe, w1, w2):
    half = CMP_BLOCK // 2
    eye = jnp.eye(N_GROUPS, dtype=F32)
    w1r = w1.reshape(2, half, HEAD_DIM, CMP_HIDDEN)
    w1big = jnp.einsum("ptdc,gh->ptgdhc", w1r, eye).reshape(2, half * D_KV, N_GROUPS * CMP_HIDDEN)
    w2big = jnp.einsum("cd,gh->gchd", w2, eye).reshape(N_GROUPS * CMP_HIDDEN, D_KV)
    pe_rows = jnp.broadcast_to(pe.reshape(2, half, 1, HEAD_DIM), (2, half, N_GROUPS, HEAD_DIM))
    return pe_rows.reshape(2, half * D_KV), w1big.astype(BF16), w2big.astype(BF16)


def _agg_matrix(nch, nsel, cw):
    agg_w = np.convolve(np.ones(SEL_RATIO), np.ones(CMP_BLOCK // CMP_STRIDE))
    a = np.zeros((nch, nsel), np.float32)
    for j in range(nsel):
        for o, wgt in enumerate(agg_w):
            c = SEL_RATIO * j + o - (CMP_BLOCK // CMP_STRIDE - 1)
            if 0 <= c < nch - 1:
                a[c, j] = wgt
    a = a.T.reshape(nsel, nch // cw, cw).transpose(1, 0, 2)
    return jnp.asarray(a, BF16)


def _rope_inv_freq_row():
    inv = ROPE_THETA ** (-np.arange(ROPE_HALF, dtype=np.float32) * 2.0 / ROPE_DIM)
    lane = np.arange(LANES) % HEAD_DIM
    row = np.where(lane < ROPE_DIM, inv[lane % ROPE_HALF], 0.0).astype(np.float32)
    return jnp.asarray(row.reshape(1, LANES))


def _forward(x, positions, p, *, tm, tq, tc, nbh, tqs, tks, tqc, cw, tbw):
    b, s, _ = x.shape
    depth = p["w_in"].shape[0]
    nch = s // CMP_STRIDE
    nsel = s // SEL_BLOCK
    cos, sin = _rope_call(positions.reshape(b, s, 1), _rope_inv_freq_row(), tm)
    agg = _agg_matrix(nch, nsel, cw)
    row = lambda v: v.reshape(1, -1)
    for l in range(depth):
        x2d = x.reshape(b * s, D_MODEL)
        x2d = _ffn_call(x2d, row(p["ffn1_norm_pre"][l]), p["ffn1_w_gate"][l].astype(BF16),
                        p["ffn1_w_up"][l].astype(BF16), p["ffn1_w_down"][l].astype(BF16),
                        row(p["ffn1_norm_post"][l]), tm)
        (q_raw, q_rot, kc_in, vc_in, ksaug, vs, kw, vw, gates, convn) = _inproj_call(
            x2d.reshape(b, s, D_MODEL), cos, sin, row(p["mix_norm_pre"][l]), _prep_w_in(p["w_in"][l]),
            p["conv_w"][l], row(p["conv_out_norm"][l]), tm, nbh)
        pek, w1k, w2k = _prep_compress(p["cmp_pe_k"][l], p["cmp_w1_k"][l], p["cmp_w2_k"][l])
        pev, w1v, w2v = _prep_compress(p["cmp_pe_v"][l], p["cmp_w1_v"][l], p["cmp_w2_v"][l])
        kcmp, vcmp = _compress_call(kc_in.reshape(b, nch, CMP_STRIDE * D_KV),
                                    vc_in.reshape(b, nch, CMP_STRIDE * D_KV),
                                    pek, w1k, w2k, pev, w1v, w2v, tc)
        o_cmp, bias = _cmp_call(q_raw, kcmp, vcmp, gates, agg, tqc, cw)
        o_slc = _slc_call(q_rot, bias, ksaug, vs, gates, tqs, tks, nbh)
        o_win = _win_call(q_rot, kw, vw, gates, tbw, tq)
        flat = lambda a: a.reshape(b * s, a.shape[-1])
        x2d = _out_call(x2d, flat(o_cmp), flat(o_slc), flat(o_win), flat(convn),
                        row(p["attn_out_norm"][l]), p["w_out"][l].astype(BF16), row(p["mix_norm_post"][l]),
                        row(p["ffn2_norm_pre"][l]), p["ffn2_w_gate"][l].astype(BF16),
                        p["ffn2_w_up"][l].astype(BF16), p["ffn2_w_down"][l].astype(BF16),
                        row(p["ffn2_norm_post"][l]), tm)
        x = x2d.reshape(b, s, D_MODEL)
    return x


def kernel(x, positions, ffn1_norm_pre, ffn1_w_gate, ffn1_w_up, ffn1_w_down, ffn1_norm_post, mix_norm_pre, w_in, cmp_pe_k, cmp_w1_k, cmp_w2_k, cmp_pe_v, cmp_w1_v, cmp_w2_v, conv_w, attn_out_norm, conv_out_norm, w_out, mix_norm_post, ffn2_norm_pre, ffn2_w_gate, ffn2_w_up, ffn2_w_down, ffn2_norm_post):
    params = dict(
        ffn1_norm_pre=ffn1_norm_pre, ffn1_w_gate=ffn1_w_gate, ffn1_w_up=ffn1_w_up, ffn1_w_down=ffn1_w_down,
        ffn1_norm_post=ffn1_norm_post, mix_norm_pre=mix_norm_pre, w_in=w_in,
        cmp_pe_k=cmp_pe_k, cmp_w1_k=cmp_w1_k, cmp_w2_k=cmp_w2_k,
        cmp_pe_v=cmp_pe_v, cmp_w1_v=cmp_w1_v, cmp_w2_v=cmp_w2_v,
        conv_w=conv_w, attn_out_norm=attn_out_norm, conv_out_norm=conv_out_norm, w_out=w_out,
        mix_norm_post=mix_norm_post, ffn2_norm_pre=ffn2_norm_pre, ffn2_w_gate=ffn2_w_gate,
        ffn2_w_up=ffn2_w_up, ffn2_w_down=ffn2_w_down, ffn2_norm_post=ffn2_norm_post)
    return _forward(x, positions, params, tm=512, tq=256, tc=256, nbh=128, tqs=512, tks=1024, tqc=1024, cw=256, tbw=1024)
```

```python
import functools
import math

import numpy as np
import jax
import jax.numpy as jnp
from jax import lax
from jax.experimental import pallas as pl
from jax.experimental.pallas import tpu as pltpu

D_MODEL = 1024
N_HEADS = 8
HEAD_DIM = 64
N_GROUPS = 2
HPG = N_HEADS // N_GROUPS
D_ATTN = N_HEADS * HEAD_DIM
D_KV = N_GROUPS * HEAD_DIM
D_CONV = D_MODEL - D_ATTN
CONV_WIDTH = 3
CMP_BLOCK = 32
CMP_STRIDE = 16
CMP_SHIFT = 4
CMP_HIDDEN = 256
SEL_BLOCK = 64
SEL_SHIFT = 6
SEL_RATIO = SEL_BLOCK // CMP_STRIDE
N_SELECT = 16
N_FORCED = 3
WINDOW = 512
ROPE_THETA = 500000.0
ROPE_DIM = HEAD_DIM // 4
ROPE_HALF = ROPE_DIM // 2
D_FF = 2816
N_GATES = 3 * N_HEADS
EPS = 1e-6
NEG_INF = -1e30
FORCE_SCORE = 1e9
SCALE = 1.0 / math.sqrt(HEAD_DIM)

LANES = 128
SUBLANES = 8
GATE_COLS = N_GROUPS * LANES
VMEM_LIMIT = 56 * 1024 * 1024
FF_CHUNKS = ((0, 768), (768, 1536), (1536, 2304), (2304, 2816))
N_SHIFT_COLS = 2
SAFE_SCORE_BOUND = 40.0

F32 = jnp.float32
BF16 = jnp.bfloat16


def _const_spec(shape):
    nd = len(shape)
    return pl.BlockSpec(shape, lambda *_: (0,) * nd, pipeline_mode=pl.Buffered(1))


def _rms(x, g):
    ms = jnp.mean(x * x, axis=-1, keepdims=True)
    return x * lax.rsqrt(ms + EPS) * g


def _ffn(x, g_pre, wg_ref, wu_ref, wd_ref, g_post):
    h = _rms(x, g_pre).astype(BF16)
    d = None
    for c0, c1 in FF_CHUNKS:
        gate = jnp.dot(h, wg_ref[:, c0:c1], preferred_element_type=F32)
        up = jnp.dot(h, wu_ref[:, c0:c1], preferred_element_type=F32)
        a = (gate * jax.nn.sigmoid(gate) * up).astype(BF16)
        part = jnp.dot(a, wd_ref[c0:c1, :], preferred_element_type=F32)
        d = part if d is None else d + part
    return x + 0.5 * _rms(d, g_post)


def _ffn_kernel(x_ref, gpre_ref, wg_ref, wu_ref, wd_ref, gpost_ref, o_ref):
    o_ref[...] = _ffn(x_ref[...], gpre_ref[...], wg_ref, wu_ref, wd_ref, gpost_ref[...])


def _ffn_call(x2d, g_pre, wg, wu, wd, g_post, tm):
    t = x2d.shape[0]
    row = pl.BlockSpec((tm, D_MODEL), lambda i: (i, 0))
    return pl.pallas_call(
        _ffn_kernel,
        grid=(t // tm,),
        in_specs=[row, _const_spec((1, D_MODEL)), _const_spec(wg.shape), _const_spec(wu.shape),
                  _const_spec(wd.shape), _const_spec((1, D_MODEL))],
        out_specs=row,
        out_shape=jax.ShapeDtypeStruct(x2d.shape, F32),
        compiler_params=pltpu.CompilerParams(dimension_semantics=("arbitrary",),
                                             vmem_limit_bytes=VMEM_LIMIT),
        name="ffn1",
    )(x2d, g_pre, wg, wu, wd, g_post)


_C_Q = 0
_C_KC = _C_Q + D_ATTN
_C_VC = _C_KC + D_KV
_C_KS = _C_VC + D_KV
_C_VS = _C_KS + D_KV
_C_KW = _C_VS + D_KV
_C_VW = _C_KW + D_KV
_C_GATE = _C_VW + D_KV
_C_BG = _C_GATE + GATE_COLS
_C_CG = _C_BG + D_CONV
_C_XC = _C_CG + D_CONV
_C_END = _C_XC + D_CONV


def _rope_kernel(pos_ref, invf_ref, cos_ref, sin_ref):
    ang = pos_ref[0].astype(F32) * invf_ref[...]
    cos_ref[0] = jnp.cos(ang)
    sin_ref[0] = jnp.sin(ang)


def _rope_call(pos3, invf, tm):
    b, s, _ = pos3.shape
    row = lambda width: pl.BlockSpec((1, tm, width), lambda bi, i: (bi, i, 0))
    shape = jax.ShapeDtypeStruct((b, s, LANES), F32)
    return pl.pallas_call(
        _rope_kernel,
        grid=(b, s // tm),
        in_specs=[row(1), _const_spec((1, LANES))],
        out_specs=[row(LANES), row(LANES)],
        out_shape=[shape, shape],
        compiler_params=pltpu.CompilerParams(dimension_semantics=("arbitrary", "arbitrary"),
                                             vmem_limit_bytes=VMEM_LIMIT),
        name="rope_tables",
    )(pos3, invf)


def _inproj_kernel(nbh, x_ref, cos_ref, sin_ref, g_ref, w_ref, convw_ref, convg_ref,
                   qraw_ref, qrot_ref, kc_ref, vc_ref, ksaug_ref, vs_ref, kw_ref, vw_ref,
                   gates_ref, convn_ref, ubuf_ref):
    i = pl.program_id(1)
    tm = x_ref.shape[1]
    h = _rms(x_ref[0], g_ref[...]).astype(BF16)

    def proj(c0, c1):
        return jnp.dot(h, w_ref[:, c0:c1], preferred_element_type=F32)

    cos = cos_ref[0]
    sin = sin_ref[0]
    lane = lax.broadcasted_iota(jnp.int32, (1, LANES), 1) & (HEAD_DIM - 1)
    s_lo = jnp.where(lane < ROPE_HALF, -sin, 0.0)
    s_hi = jnp.where(lane >= ROPE_HALF, sin, 0.0)

    def rope(z):
        return (z * cos + pltpu.roll(z, LANES - ROPE_HALF, 1) * s_lo
                + pltpu.roll(z, ROPE_HALF, 1) * s_hi)

    for c in range(0, D_ATTN // LANES, 2):
        zq2 = proj(_C_Q + c * LANES, _C_Q + (c + 2) * LANES)
        for cc in (c, c + 1):
            zq = zq2[:, (cc - c) * LANES:(cc - c + 1) * LANES]
            qraw_ref[0, :, cc * LANES:(cc + 1) * LANES] = zq.astype(BF16)
            qrot_ref[0, :, cc * LANES:(cc + 1) * LANES] = rope(zq).astype(BF16)

    kvc = proj(_C_KC, _C_KS)
    kc_ref[0] = kvc[:, 0:D_KV]
    vc_ref[0] = kvc[:, D_KV:2 * D_KV]
    kvs = proj(_C_KS, _C_KW)
    ks = rope(kvs[:, 0:D_KV]).astype(BF16)
    vs = kvs[:, D_KV:2 * D_KV].astype(BF16)
    kvw = proj(_C_KW, _C_GATE)
    kw = rope(kvw[:, 0:D_KV]).astype(BF16)
    vw = kvw[:, D_KV:2 * D_KV].astype(BF16)
    row_blk = ((i * tm + lax.broadcasted_iota(jnp.int32, (tm, nbh), 0)) >> SEL_SHIFT) & (nbh - 1)
    onehot = jnp.where(row_blk == lax.broadcasted_iota(jnp.int32, (tm, nbh), 1), 1.0, 0.0).astype(BF16)
    tail = jnp.where(lax.broadcasted_iota(jnp.int32, (tm, HEAD_DIM), 1) < N_SHIFT_COLS, 1.0, 0.0).astype(BF16)
    for g in range(N_GROUPS):
        sl = slice(g * HEAD_DIM, (g + 1) * HEAD_DIM)
        ksaug_ref[0, g, :, 0:nbh] = onehot
        ksaug_ref[0, g, :, nbh:nbh + HEAD_DIM] = ks[:, sl]
        ksaug_ref[0, g, :, nbh + HEAD_DIM:nbh + 2 * HEAD_DIM] = tail
        vs_ref[0, g, :, 0:HEAD_DIM] = vs[:, sl]
        vs_ref[0, g, :, HEAD_DIM:2 * HEAD_DIM] = tail
        kw_ref[0, g, :, 0:HEAD_DIM] = kw[:, sl]
        kw_ref[0, g, :, HEAD_DIM:2 * HEAD_DIM] = tail
        vw_ref[0, g, :, 0:HEAD_DIM] = vw[:, sl]
        vw_ref[0, g, :, HEAD_DIM:2 * HEAD_DIM] = tail

    gates_ref[0] = jax.nn.sigmoid(proj(_C_GATE, _C_BG))

    u = proj(_C_CG, _C_XC) * proj(_C_XC, _C_END)

    @pl.when(i == 0)
    def _():
        ubuf_ref[0:8, :] = jnp.zeros((8, D_CONV), F32)

    @pl.when(i > 0)
    def _():
        ubuf_ref[0:8, :] = ubuf_ref[tm:tm + 8, :]

    ubuf_ref[8:tm + 8, :] = u
    w = convw_ref[...]
    y = (w[2:3, :] * u + w[1:2, :] * ubuf_ref[7:tm + 7, :] + w[0:1, :] * ubuf_ref[6:tm + 6, :])
    conv = proj(_C_BG, _C_CG) * y
    convn_ref[0] = _rms(conv, convg_ref[...]).astype(BF16)


def _inproj_call(x, cos, sin, g_pre, w_in, conv_w, conv_g, tm, nbh):
    b, s, _ = x.shape
    grid = (b, s // tm)
    row = lambda width: pl.BlockSpec((1, tm, width), lambda bi, i: (bi, i, 0))
    grp = lambda width: pl.BlockSpec((1, N_GROUPS, tm, width), lambda bi, i: (bi, 0, i, 0))
    kern = functools.partial(_inproj_kernel, nbh)
    return pl.pallas_call(
        kern,
        grid=grid,
        in_specs=[row(D_MODEL), row(LANES), row(LANES), _const_spec((1, D_MODEL)), _const_spec(w_in.shape),
                  _const_spec((CONV_WIDTH, D_CONV)), _const_spec((1, D_CONV))],
        out_specs=[row(D_ATTN), row(D_ATTN), row(D_KV), row(D_KV), grp(nbh + 2 * HEAD_DIM), grp(2 * HEAD_DIM),
                   grp(2 * HEAD_DIM), grp(2 * HEAD_DIM), row(GATE_COLS), row(D_CONV)],
        out_shape=[
            jax.ShapeDtypeStruct((b, s, D_ATTN), BF16),
            jax.ShapeDtypeStruct((b, s, D_ATTN), BF16),
            jax.ShapeDtypeStruct((b, s, D_KV), F32),
            jax.ShapeDtypeStruct((b, s, D_KV), F32),
            jax.ShapeDtypeStruct((b, N_GROUPS, s, nbh + 2 * HEAD_DIM), BF16),
            jax.ShapeDtypeStruct((b, N_GROUPS, s, 2 * HEAD_DIM), BF16),
            jax.ShapeDtypeStruct((b, N_GROUPS, s, 2 * HEAD_DIM), BF16),
            jax.ShapeDtypeStruct((b, N_GROUPS, s, 2 * HEAD_DIM), BF16),
            jax.ShapeDtypeStruct((b, s, GATE_COLS), F32),
            jax.ShapeDtypeStruct((b, s, D_CONV), BF16),
        ],
        scratch_shapes=[pltpu.VMEM((tm + 8, D_CONV), F32)],
        compiler_params=pltpu.CompilerParams(dimension_semantics=("arbitrary", "arbitrary"),
                                             vmem_limit_bytes=VMEM_LIMIT),
        name="inproj",
    )(x, cos, sin, g_pre, w_in, conv_w, conv_g)


def _compress_kernel(kc_ref, kcn_ref, vc_ref, vcn_ref, pek_ref, w1k_ref, w2k_ref,
                     pev_ref, w1v_ref, w2v_ref, ko_ref, vo_ref):
    tc = kc_ref.shape[1]
    hid_w = N_GROUPS * CMP_HIDDEN
    last = lax.broadcasted_iota(jnp.int32, (tc, 1), 0) == tc - 1

    def one(x_ref, xn_ref, pe_ref, w1_ref, w2_ref, o_ref):
        x = x_ref[0]
        top = jnp.dot((x + pe_ref[0:1, :]).astype(BF16), w1_ref[0], preferred_element_type=F32)
        xb = (x + pe_ref[1:2, :]).astype(BF16)
        bot = jnp.dot(xb, w1_ref[1], preferred_element_type=F32)
        xnb = (xn_ref[0] + pe_ref[1:2, :]).astype(BF16)
        botn = jnp.dot(xnb, w1_ref[1], preferred_element_type=F32)
        shifted = jnp.where(last, botn[0:1, :], pltpu.roll(bot, tc - 1, 0))
        hid = jax.nn.gelu(top + shifted).astype(BF16)
        out = jnp.dot(hid, w2_ref[...], preferred_element_type=F32)
        tail = jnp.where(lax.broadcasted_iota(jnp.int32, (tc, HEAD_DIM), 1) < N_SHIFT_COLS, 1.0, 0.0).astype(BF16)
        for g in range(N_GROUPS):
            o_ref[0, g, :, 0:HEAD_DIM] = out[:, g * HEAD_DIM:(g + 1) * HEAD_DIM].astype(BF16)
            o_ref[0, g, :, HEAD_DIM:2 * HEAD_DIM] = tail

    one(kc_ref, kcn_ref, pek_ref, w1k_ref, w2k_ref, ko_ref)
    one(vc_ref, vcn_ref, pev_ref, w1v_ref, w2v_ref, vo_ref)


def _compress_call(kc_in, vc_in, pek, w1k, w2k, pev, w1v, w2v, tc):
    b, nch, width = kc_in.shape
    nt = nch // tc
    last8 = nch // 8 - 1
    cur = pl.BlockSpec((1, tc, width), lambda bi, i: (bi, i, 0))
    nxt = pl.BlockSpec((1, 8, width), lambda bi, i: (bi, jnp.minimum((i + 1) * (tc // 8), last8), 0))
    out = pl.BlockSpec((1, N_GROUPS, tc, 2 * HEAD_DIM), lambda bi, i: (bi, 0, i, 0))
    oshape = jax.ShapeDtypeStruct((b, N_GROUPS, nch, 2 * HEAD_DIM), BF16)
    return pl.pallas_call(
        _compress_kernel,
        grid=(b, nt),
        in_specs=[cur, nxt, cur, nxt, _const_spec(pek.shape), _const_spec(w1k.shape), _const_spec(w2k.shape),
                  _const_spec(pev.shape), _const_spec(w1v.shape), _const_spec(w2v.shape)],
        out_specs=[out, out],
        out_shape=[oshape, oshape],
        compiler_params=pltpu.CompilerParams(dimension_semantics=("arbitrary", "arbitrary"),
                                             vmem_limit_bytes=VMEM_LIMIT),
        name="compress",
    )(kc_in, kc_in, vc_in, vc_in, pek, w1k, w2k, pev, w1v, w2v)


def _split3(x):
    hi = x.astype(BF16)
    r = x - hi.astype(F32)
    mid = r.astype(BF16)
    lo = (r - mid.astype(F32)).astype(BF16)
    return hi, mid, lo


def _tile_score_bound(qs, kmax):
    qf = qs.astype(F32)
    sq = qf * qf
    nrm2 = None
    for h in range(HPG):
        rows = jnp.sum(sq[:, h * HEAD_DIM:(h + 1) * HEAD_DIM], axis=-1, keepdims=True)
        top = jnp.max(rows, axis=0, keepdims=True)
        nrm2 = top if nrm2 is None else jnp.maximum(nrm2, top)
    return jnp.sqrt(nrm2) * kmax


def _shift_cols(shift):
    hi = shift.astype(BF16).astype(F32)
    lo = (shift - hi).astype(BF16).astype(F32)
    lane = lax.broadcasted_iota(jnp.int32, (1, HEAD_DIM), 1)
    return jnp.where(lane == 0, -hi, jnp.where(lane == 1, -lo, 0.0)).astype(BF16)


def _cmp_kernel(k_top, cw, q_ref, kaug_ref, vaug_ref, gates_ref, aggt_ref, o_ref, bias_ref,
                qaug_ref, e_ref, acc_ref, kmax_ref, pslc_ref):
    qt = pl.program_id(2)
    tq = q_ref.shape[1]
    nch = kaug_ref.shape[2]
    nsel = bias_ref.shape[3]
    m_rows = HPG * tq
    t0 = qt * tq
    tcol = slice(HEAD_DIM, 2 * HEAD_DIM)
    c_last = ((t0 + tq - CMP_BLOCK) >> CMP_SHIFT) // cw
    c_mask = jnp.maximum(c_last - 1, 0)

    @pl.when(qt == 0)
    def _():
        k = kaug_ref[0, 0, :, 0:HEAD_DIM].astype(F32)
        n_ok = lax.broadcasted_iota(jnp.int32, (nch, 1), 0) < nch - 1
        ksq = jnp.where(n_ok, jnp.sum(k * k, axis=-1, keepdims=True), 0.0)
        kmax_ref[...] = jnp.broadcast_to(jnp.sqrt(jnp.max(ksq, axis=0, keepdims=True)), kmax_ref.shape)

    qs = q_ref[0] * SCALE
    for h in range(HPG):
        qaug_ref[h * tq:(h + 1) * tq, 0:HEAD_DIM] = qs[:, h * HEAD_DIM:(h + 1) * HEAD_DIM]
    bound = _tile_score_bound(qs, kmax_ref[0:1, 0:1])
    safe = jnp.max(bound) <= SAFE_SCORE_BOUND

    def scores(c, masked):
        ka = kaug_ref[0, 0, pl.ds(pl.multiple_of(c * cw, cw), cw), :]
        s = lax.dot_general(qaug_ref[...], ka, (((1,), (1,)), ((), ())), preferred_element_type=F32)
        if masked:
            row_t = t0 + (lax.broadcasted_iota(jnp.int32, (m_rows, 1), 0) & (tq - 1))
            n_vis = (row_t - (CMP_BLOCK - 1)) >> CMP_SHIFT
            n = c * cw + lax.broadcasted_iota(jnp.int32, (m_rows, cw), 1)
            s = jnp.where(n <= n_vis, s, NEG_INF)
        return s

    def sweep(fn_full, fn_masked, init):
        carry = lax.fori_loop(0, c_mask, fn_full, init)
        return lax.fori_loop(c_mask, c_last + 1, fn_masked, carry)

    @pl.when(safe)
    def _():
        qaug_ref[:, tcol] = jnp.broadcast_to(_shift_cols(bound), (m_rows, HEAD_DIM))

    @pl.when(jnp.logical_not(safe))
    def _():
        qaug_ref[:, tcol] = jnp.zeros((m_rows, HEAD_DIM), BF16)
        step = lambda masked: (lambda c, m: jnp.maximum(m, jnp.max(scores(c, masked), axis=-1, keepdims=True)))
        mx = sweep(step(False), step(True), jnp.full((m_rows, 1), NEG_INF, F32))
        qaug_ref[:, tcol] = _shift_cols(jnp.where(mx > 0.5 * NEG_INF, mx, 0.0))

    acc_ref[...] = jnp.zeros(acc_ref.shape, F32)

    def chunk(masked):
        def body(c, carry):
            e = jnp.exp(scores(c, masked))
            start = pl.multiple_of(c * cw, cw)
            e_ref[c] = e
            acc_ref[...] += jnp.dot(e.astype(BF16), vaug_ref[0, 0, pl.ds(start, cw), :],
                                    preferred_element_type=F32)
            return carry
        return body

    sweep(chunk(False), chunk(True), 0)
    acc = acc_ref[...]
    l = acc[:, HEAD_DIM:HEAD_DIM + 1]
    rinv = jnp.where(l > 0.0, 1.0 / l, 0.0)
    o = acc[:, 0:HEAD_DIM] * rinv
    gates = gates_ref[0]
    o_ref[0] = jnp.concatenate(
        [o[h * tq:(h + 1) * tq] * gates[:, h:h + 1] for h in range(HPG)], axis=-1)

    pslc_ref[...] = jnp.zeros(pslc_ref.shape, F32)

    def agg_body(c, carry):
        imp = sum(e_ref[c, h * tq:(h + 1) * tq, :] * rinv[h * tq:(h + 1) * tq] for h in range(HPG))
        at = aggt_ref[c]
        pslc_ref[...] += sum(lax.dot_general(at, part, (((1,), (1,)), ((), ())), preferred_element_type=F32)
                             for part in _split3(imp))
        return carry

    lax.fori_loop(0, c_last + 1, agg_body, 0)

    n_slabs = nsel // SUBLANES
    sub = lax.broadcasted_iota(jnp.int32, (SUBLANES, LANES), 0)
    sub_f = sub.astype(F32)
    no_slab = float(n_slabs)
    n_cols = tq // LANES

    def tree(fn, xs):
        while len(xs) > 1:
            xs = [fn(xs[i], xs[i + 1]) if i + 1 < len(xs) else xs[i] for i in range(0, len(xs), 2)]
        return xs[0]

    def first_slab(score, m):
        firsts = []
        for g0 in range(0, len(score), SUBLANES):
            slab = jnp.full((SUBLANES, LANES), no_slab, F32)
            for r in reversed(range(g0, min(g0 + SUBLANES, len(score)))):
                slab = jnp.where(score[r] == m, float(r), slab)
            firsts.append(slab)
        return tree(jnp.minimum, firsts)

    def select_blocks(n_act, group):
        for c0 in range(0, n_cols, group):
            cols = range(c0, min(c0 + group, n_cols))
            lanes = {c: slice(c * LANES, (c + 1) * LANES) for c in cols}
            cur = {c: (t0 + c * LANES + lax.broadcasted_iota(jnp.int32, (1, LANES), 1)) >> SEL_SHIFT
                   for c in cols}
            score = {}
            for c in cols:
                score[c] = []
                for r in range(n_act):
                    j = sub + r * SUBLANES
                    free = (j <= cur[c] - 2) & (j > 0)
                    score[c].append(jnp.where(
                        free, pslc_ref[r * SUBLANES:(r + 1) * SUBLANES, lanes[c]], -jnp.inf))
            for _ in range(max(k_top - N_FORCED, 0)):
                for c in cols:
                    m = jnp.max(tree(jnp.maximum, score[c]), axis=0, keepdims=True)
                    slab = first_slab(score[c], m)
                    block = slab * SUBLANES + sub_f
                    first = jnp.min(block, axis=0, keepdims=True)
                    taken = jnp.where(block == first, slab, -1.0)
                    score[c] = [jnp.where(taken == float(r), -jnp.inf, score[c][r]) for r in range(n_act)]
            for c in cols:
                for r in range(n_act):
                    pslc_ref[r * SUBLANES:(r + 1) * SUBLANES, lanes[c]] = jnp.where(
                        (sub + r * SUBLANES <= cur[c]) & (score[c][r] == -jnp.inf), 0.0, NEG_INF)
        if n_act < n_slabs:
            pslc_ref[n_act * SUBLANES:, :] = jnp.full(((n_slabs - n_act) * SUBLANES, tq), NEG_INF, F32)

    if n_slabs % 2 == 0:
        half = n_slabs // 2
        few = t0 + tq <= half * SUBLANES * SEL_BLOCK
        pl.when(few)(lambda: select_blocks(half, 2))
        pl.when(jnp.logical_not(few))(lambda: select_blocks(n_slabs, 1))
    else:
        select_blocks(n_slabs, 1)
    bias_ref[0, 0] = pslc_ref[...].T.astype(BF16)


def _cmp_call(q_raw, kaug, vaug, gates, aggt, tq, cw):
    b, s, _ = q_raw.shape
    nch = kaug.shape[2]
    nsel = aggt.shape[1]
    assert nch % cw == 0 and tq // CMP_STRIDE < cw
    k_top = min(N_SELECT, nsel)
    gw = HPG * HEAD_DIM
    m_rows = HPG * tq
    qspec = pl.BlockSpec((1, tq, gw), lambda bi, g, i: (bi, i, g))
    kvspec = pl.BlockSpec((1, 1, nch, 2 * HEAD_DIM), lambda bi, g, i: (bi, g, 0, 0))
    gspec = pl.BlockSpec((1, tq, LANES), lambda bi, g, i: (bi, i, g))
    return pl.pallas_call(
        functools.partial(_cmp_kernel, k_top, cw),
        grid=(b, N_GROUPS, s // tq),
        in_specs=[qspec, kvspec, kvspec, gspec, _const_spec(aggt.shape)],
        out_specs=[qspec, pl.BlockSpec((1, 1, tq, nsel), lambda bi, g, i: (bi, g, i, 0))],
        out_shape=[jax.ShapeDtypeStruct((b, s, D_ATTN), F32),
                   jax.ShapeDtypeStruct((b, N_GROUPS, s, nsel), BF16)],
        scratch_shapes=[pltpu.VMEM((m_rows, 2 * HEAD_DIM), BF16),
                        pltpu.VMEM((nch // cw, m_rows, cw), F32),
                        pltpu.VMEM((m_rows, 2 * HEAD_DIM), F32),
                        pltpu.VMEM((8, LANES), F32),
                        pltpu.VMEM((nsel, tq), F32)],
        compiler_params=pltpu.CompilerParams(dimension_semantics=("arbitrary",) * 3,
                                             vmem_limit_bytes=VMEM_LIMIT),
        name="cmp_attn_topk",
    )(q_raw, kaug, vaug, gates, aggt)


def _slc_kernel(nbh, tk, q_ref, bias_ref, kaug_ref, vaug_ref, gates_ref, o_ref,
                qaug_ref, acc_ref, kmax_ref):
    qt = pl.program_id(2)
    tq = q_ref.shape[1]
    m_rows = HPG * tq
    t0 = qt * tq
    n_wide = t0 // tk
    n_narrow = (t0 - n_wide * tk) // tq
    n_half = qaug_ref.shape[0]
    kcol = slice(nbh, nbh + HEAD_DIM)
    tcol = slice(nbh + HEAD_DIM, nbh + 2 * HEAD_DIM)

    @pl.when(qt == 0)
    def _():
        def body(c, mx):
            k = kaug_ref[0, 0, pl.ds(pl.multiple_of(c * tk, tk), tk), kcol].astype(F32)
            return jnp.maximum(mx, jnp.sum(k * k, axis=-1, keepdims=True))
        mx = lax.fori_loop(0, kaug_ref.shape[2] // tk, body, jnp.zeros((tk, 1), F32))
        kmax_ref[...] = jnp.broadcast_to(jnp.sqrt(jnp.max(mx, axis=0, keepdims=True)), kmax_ref.shape)

    qs = q_ref[0] * SCALE
    bias = bias_ref[0, 0]
    for h in range(HPG):
        rows = slice(h * tq, (h + 1) * tq)
        for hf in range(n_half):
            qaug_ref[hf, rows, 0:nbh] = bias[:, hf * nbh:(hf + 1) * nbh]
            qaug_ref[hf, rows, kcol] = qs[:, h * HEAD_DIM:(h + 1) * HEAD_DIM]
    bound = _tile_score_bound(qs, kmax_ref[0:1, 0:1])
    safe = jnp.max(bound) <= SAFE_SCORE_BOUND

    def scores(start, width, diagonal):
        ka = kaug_ref[0, 0, pl.ds(start, width), :]
        qa = qaug_ref[start // (nbh * SEL_BLOCK)]
        s = lax.dot_general(qa, ka, (((1,), (1,)), ((), ())), preferred_element_type=F32)
        if diagonal:
            r = lax.broadcasted_iota(jnp.int32, (m_rows, width), 0) & (tq - 1)
            c = lax.broadcasted_iota(jnp.int32, (m_rows, width), 1)
            s = jnp.where(c <= r, s, NEG_INF)
        return s

    def sweep(fn, carry):
        carry = lax.fori_loop(
            0, n_wide, lambda i, c: fn(pl.multiple_of(i * tk, tk), tk, False, c), carry)
        carry = lax.fori_loop(
            0, n_narrow, lambda i, c: fn(pl.multiple_of(n_wide * tk + i * tq, tq), tq, False, c), carry)
        return fn(pl.multiple_of(t0, tq), tq, True, carry)

    @pl.when(safe)
    def _():
        tail = jnp.broadcast_to(_shift_cols(bound), (m_rows, HEAD_DIM))
        for hf in range(n_half):
            qaug_ref[hf, :, tcol] = tail

    @pl.when(jnp.logical_not(safe))
    def _():
        for hf in range(n_half):
            qaug_ref[hf, :, tcol] = jnp.zeros((m_rows, HEAD_DIM), BF16)
        mx = sweep(lambda st, w, dg, m: jnp.maximum(m, jnp.max(scores(st, w, dg), axis=-1, keepdims=True)),
                   jnp.full((m_rows, 1), NEG_INF, F32))
        tail = _shift_cols(mx)
        for hf in range(n_half):
            qaug_ref[hf, :, tcol] = tail

    acc_ref[...] = jnp.zeros(acc_ref.shape, F32)

    def pv(start, width, diagonal):
        p = jnp.exp(scores(start, width, diagonal)).astype(BF16)
        return jnp.dot(p, vaug_ref[0, 0, pl.ds(start, width), :], preferred_element_type=F32)

    def tile(start, width, diagonal, carry):
        acc_ref[...] += pv(start, width, diagonal)
        return carry

    def wide_pair(i, carry):
        first = pl.multiple_of(2 * i * tk, tk)
        acc_ref[...] += pv(first, tk, False) + pv(pl.multiple_of(first + tk, tk), tk, False)
        return carry

    lax.fori_loop(0, n_wide // 2, wide_pair, 0)
    lax.fori_loop(2 * (n_wide // 2), n_wide, lambda i, c: tile(pl.multiple_of(i * tk, tk), tk, False, c), 0)
    lax.fori_loop(
        0, n_narrow, lambda i, c: tile(pl.multiple_of(n_wide * tk + i * tq, tq), tq, False, c), 0)
    tile(pl.multiple_of(t0, tq), tq, True, 0)
    acc = acc_ref[...]
    o = acc[:, 0:HEAD_DIM] / acc[:, HEAD_DIM:HEAD_DIM + 1]
    gates = gates_ref[0]
    o_ref[0] = jnp.concatenate(
        [o[h * tq:(h + 1) * tq] * gates[:, HPG + h:HPG + h + 1] for h in range(HPG)], axis=-1)


def _slc_call(q_rot, bias, kaug, v, gates, tq, tk, nbh):
    b, s, _ = q_rot.shape
    assert tk % tq == 0 and (nbh * SEL_BLOCK) % tk == 0 and s % tk == 0
    nsel = bias.shape[3]
    gw = HPG * HEAD_DIM
    aug = nbh + 2 * HEAD_DIM
    qspec = pl.BlockSpec((1, tq, gw), lambda bi, g, i: (bi, i, g))
    return pl.pallas_call(
        functools.partial(_slc_kernel, nbh, tk),
        grid=(b, N_GROUPS, s // tq),
        in_specs=[qspec,
                  pl.BlockSpec((1, 1, tq, nsel), lambda bi, g, i: (bi, g, i, 0)),
                  pl.BlockSpec((1, 1, s, aug), lambda bi, g, i: (bi, g, 0, 0)),
                  pl.BlockSpec((1, 1, s, 2 * HEAD_DIM), lambda bi, g, i: (bi, g, 0, 0)),
                  pl.BlockSpec((1, tq, LANES), lambda bi, g, i: (bi, i, g))],
        out_specs=qspec,
        out_shape=jax.ShapeDtypeStruct((b, s, D_ATTN), F32),
        scratch_shapes=[pltpu.VMEM((nsel // nbh, HPG * tq, aug), BF16),
                        pltpu.VMEM((HPG * tq, 2 * HEAD_DIM), F32),
                        pltpu.VMEM((8, LANES), F32)],
        compiler_params=pltpu.CompilerParams(dimension_semantics=("arbitrary",) * 3,
                                             vmem_limit_bytes=VMEM_LIMIT),
        name="slc_attn",
    )(q_rot, bias, kaug, v, gates)


def _win_kernel(tq, q_ref, kaug_ref, vaug_ref, gates_ref, o_ref, qaug_ref, kmax_ref):
    qb = pl.program_id(2)
    tb = q_ref.shape[1]
    n_sub = tb // tq
    span = WINDOW + tq
    m_rows = HPG * tq
    tcol = slice(HEAD_DIM, 2 * HEAD_DIM)

    @pl.when(qb == 0)
    def _():
        def body(c, mx):
            k = kaug_ref[0, 0, pl.ds(pl.multiple_of(c * tq, tq), tq), 0:HEAD_DIM].astype(F32)
            return jnp.maximum(mx, jnp.sum(k * k, axis=-1, keepdims=True))
        mx = lax.fori_loop(0, kaug_ref.shape[2] // tq, body, jnp.zeros((tq, 1), F32))
        kmax_ref[...] = jnp.broadcast_to(jnp.sqrt(jnp.max(mx, axis=0, keepdims=True)), kmax_ref.shape)

    qs = q_ref[0] * SCALE
    for j in range(n_sub):
        for h in range(HPG):
            qaug_ref[j, h * tq:(h + 1) * tq, 0:HEAD_DIM] = qs[j * tq:(j + 1) * tq, h * HEAD_DIM:(h + 1) * HEAD_DIM]
    bound = _tile_score_bound(qs, kmax_ref[0:1, 0:1])
    safe = jnp.max(bound) <= SAFE_SCORE_BOUND

    def band_start(j):
        return pl.multiple_of(jnp.maximum(qb * tb + j * tq - WINDOW, 0), tq)

    def scores(j):
        t0 = qb * tb + j * tq
        start = band_start(j)
        ka = kaug_ref[0, 0, pl.ds(start, span), :]
        s = lax.dot_general(qaug_ref[j], ka, (((1,), (1,)), ((), ())), preferred_element_type=F32)
        d = (lax.broadcasted_iota(jnp.int32, (m_rows, tq), 1)
             - (lax.broadcasted_iota(jnp.int32, (m_rows, tq), 0) & (tq - 1)))
        blocks = []
        for blk in range(span // tq):
            off = t0 - start - blk * tq
            ok = lax.bitcast_convert_type(off - d, jnp.uint32) < jnp.uint32(WINDOW)
            blocks.append(jnp.where(ok, s[:, blk * tq:(blk + 1) * tq], NEG_INF))
        return jnp.concatenate(blocks, axis=-1)

    @pl.when(safe)
    def _():
        for j in range(n_sub):
            qaug_ref[j, :, tcol] = jnp.broadcast_to(_shift_cols(bound), (m_rows, HEAD_DIM))

    @pl.when(jnp.logical_not(safe))
    def _():
        for j in range(n_sub):
            qaug_ref[j, :, tcol] = jnp.zeros((m_rows, HEAD_DIM), BF16)
            qaug_ref[j, :, tcol] = _shift_cols(jnp.max(scores(j), axis=-1, keepdims=True))

    gates = gates_ref[0]
    for j in range(n_sub):
        p = jnp.exp(scores(j)).astype(BF16)
        acc = jnp.dot(p, vaug_ref[0, 0, pl.ds(band_start(j), span), :], preferred_element_type=F32)
        o = acc[:, 0:HEAD_DIM] / acc[:, HEAD_DIM:HEAD_DIM + 1]
        gj = gates[j * tq:(j + 1) * tq]
        o_ref[0, j * tq:(j + 1) * tq, :] = jnp.concatenate(
            [o[h * tq:(h + 1) * tq] * gj[:, 2 * HPG + h:2 * HPG + h + 1] for h in range(HPG)], axis=-1)


def _win_call(q_rot, k, v, gates, tb, tq):
    b, s, _ = q_rot.shape
    assert tb % tq == 0 and WINDOW % tq == 0
    gw = HPG * HEAD_DIM
    qspec = pl.BlockSpec((1, tb, gw), lambda bi, g, i: (bi, i, g))
    kvspec = pl.BlockSpec((1, 1, s, 2 * HEAD_DIM), lambda bi, g, i: (bi, g, 0, 0))
    return pl.pallas_call(
        functools.partial(_win_kernel, tq),
        grid=(b, N_GROUPS, s // tb),
        in_specs=[qspec, kvspec, kvspec, pl.BlockSpec((1, tb, LANES), lambda bi, g, i: (bi, i, g))],
        out_specs=qspec,
        out_shape=jax.ShapeDtypeStruct((b, s, D_ATTN), F32),
        scratch_shapes=[pltpu.VMEM((tb // tq, HPG * tq, 2 * HEAD_DIM), BF16),
                        pltpu.VMEM((8, LANES), F32)],
        compiler_params=pltpu.CompilerParams(dimension_semantics=("arbitrary",) * 3,
                                             vmem_limit_bytes=VMEM_LIMIT),
        name="win_attn",
    )(q_rot, k, v, gates)


def _out_kernel(x_ref, oc_ref, os_ref, ow_ref, convn_ref, ga_ref, wout_ref, gpost_ref,
                gpre2_ref, wg_ref, wu_ref, wd_ref, gpost2_ref, o_ref):
    attn = oc_ref[...] + os_ref[...] + ow_ref[...]
    an = _rms(attn, ga_ref[...]).astype(BF16)
    h = (jnp.dot(an, wout_ref[0:D_ATTN, :], preferred_element_type=F32)
         + jnp.dot(convn_ref[...], wout_ref[D_ATTN:D_MODEL, :], preferred_element_type=F32))
    x1 = x_ref[...] + _rms(h, gpost_ref[...])
    o_ref[...] = _ffn(x1, gpre2_ref[...], wg_ref, wu_ref, wd_ref, gpost2_ref[...])


def _out_call(x2d, oc, osl, ow, convn, ga, wout, gpost, gpre2, wg, wu, wd, gpost2, tm):
    t = x2d.shape[0]
    row = lambda width: pl.BlockSpec((tm, width), lambda i: (i, 0))
    return pl.pallas_call(
        _out_kernel,
        grid=(t // tm,),
        in_specs=[row(D_MODEL), row(D_ATTN), row(D_ATTN), row(D_ATTN), row(D_CONV),
                  _const_spec((1, D_ATTN)), _const_spec(wout.shape), _const_spec((1, D_MODEL)),
                  _const_spec((1, D_MODEL)), _const_spec(wg.shape), _const_spec(wu.shape),
                  _const_spec(wd.shape), _const_spec((1, D_MODEL))],
        out_specs=row(D_MODEL),
        out_shape=jax.ShapeDtypeStruct(x2d.shape, F32),
        compiler_params=pltpu.CompilerParams(dimension_semantics=("arbitrary",),
                                             vmem_limit_bytes=VMEM_LIMIT),
        name="outproj_ffn2",
    )(x2d, oc, osl, ow, convn, ga, wout, gpost, gpre2, wg, wu, wd, gpost2)


def _prep_w_in(w_in):
    sizes = [D_ATTN] + [D_KV] * 6 + [N_GATES] + [D_CONV] * 3
    cuts = np.cumsum([0] + sizes)
    q, kc, vc, ks, vs, kw, vw, gt, bg, cg, xc = [w_in[:, cuts[i]:cuts[i + 1]] for i in range(len(sizes))]
    gt = gt.reshape(D_MODEL, N_GROUPS, HPG, 3).transpose(0, 1, 3, 2).reshape(D_MODEL, N_GROUPS, 3 * HPG)
    gt = jnp.pad(gt, ((0, 0), (0, 0), (0, LANES - 3 * HPG))).reshape(D_MODEL, GATE_COLS)
    return jnp.concatenate([q, kc, vc, ks, vs, kw, vw, gt, bg, cg, xc], axis=1).astype(BF16)


def _prep_compress(pe, w1, w2):
    half = CMP_BLOCK // 2
    eye = jnp.eye(N_GROUPS, dtype=F32)
    w1r = w1.reshape(2, half, HEAD_DIM, CMP_HIDDEN)
    w1big = jnp.einsum("ptdc,gh->ptgdhc", w1r, eye).reshape(2, half * D_KV, N_GROUPS * CMP_HIDDEN)
    w2big = jnp.einsum("cd,gh->gchd", w2, eye).reshape(N_GROUPS * CMP_HIDDEN, D_KV)
    pe_rows = jnp.broadcast_to(pe.reshape(2, half, 1, HEAD_DIM), (2, half, N_GROUPS, HEAD_DIM))
    return pe_rows.reshape(2, half * D_KV), w1big.astype(BF16), w2big.astype(BF16)


def _agg_matrix(nch, nsel, cw):
    agg_w = np.convolve(np.ones(SEL_RATIO), np.ones(CMP_BLOCK // CMP_STRIDE))
    a = np.zeros((nch, nsel), np.float32)
    for j in range(nsel):
        for o, wgt in enumerate(agg_w):
            c = SEL_RATIO * j + o - (CMP_BLOCK // CMP_STRIDE - 1)
            if 0 <= c < nch - 1:
                a[c, j] = wgt
    a = a.T.reshape(nsel, nch // cw, cw).transpose(1, 0, 2)
    return jnp.asarray(a, BF16)


def _rope_inv_freq_row():
    inv = ROPE_THETA ** (-np.arange(ROPE_HALF, dtype=np.float32) * 2.0 / ROPE_DIM)
    lane = np.arange(LANES) % HEAD_DIM
    row = np.where(lane < ROPE_DIM, inv[lane % ROPE_HALF], 0.0).astype(np.float32)
    return jnp.asarray(row.reshape(1, LANES))


def _forward(x, positions, p, *, tm, tq, tc, nbh, tqs, tks, tqc, cw, tbw):
    b, s, _ = x.shape
    depth = p["w_in"].shape[0]
    nch = s // CMP_STRIDE
    nsel = s // SEL_BLOCK
    cos, sin = _rope_call(positions.reshape(b, s, 1), _rope_inv_freq_row(), tm)
    agg = _agg_matrix(nch, nsel, cw)
    row = lambda v: v.reshape(1, -1)
    for l in range(depth):
        x2d = x.reshape(b * s, D_MODEL)
        x2d = _ffn_call(x2d, row(p["ffn1_norm_pre"][l]), p["ffn1_w_gate"][l].astype(BF16),
                        p["ffn1_w_up"][l].astype(BF16), p["ffn1_w_down"][l].astype(BF16),
                        row(p["ffn1_norm_post"][l]), tm)
        (q_raw, q_rot, kc_in, vc_in, ksaug, vs, kw, vw, gates, convn) = _inproj_call(
            x2d.reshape(b, s, D_MODEL), cos, sin, row(p["mix_norm_pre"][l]), _prep_w_in(p["w_in"][l]),
            p["conv_w"][l], row(p["conv_out_norm"][l]), tm, nbh)
        pek, w1k, w2k = _prep_compress(p["cmp_pe_k"][l], p["cmp_w1_k"][l], p["cmp_w2_k"][l])
        pev, w1v, w2v = _prep_compress(p["cmp_pe_v"][l], p["cmp_w1_v"][l], p["cmp_w2_v"][l])
        kcmp, vcmp = _compress_call(kc_in.reshape(b, nch, CMP_STRIDE * D_KV),
                                    vc_in.reshape(b, nch, CMP_STRIDE * D_KV),
                                    pek, w1k, w2k, pev, w1v, w2v, tc)
        o_cmp, bias = _cmp_call(q_raw, kcmp, vcmp, gates, agg, tqc, cw)
        o_slc = _slc_call(q_rot, bias, ksaug, vs, gates, tqs, tks, nbh)
        o_win = _win_call(q_rot, kw, vw, gates, tbw, tq)
        flat = lambda a: a.reshape(b * s, a.shape[-1])
        x2d = _out_call(x2d, flat(o_cmp), flat(o_slc), flat(o_win), flat(convn),
                        row(p["attn_out_norm"][l]), p["w_out"][l].astype(BF16), row(p["mix_norm_post"][l]),
                        row(p["ffn2_norm_pre"][l]), p["ffn2_w_gate"][l].astype(BF16),
                        p["ffn2_w_up"][l].astype(BF16), p["ffn2_w_down"][l].astype(BF16),
                        row(p["ffn2_norm_post"][l]), tm)
        x = x2d.reshape(b, s, D_MODEL)
    return x


def kernel(x, positions, ffn1_norm_pre, ffn1_w_gate, ffn1_w_up, ffn1_w_down, ffn1_norm_post, mix_norm_pre, w_in, cmp_pe_k, cmp_w1_k, cmp_w2_k, cmp_pe_v, cmp_w1_v, cmp_w2_v, conv_w, attn_out_norm, conv_out_norm, w_out, mix_norm_post, ffn2_norm_pre, ffn2_w_gate, ffn2_w_up, ffn2_w_down, ffn2_norm_post):
    params = dict(
        ffn1_norm_pre=ffn1_norm_pre, ffn1_w_gate=ffn1_w_gate, ffn1_w_up=ffn1_w_up, ffn1_w_down=ffn1_w_down,
        ffn1_norm_post=ffn1_norm_post, mix_norm_pre=mix_norm_pre, w_in=w_in,
        cmp_pe_k=cmp_pe_k, cmp_w1_k=cmp_w1_k, cmp_w2_k=cmp_w2_k,
        cmp_pe_v=cmp_pe_v, cmp_w1_v=cmp_w1_v, cmp_w2_v=cmp_w2_v,
        conv_w=conv_w, attn_out_norm=attn_out_norm, conv_out_norm=conv_out_norm, w_out=w_out,
        mix_norm_post=mix_norm_post, ffn2_norm_pre=ffn2_norm_pre, ffn2_w_gate=ffn2_w_gate,
        ffn2_w_up=ffn2_w_up, ffn2_w_down=ffn2_w_down, ffn2_norm_post=ffn2_norm_post)
    return _forward(x, positions, params, tm=512, tq=256, tc=256, nbh=128, tqs=512, tks=1024, tqc=1024, cw=256, tbw=1024)
```

```python
import functools
import math

import numpy as np
import jax
import jax.numpy as jnp
from jax import lax
from jax.experimental import pallas as pl
from jax.experimental.pallas import tpu as pltpu

D_MODEL = 1024
N_HEADS = 8
HEAD_DIM = 64
N_GROUPS = 2
HPG = N_HEADS // N_GROUPS
D_ATTN = N_HEADS * HEAD_DIM
D_KV = N_GROUPS * HEAD_DIM
D_CONV = D_MODEL - D_ATTN
CONV_WIDTH = 3
CMP_BLOCK = 32
CMP_STRIDE = 16
CMP_SHIFT = 4
CMP_HIDDEN = 256
SEL_BLOCK = 64
SEL_SHIFT = 6
SEL_RATIO = SEL_BLOCK // CMP_STRIDE
N_SELECT = 16
N_FORCED = 3
WINDOW = 512
ROPE_THETA = 500000.0
ROPE_DIM = HEAD_DIM // 4
ROPE_HALF = ROPE_DIM // 2
D_FF = 2816
N_GATES = 3 * N_HEADS
EPS = 1e-6
NEG_INF = -1e30
FORCE_SCORE = 1e9
SCALE = 1.0 / math.sqrt(HEAD_DIM)

LANES = 128
SUBLANES = 8
GATE_COLS = N_GROUPS * LANES
VMEM_LIMIT = 56 * 1024 * 1024
FF_CHUNKS = ((0, 768), (768, 1536), (1536, 2304), (2304, 2816))
N_SHIFT_COLS = 2
SAFE_SCORE_BOUND = 40.0
HEAD_SHIFT = 6
BF16_ROUND_UP = 1.0 + 2.0 ** -8

F32 = jnp.float32
BF16 = jnp.bfloat16


def _const_spec(shape):
    nd = len(shape)
    return pl.BlockSpec(shape, lambda *_: (0,) * nd, pipeline_mode=pl.Buffered(1))


def _rms(x, g):
    ms = jnp.mean(x * x, axis=-1, keepdims=True)
    return x * lax.rsqrt(ms + EPS) * g


def _ffn(x, g_pre, wg_ref, wu_ref, wd_ref, g_post):
    h = _rms(x, g_pre).astype(BF16)
    d = None
    for c0, c1 in FF_CHUNKS:
        gate = jnp.dot(h, wg_ref[:, c0:c1], preferred_element_type=F32)
        up = jnp.dot(h, wu_ref[:, c0:c1], preferred_element_type=F32)
        a = (gate * jax.nn.sigmoid(gate) * up).astype(BF16)
        part = jnp.dot(a, wd_ref[c0:c1, :], preferred_element_type=F32)
        d = part if d is None else d + part
    return x + 0.5 * _rms(d, g_post)


def _ffn_kernel(x_ref, gpre_ref, wg_ref, wu_ref, wd_ref, gpost_ref, o_ref):
    o_ref[...] = _ffn(x_ref[...], gpre_ref[...], wg_ref, wu_ref, wd_ref, gpost_ref[...])


def _ffn_call(x2d, g_pre, wg, wu, wd, g_post, tm):
    t = x2d.shape[0]
    row = pl.BlockSpec((tm, D_MODEL), lambda i: (i, 0))
    return pl.pallas_call(
        _ffn_kernel,
        grid=(t // tm,),
        in_specs=[row, _const_spec((1, D_MODEL)), _const_spec(wg.shape), _const_spec(wu.shape),
                  _const_spec(wd.shape), _const_spec((1, D_MODEL))],
        out_specs=row,
        out_shape=jax.ShapeDtypeStruct(x2d.shape, F32),
        compiler_params=pltpu.CompilerParams(dimension_semantics=("arbitrary",),
                                             vmem_limit_bytes=VMEM_LIMIT),
        name="ffn1",
    )(x2d, g_pre, wg, wu, wd, g_post)


_C_Q = 0
_C_KC = _C_Q + D_ATTN
_C_VC = _C_KC + D_KV
_C_KS = _C_VC + D_KV
_C_VS = _C_KS + D_KV
_C_KW = _C_VS + D_KV
_C_VW = _C_KW + D_KV
_C_GATE = _C_VW + D_KV
_C_BG = _C_GATE + GATE_COLS
_C_CG = _C_BG + D_CONV
_C_XC = _C_CG + D_CONV
_C_END = _C_XC + D_CONV


def _rope_kernel(pos_ref, invf_ref, cos_ref, sin_ref):
    ang = pos_ref[0].astype(F32) * invf_ref[...]
    cos_ref[0] = jnp.cos(ang)
    sin_ref[0] = jnp.sin(ang)


def _rope_call(pos3, invf, tm):
    b, s, _ = pos3.shape
    row = lambda width: pl.BlockSpec((1, tm, width), lambda bi, i: (bi, i, 0))
    shape = jax.ShapeDtypeStruct((b, s, LANES), F32)
    return pl.pallas_call(
        _rope_kernel,
        grid=(b, s // tm),
        in_specs=[row(1), _const_spec((1, LANES))],
        out_specs=[row(LANES), row(LANES)],
        out_shape=[shape, shape],
        compiler_params=pltpu.CompilerParams(dimension_semantics=("arbitrary", "arbitrary"),
                                             vmem_limit_bytes=VMEM_LIMIT),
        name="rope_tables",
    )(pos3, invf)


def _inproj_kernel(nbh, x_ref, cos_ref, sin_ref, g_ref, w_ref, convw_ref, convg_ref,
                   qraw_ref, qrot_ref, kc_ref, vc_ref, ksaug_ref, vs_ref, kw_ref, vw_ref,
                   gates_ref, convn_ref, ubuf_ref):
    i = pl.program_id(1)
    tm = x_ref.shape[1]
    h = _rms(x_ref[0], g_ref[...]).astype(BF16)

    def proj(c0, c1):
        return jnp.dot(h, w_ref[:, c0:c1], preferred_element_type=F32)

    cos = cos_ref[0]
    sin = sin_ref[0]
    lane = lax.broadcasted_iota(jnp.int32, (1, LANES), 1) & (HEAD_DIM - 1)
    s_lo = jnp.where(lane < ROPE_HALF, -sin, 0.0)
    s_hi = jnp.where(lane >= ROPE_HALF, sin, 0.0)

    def rope(z):
        return (z * cos + pltpu.roll(z, LANES - ROPE_HALF, 1) * s_lo
                + pltpu.roll(z, ROPE_HALF, 1) * s_hi)

    for c in range(0, D_ATTN // LANES, 2):
        zq2 = proj(_C_Q + c * LANES, _C_Q + (c + 2) * LANES)
        for cc in (c, c + 1):
            zq = zq2[:, (cc - c) * LANES:(cc - c + 1) * LANES]
            qraw_ref[0, :, cc * LANES:(cc + 1) * LANES] = zq.astype(BF16)
            qrot_ref[0, :, cc * LANES:(cc + 1) * LANES] = rope(zq).astype(BF16)

    kvc = proj(_C_KC, _C_KS)
    kc_ref[0] = kvc[:, 0:D_KV]
    vc_ref[0] = kvc[:, D_KV:2 * D_KV]
    kvs = proj(_C_KS, _C_KW)
    ks = rope(kvs[:, 0:D_KV]).astype(BF16)
    vs = kvs[:, D_KV:2 * D_KV].astype(BF16)
    kvw = proj(_C_KW, _C_GATE)
    kw = rope(kvw[:, 0:D_KV]).astype(BF16)
    vw = kvw[:, D_KV:2 * D_KV].astype(BF16)
    row_blk = ((i * tm + lax.broadcasted_iota(jnp.int32, (tm, nbh), 0)) >> SEL_SHIFT) & (nbh - 1)
    onehot = jnp.where(row_blk == lax.broadcasted_iota(jnp.int32, (tm, nbh), 1), 1.0, 0.0).astype(BF16)
    tail = jnp.where(lax.broadcasted_iota(jnp.int32, (tm, HEAD_DIM), 1) < N_SHIFT_COLS, 1.0, 0.0).astype(BF16)
    for g in range(N_GROUPS):
        sl = slice(g * HEAD_DIM, (g + 1) * HEAD_DIM)
        ksaug_ref[0, g, :, 0:nbh] = onehot
        ksaug_ref[0, g, :, nbh:nbh + HEAD_DIM] = ks[:, sl]
        ksaug_ref[0, g, :, nbh + HEAD_DIM:nbh + 2 * HEAD_DIM] = tail
        vs_ref[0, g, :, 0:HEAD_DIM] = vs[:, sl]
        vs_ref[0, g, :, HEAD_DIM:2 * HEAD_DIM] = tail
        kw_ref[0, g, :, 0:HEAD_DIM] = kw[:, sl]
        kw_ref[0, g, :, HEAD_DIM:2 * HEAD_DIM] = tail
        vw_ref[0, g, :, 0:HEAD_DIM] = vw[:, sl]
        vw_ref[0, g, :, HEAD_DIM:2 * HEAD_DIM] = tail

    gates_ref[0] = jax.nn.sigmoid(proj(_C_GATE, _C_BG))

    u = proj(_C_CG, _C_XC) * proj(_C_XC, _C_END)

    @pl.when(i == 0)
    def _():
        ubuf_ref[0:8, :] = jnp.zeros((8, D_CONV), F32)

    @pl.when(i > 0)
    def _():
        ubuf_ref[0:8, :] = ubuf_ref[tm:tm + 8, :]

    ubuf_ref[8:tm + 8, :] = u
    w = convw_ref[...]
    y = (w[2:3, :] * u + w[1:2, :] * ubuf_ref[7:tm + 7, :] + w[0:1, :] * ubuf_ref[6:tm + 6, :])
    conv = proj(_C_BG, _C_CG) * y
    convn_ref[0] = _rms(conv, convg_ref[...]).astype(BF16)


def _inproj_call(x, cos, sin, g_pre, w_in, conv_w, conv_g, tm, nbh):
    b, s, _ = x.shape
    grid = (b, s // tm)
    row = lambda width: pl.BlockSpec((1, tm, width), lambda bi, i: (bi, i, 0))
    grp = lambda width: pl.BlockSpec((1, N_GROUPS, tm, width), lambda bi, i: (bi, 0, i, 0))
    kern = functools.partial(_inproj_kernel, nbh)
    return pl.pallas_call(
        kern,
        grid=grid,
        in_specs=[row(D_MODEL), row(LANES), row(LANES), _const_spec((1, D_MODEL)), _const_spec(w_in.shape),
                  _const_spec((CONV_WIDTH, D_CONV)), _const_spec((1, D_CONV))],
        out_specs=[row(D_ATTN), row(D_ATTN), row(D_KV), row(D_KV), grp(nbh + 2 * HEAD_DIM), grp(2 * HEAD_DIM),
                   grp(2 * HEAD_DIM), grp(2 * HEAD_DIM), row(GATE_COLS), row(D_CONV)],
        out_shape=[
            jax.ShapeDtypeStruct((b, s, D_ATTN), BF16),
            jax.ShapeDtypeStruct((b, s, D_ATTN), BF16),
            jax.ShapeDtypeStruct((b, s, D_KV), F32),
            jax.ShapeDtypeStruct((b, s, D_KV), F32),
            jax.ShapeDtypeStruct((b, N_GROUPS, s, nbh + 2 * HEAD_DIM), BF16),
            jax.ShapeDtypeStruct((b, N_GROUPS, s, 2 * HEAD_DIM), BF16),
            jax.ShapeDtypeStruct((b, N_GROUPS, s, 2 * HEAD_DIM), BF16),
            jax.ShapeDtypeStruct((b, N_GROUPS, s, 2 * HEAD_DIM), BF16),
            jax.ShapeDtypeStruct((b, s, GATE_COLS), F32),
            jax.ShapeDtypeStruct((b, s, D_CONV), BF16),
        ],
        scratch_shapes=[pltpu.VMEM((tm + 8, D_CONV), F32)],
        compiler_params=pltpu.CompilerParams(dimension_semantics=("arbitrary", "arbitrary"),
                                             vmem_limit_bytes=VMEM_LIMIT),
        name="inproj",
    )(x, cos, sin, g_pre, w_in, conv_w, conv_g)


def _compress_kernel(kc_ref, kcn_ref, vc_ref, vcn_ref, pek_ref, w1k_ref, w2k_ref,
                     pev_ref, w1v_ref, w2v_ref, ko_ref, vo_ref):
    tc = kc_ref.shape[1]
    hid_w = N_GROUPS * CMP_HIDDEN
    last = lax.broadcasted_iota(jnp.int32, (tc, 1), 0) == tc - 1

    def one(x_ref, xn_ref, pe_ref, w1_ref, w2_ref, o_ref):
        x = x_ref[0]
        top = jnp.dot((x + pe_ref[0:1, :]).astype(BF16), w1_ref[0], preferred_element_type=F32)
        xb = (x + pe_ref[1:2, :]).astype(BF16)
        bot = jnp.dot(xb, w1_ref[1], preferred_element_type=F32)
        xnb = (xn_ref[0] + pe_ref[1:2, :]).astype(BF16)
        botn = jnp.dot(xnb, w1_ref[1], preferred_element_type=F32)
        shifted = jnp.where(last, botn[0:1, :], pltpu.roll(bot, tc - 1, 0))
        hid = jax.nn.gelu(top + shifted).astype(BF16)
        out = jnp.dot(hid, w2_ref[...], preferred_element_type=F32)
        tail = jnp.where(lax.broadcasted_iota(jnp.int32, (tc, HEAD_DIM), 1) < N_SHIFT_COLS, 1.0, 0.0).astype(BF16)
        for g in range(N_GROUPS):
            o_ref[0, g, :, 0:HEAD_DIM] = out[:, g * HEAD_DIM:(g + 1) * HEAD_DIM].astype(BF16)
            o_ref[0, g, :, HEAD_DIM:2 * HEAD_DIM] = tail

    one(kc_ref, kcn_ref, pek_ref, w1k_ref, w2k_ref, ko_ref)
    one(vc_ref, vcn_ref, pev_ref, w1v_ref, w2v_ref, vo_ref)


def _compress_call(kc_in, vc_in, pek, w1k, w2k, pev, w1v, w2v, tc):
    b, nch, width = kc_in.shape
    nt = nch // tc
    last8 = nch // 8 - 1
    cur = pl.BlockSpec((1, tc, width), lambda bi, i: (bi, i, 0))
    nxt = pl.BlockSpec((1, 8, width), lambda bi, i: (bi, jnp.minimum((i + 1) * (tc // 8), last8), 0))
    out = pl.BlockSpec((1, N_GROUPS, tc, 2 * HEAD_DIM), lambda bi, i: (bi, 0, i, 0))
    oshape = jax.ShapeDtypeStruct((b, N_GROUPS, nch, 2 * HEAD_DIM), BF16)
    return pl.pallas_call(
        _compress_kernel,
        grid=(b, nt),
        in_specs=[cur, nxt, cur, nxt, _const_spec(pek.shape), _const_spec(w1k.shape), _const_spec(w2k.shape),
                  _const_spec(pev.shape), _const_spec(w1v.shape), _const_spec(w2v.shape)],
        out_specs=[out, out],
        out_shape=[oshape, oshape],
        compiler_params=pltpu.CompilerParams(dimension_semantics=("arbitrary", "arbitrary"),
                                             vmem_limit_bytes=VMEM_LIMIT),
        name="compress",
    )(kc_in, kc_in, vc_in, vc_in, pek, w1k, w2k, pev, w1v, w2v)


def _split3(x):
    hi = x.astype(BF16)
    r = x - hi.astype(F32)
    mid = r.astype(BF16)
    lo = (r - mid.astype(F32)).astype(BF16)
    return hi, mid, lo


def _tile_score_bound(qs, kmax):
    qf = qs.astype(F32)
    width = HPG * HEAD_DIM
    heads = jnp.where((lax.broadcasted_iota(jnp.int32, (width, LANES), 0) >> HEAD_SHIFT)
                      == lax.broadcasted_iota(jnp.int32, (width, LANES), 1), 1.0, 0.0).astype(BF16)
    nrm2 = jnp.dot((qf * qf).astype(BF16), heads, preferred_element_type=F32)
    top = jnp.max(jnp.max(nrm2, axis=0, keepdims=True), axis=1, keepdims=True)
    return jnp.sqrt(top * BF16_ROUND_UP) * kmax


def _shift_cols(shift):
    hi = shift.astype(BF16).astype(F32)
    lo = (shift - hi).astype(BF16).astype(F32)
    lane = lax.broadcasted_iota(jnp.int32, (1, HEAD_DIM), 1)
    return jnp.where(lane == 0, -hi, jnp.where(lane == 1, -lo, 0.0)).astype(BF16)


def _cmp_kernel(k_top, cw, q_ref, kaug_ref, vaug_ref, gates_ref, aggt_ref, o_ref, bias_ref,
                qaug_ref, e_ref, acc_ref, kmax_ref, pslc_ref):
    qt = pl.program_id(2)
    tq = q_ref.shape[1]
    nch = kaug_ref.shape[2]
    nsel = bias_ref.shape[3]
    m_rows = HPG * tq
    t0 = qt * tq
    tcol = slice(HEAD_DIM, 2 * HEAD_DIM)
    c_last = ((t0 + tq - CMP_BLOCK) >> CMP_SHIFT) // cw
    c_mask = jnp.maximum(c_last - 1, 0)

    @pl.when(qt == 0)
    def _():
        k = kaug_ref[0, 0, :, 0:HEAD_DIM].astype(F32)
        n_ok = lax.broadcasted_iota(jnp.int32, (nch, 1), 0) < nch - 1
        ksq = jnp.where(n_ok, jnp.sum(k * k, axis=-1, keepdims=True), 0.0)
        kmax_ref[...] = jnp.broadcast_to(jnp.sqrt(jnp.max(ksq, axis=0, keepdims=True)), kmax_ref.shape)

    qs = q_ref[0] * SCALE
    for h in range(HPG):
        qaug_ref[h * tq:(h + 1) * tq, 0:HEAD_DIM] = qs[:, h * HEAD_DIM:(h + 1) * HEAD_DIM]
    bound = _tile_score_bound(qs, kmax_ref[0:1, 0:1])
    safe = jnp.max(bound) <= SAFE_SCORE_BOUND

    def scores(c, masked):
        ka = kaug_ref[0, 0, pl.ds(pl.multiple_of(c * cw, cw), cw), :]
        s = lax.dot_general(qaug_ref[...], ka, (((1,), (1,)), ((), ())), preferred_element_type=F32)
        if masked:
            row_t = t0 + (lax.broadcasted_iota(jnp.int32, (m_rows, 1), 0) & (tq - 1))
            n_vis = (row_t - (CMP_BLOCK - 1)) >> CMP_SHIFT
            n = c * cw + lax.broadcasted_iota(jnp.int32, (m_rows, cw), 1)
            s = jnp.where(n <= n_vis, s, NEG_INF)
        return s

    def sweep(fn_full, fn_masked, init):
        carry = lax.fori_loop(0, c_mask, fn_full, init)
        return lax.fori_loop(c_mask, c_last + 1, fn_masked, carry)

    @pl.when(safe)
    def _():
        qaug_ref[:, tcol] = jnp.broadcast_to(_shift_cols(bound), (m_rows, HEAD_DIM))

    @pl.when(jnp.logical_not(safe))
    def _():
        qaug_ref[:, tcol] = jnp.zeros((m_rows, HEAD_DIM), BF16)
        step = lambda masked: (lambda c, m: jnp.maximum(m, jnp.max(scores(c, masked), axis=-1, keepdims=True)))
        mx = sweep(step(False), step(True), jnp.full((m_rows, 1), NEG_INF, F32))
        qaug_ref[:, tcol] = _shift_cols(jnp.where(mx > 0.5 * NEG_INF, mx, 0.0))

    acc_ref[...] = jnp.zeros(acc_ref.shape, F32)

    def chunk(masked):
        def body(c, carry):
            e = jnp.exp(scores(c, masked))
            start = pl.multiple_of(c * cw, cw)
            e_ref[c] = e
            acc_ref[...] += jnp.dot(e.astype(BF16), vaug_ref[0, 0, pl.ds(start, cw), :],
                                    preferred_element_type=F32)
            return carry
        return body

    sweep(chunk(False), chunk(True), 0)
    acc = acc_ref[...]
    l = acc[:, HEAD_DIM:HEAD_DIM + 1]
    rinv = jnp.where(l > 0.0, 1.0 / l, 0.0)
    o = acc[:, 0:HEAD_DIM] * rinv
    gates = gates_ref[0]
    o_ref[0] = jnp.concatenate(
        [o[h * tq:(h + 1) * tq] * gates[:, h:h + 1] for h in range(HPG)], axis=-1)

    pslc_ref[...] = jnp.zeros(pslc_ref.shape, F32)

    def agg_body(c, carry):
        imp = sum(e_ref[c, h * tq:(h + 1) * tq, :] * rinv[h * tq:(h + 1) * tq] for h in range(HPG))
        at = aggt_ref[c]
        pslc_ref[...] += sum(lax.dot_general(at, part, (((1,), (1,)), ((), ())), preferred_element_type=F32)
                             for part in _split3(imp))
        return carry

    lax.fori_loop(0, c_last + 1, agg_body, 0)

    n_slabs = nsel // SUBLANES
    sub = lax.broadcasted_iota(jnp.int32, (SUBLANES, LANES), 0)
    sub_f = sub.astype(F32)
    no_slab = float(n_slabs)
    n_cols = tq // LANES

    def tree(fn, xs):
        while len(xs) > 1:
            xs = [fn(xs[i], xs[i + 1]) if i + 1 < len(xs) else xs[i] for i in range(0, len(xs), 2)]
        return xs[0]

    def first_slab(score, m):
        firsts = []
        for g0 in range(0, len(score), SUBLANES):
            slab = jnp.full((SUBLANES, LANES), no_slab, F32)
            for r in reversed(range(g0, min(g0 + SUBLANES, len(score)))):
                slab = jnp.where(score[r] == m, float(r), slab)
            firsts.append(slab)
        return tree(jnp.minimum, firsts)

    def select_blocks(n_act, group):
        for c0 in range(0, n_cols, group):
            cols = range(c0, min(c0 + group, n_cols))
            lanes = {c: slice(c * LANES, (c + 1) * LANES) for c in cols}
            cur = {c: (t0 + c * LANES + lax.broadcasted_iota(jnp.int32, (1, LANES), 1)) >> SEL_SHIFT
                   for c in cols}
            score = {}
            for c in cols:
                score[c] = []
                for r in range(n_act):
                    j = sub + r * SUBLANES
                    free = (j <= cur[c] - 2) & (j > 0)
                    score[c].append(jnp.where(
                        free, pslc_ref[r * SUBLANES:(r + 1) * SUBLANES, lanes[c]], -jnp.inf))
            for _ in range(max(k_top - N_FORCED, 0)):
                for c in cols:
                    m = jnp.max(tree(jnp.maximum, score[c]), axis=0, keepdims=True)
                    slab = first_slab(score[c], m)
                    block = slab * SUBLANES + sub_f
                    first = jnp.min(block, axis=0, keepdims=True)
                    taken = jnp.where(block == first, slab, -1.0)
                    score[c] = [jnp.where(taken == float(r), -jnp.inf, score[c][r]) for r in range(n_act)]
            for c in cols:
                for r in range(n_act):
                    pslc_ref[r * SUBLANES:(r + 1) * SUBLANES, lanes[c]] = jnp.where(
                        (sub + r * SUBLANES <= cur[c]) & (score[c][r] == -jnp.inf), 0.0, NEG_INF)
        if n_act < n_slabs:
            pslc_ref[n_act * SUBLANES:, :] = jnp.full(((n_slabs - n_act) * SUBLANES, tq), NEG_INF, F32)

    if n_slabs % 2 == 0:
        half = n_slabs // 2
        few = t0 + tq <= half * SUBLANES * SEL_BLOCK
        pl.when(few)(lambda: select_blocks(half, 2))
        pl.when(jnp.logical_not(few))(lambda: select_blocks(n_slabs, 1))
    else:
        select_blocks(n_slabs, 1)
    bias_ref[0, 0] = pslc_ref[...].T.astype(BF16)


def _cmp_call(q_raw, kaug, vaug, gates, aggt, tq, cw):
    b, s, _ = q_raw.shape
    nch = kaug.shape[2]
    nsel = aggt.shape[1]
    assert nch % cw == 0 and tq // CMP_STRIDE < cw
    k_top = min(N_SELECT, nsel)
    gw = HPG * HEAD_DIM
    m_rows = HPG * tq
    qspec = pl.BlockSpec((1, tq, gw), lambda bi, g, i: (bi, i, g))
    kvspec = pl.BlockSpec((1, 1, nch, 2 * HEAD_DIM), lambda bi, g, i: (bi, g, 0, 0))
    gspec = pl.BlockSpec((1, tq, LANES), lambda bi, g, i: (bi, i, g))
    return pl.pallas_call(
        functools.partial(_cmp_kernel, k_top, cw),
        grid=(b, N_GROUPS, s // tq),
        in_specs=[qspec, kvspec, kvspec, gspec, _const_spec(aggt.shape)],
        out_specs=[qspec, pl.BlockSpec((1, 1, tq, nsel), lambda bi, g, i: (bi, g, i, 0))],
        out_shape=[jax.ShapeDtypeStruct((b, s, D_ATTN), F32),
                   jax.ShapeDtypeStruct((b, N_GROUPS, s, nsel), BF16)],
        scratch_shapes=[pltpu.VMEM((m_rows, 2 * HEAD_DIM), BF16),
                        pltpu.VMEM((nch // cw, m_rows, cw), F32),
                        pltpu.VMEM((m_rows, 2 * HEAD_DIM), F32),
                        pltpu.VMEM((8, LANES), F32),
                        pltpu.VMEM((nsel, tq), F32)],
        compiler_params=pltpu.CompilerParams(dimension_semantics=("arbitrary",) * 3,
                                             vmem_limit_bytes=VMEM_LIMIT),
        name="cmp_attn_topk",
    )(q_raw, kaug, vaug, gates, aggt)


def _slc_kernel(nbh, tk, q_ref, bias_ref, kaug_ref, vaug_ref, gates_ref, o_ref,
                qaug_ref, acc_ref, kmax_ref):
    qt = pl.program_id(2)
    tq = q_ref.shape[1]
    m_rows = HPG * tq
    t0 = qt * tq
    n_wide = t0 // tk
    n_narrow = (t0 - n_wide * tk) // tq
    n_half = qaug_ref.shape[0]
    kcol = slice(nbh, nbh + HEAD_DIM)
    tcol = slice(nbh + HEAD_DIM, nbh + 2 * HEAD_DIM)

    @pl.when(qt == 0)
    def _():
        def body(c, mx):
            k = kaug_ref[0, 0, pl.ds(pl.multiple_of(c * tk, tk), tk), kcol].astype(F32)
            return jnp.maximum(mx, jnp.sum(k * k, axis=-1, keepdims=True))
        mx = lax.fori_loop(0, kaug_ref.shape[2] // tk, body, jnp.zeros((tk, 1), F32))
        kmax_ref[...] = jnp.broadcast_to(jnp.sqrt(jnp.max(mx, axis=0, keepdims=True)), kmax_ref.shape)

    qs = q_ref[0] * SCALE
    bias = bias_ref[0, 0]
    for h in range(HPG):
        rows = slice(h * tq, (h + 1) * tq)
        for hf in range(n_half):
            qaug_ref[hf, rows, 0:nbh] = bias[:, hf * nbh:(hf + 1) * nbh]
            qaug_ref[hf, rows, kcol] = qs[:, h * HEAD_DIM:(h + 1) * HEAD_DIM]
    bound = _tile_score_bound(qs, kmax_ref[0:1, 0:1])
    safe = jnp.max(bound) <= SAFE_SCORE_BOUND

    def scores(start, width, diagonal):
        ka = kaug_ref[0, 0, pl.ds(start, width), :]
        qa = qaug_ref[start // (nbh * SEL_BLOCK)]
        s = lax.dot_general(qa, ka, (((1,), (1,)), ((), ())), preferred_element_type=F32)
        if diagonal:
            r = lax.broadcasted_iota(jnp.int32, (m_rows, width), 0) & (tq - 1)
            c = lax.broadcasted_iota(jnp.int32, (m_rows, width), 1)
            s = jnp.where(c <= r, s, NEG_INF)
        return s

    def sweep(fn, carry):
        carry = lax.fori_loop(
            0, n_wide, lambda i, c: fn(pl.multiple_of(i * tk, tk), tk, False, c), carry)
        carry = lax.fori_loop(
            0, n_narrow, lambda i, c: fn(pl.multiple_of(n_wide * tk + i * tq, tq), tq, False, c), carry)
        return fn(pl.multiple_of(t0, tq), tq, True, carry)

    @pl.when(safe)
    def _():
        tail = jnp.broadcast_to(_shift_cols(bound), (m_rows, HEAD_DIM))
        for hf in range(n_half):
            qaug_ref[hf, :, tcol] = tail

    @pl.when(jnp.logical_not(safe))
    def _():
        for hf in range(n_half):
            qaug_ref[hf, :, tcol] = jnp.zeros((m_rows, HEAD_DIM), BF16)
        mx = sweep(lambda st, w, dg, m: jnp.maximum(m, jnp.max(scores(st, w, dg), axis=-1, keepdims=True)),
                   jnp.full((m_rows, 1), NEG_INF, F32))
        tail = _shift_cols(mx)
        for hf in range(n_half):
            qaug_ref[hf, :, tcol] = tail

    acc_ref[...] = jnp.zeros(acc_ref.shape, F32)

    def pv(start, width, diagonal):
        p = jnp.exp(scores(start, width, diagonal)).astype(BF16)
        return jnp.dot(p, vaug_ref[0, 0, pl.ds(start, width), :], preferred_element_type=F32)

    def tile(start, width, diagonal, carry):
        acc_ref[...] += pv(start, width, diagonal)
        return carry

    def wide_pair(i, carry):
        first = pl.multiple_of(2 * i * tk, tk)
        acc_ref[...] += pv(first, tk, False) + pv(pl.multiple_of(first + tk, tk), tk, False)
        return carry

    lax.fori_loop(0, n_wide // 2, wide_pair, 0)
    lax.fori_loop(2 * (n_wide // 2), n_wide, lambda i, c: tile(pl.multiple_of(i * tk, tk), tk, False, c), 0)
    lax.fori_loop(
        0, n_narrow, lambda i, c: tile(pl.multiple_of(n_wide * tk + i * tq, tq), tq, False, c), 0)
    tile(pl.multiple_of(t0, tq), tq, True, 0)
    acc = acc_ref[...]
    o = acc[:, 0:HEAD_DIM] / acc[:, HEAD_DIM:HEAD_DIM + 1]
    gates = gates_ref[0]
    o_ref[0] = jnp.concatenate(
        [o[h * tq:(h + 1) * tq] * gates[:, HPG + h:HPG + h + 1] for h in range(HPG)], axis=-1)


def _slc_call(q_rot, bias, kaug, v, gates, tq, tk, nbh):
    b, s, _ = q_rot.shape
    assert tk % tq == 0 and (nbh * SEL_BLOCK) % tk == 0 and s % tk == 0
    nsel = bias.shape[3]
    gw = HPG * HEAD_DIM
    aug = nbh + 2 * HEAD_DIM
    qspec = pl.BlockSpec((1, tq, gw), lambda bi, g, i: (bi, i, g))
    return pl.pallas_call(
        functools.partial(_slc_kernel, nbh, tk),
        grid=(b, N_GROUPS, s // tq),
        in_specs=[qspec,
                  pl.BlockSpec((1, 1, tq, nsel), lambda bi, g, i: (bi, g, i, 0)),
                  pl.BlockSpec((1, 1, s, aug), lambda bi, g, i: (bi, g, 0, 0)),
                  pl.BlockSpec((1, 1, s, 2 * HEAD_DIM), lambda bi, g, i: (bi, g, 0, 0)),
                  pl.BlockSpec((1, tq, LANES), lambda bi, g, i: (bi, i, g))],
        out_specs=qspec,
        out_shape=jax.ShapeDtypeStruct((b, s, D_ATTN), F32),
        scratch_shapes=[pltpu.VMEM((nsel // nbh, HPG * tq, aug), BF16),
                        pltpu.VMEM((HPG * tq, 2 * HEAD_DIM), F32),
                        pltpu.VMEM((8, LANES), F32)],
        compiler_params=pltpu.CompilerParams(dimension_semantics=("arbitrary",) * 3,
                                             vmem_limit_bytes=VMEM_LIMIT),
        name="slc_attn",
    )(q_rot, bias, kaug, v, gates)


def _win_kernel(tq, q_ref, kaug_ref, vaug_ref, gates_ref, o_ref, qaug_ref, kmax_ref):
    qb = pl.program_id(2)
    tb = q_ref.shape[1]
    n_sub = tb // tq
    span = WINDOW + tq
    m_rows = HPG * tq
    tcol = slice(HEAD_DIM, 2 * HEAD_DIM)

    @pl.when(qb == 0)
    def _():
        def body(c, mx):
            k = kaug_ref[0, 0, pl.ds(pl.multiple_of(c * tq, tq), tq), 0:HEAD_DIM].astype(F32)
            return jnp.maximum(mx, jnp.sum(k * k, axis=-1, keepdims=True))
        mx = lax.fori_loop(0, kaug_ref.shape[2] // tq, body, jnp.zeros((tq, 1), F32))
        kmax_ref[...] = jnp.broadcast_to(jnp.sqrt(jnp.max(mx, axis=0, keepdims=True)), kmax_ref.shape)

    qs = q_ref[0] * SCALE
    for j in range(n_sub):
        for h in range(HPG):
            qaug_ref[j, h * tq:(h + 1) * tq, 0:HEAD_DIM] = qs[j * tq:(j + 1) * tq, h * HEAD_DIM:(h + 1) * HEAD_DIM]
    bound = _tile_score_bound(qs, kmax_ref[0:1, 0:1])
    safe = jnp.max(bound) <= SAFE_SCORE_BOUND

    def band_start(j):
        return pl.multiple_of(jnp.maximum(qb * tb + j * tq - WINDOW, 0), tq)

    def scores(j):
        t0 = qb * tb + j * tq
        start = band_start(j)
        ka = kaug_ref[0, 0, pl.ds(start, span), :]
        s = lax.dot_general(qaug_ref[j], ka, (((1,), (1,)), ((), ())), preferred_element_type=F32)
        d = (lax.broadcasted_iota(jnp.int32, (m_rows, tq), 1)
             - (lax.broadcasted_iota(jnp.int32, (m_rows, tq), 0) & (tq - 1)))
        blocks = []
        for blk in range(span // tq):
            off = t0 - start - blk * tq
            ok = lax.bitcast_convert_type(off - d, jnp.uint32) < jnp.uint32(WINDOW)
            blocks.append(jnp.where(ok, s[:, blk * tq:(blk + 1) * tq], NEG_INF))
        return jnp.concatenate(blocks, axis=-1)

    @pl.when(safe)
    def _():
        for j in range(n_sub):
            qaug_ref[j, :, tcol] = jnp.broadcast_to(_shift_cols(bound), (m_rows, HEAD_DIM))

    @pl.when(jnp.logical_not(safe))
    def _():
        for j in range(n_sub):
            qaug_ref[j, :, tcol] = jnp.zeros((m_rows, HEAD_DIM), BF16)
            qaug_ref[j, :, tcol] = _shift_cols(jnp.max(scores(j), axis=-1, keepdims=True))

    gates = gates_ref[0]
    for j in range(n_sub):
        p = jnp.exp(scores(j)).astype(BF16)
        acc = jnp.dot(p, vaug_ref[0, 0, pl.ds(band_start(j), span), :], preferred_element_type=F32)
        o = acc[:, 0:HEAD_DIM] / acc[:, HEAD_DIM:HEAD_DIM + 1]
        gj = gates[j * tq:(j + 1) * tq]
        o_ref[0, j * tq:(j + 1) * tq, :] = jnp.concatenate(
            [o[h * tq:(h + 1) * tq] * gj[:, 2 * HPG + h:2 * HPG + h + 1] for h in range(HPG)], axis=-1)


def _win_call(q_rot, k, v, gates, tb, tq):
    b, s, _ = q_rot.shape
    assert tb % tq == 0 and WINDOW % tq == 0
    gw = HPG * HEAD_DIM
    qspec = pl.BlockSpec((1, tb, gw), lambda bi, g, i: (bi, i, g))
    kvspec = pl.BlockSpec((1, 1, s, 2 * HEAD_DIM), lambda bi, g, i: (bi, g, 0, 0))
    return pl.pallas_call(
        functools.partial(_win_kernel, tq),
        grid=(b, N_GROUPS, s // tb),
        in_specs=[qspec, kvspec, kvspec, pl.BlockSpec((1, tb, LANES), lambda bi, g, i: (bi, i, g))],
        out_specs=qspec,
        out_shape=jax.ShapeDtypeStruct((b, s, D_ATTN), F32),
        scratch_shapes=[pltpu.VMEM((tb // tq, HPG * tq, 2 * HEAD_DIM), BF16),
                        pltpu.VMEM((8, LANES), F32)],
        compiler_params=pltpu.CompilerParams(dimension_semantics=("arbitrary",) * 3,
                                             vmem_limit_bytes=VMEM_LIMIT),
        name="win_attn",
    )(q_rot, k, v, gates)


def _out_kernel(x_ref, oc_ref, os_ref, ow_ref, convn_ref, ga_ref, wout_ref, gpost_ref,
                gpre2_ref, wg_ref, wu_ref, wd_ref, gpost2_ref, o_ref):
    attn = oc_ref[...] + os_ref[...] + ow_ref[...]
    an = _rms(attn, ga_ref[...]).astype(BF16)
    h = (jnp.dot(an, wout_ref[0:D_ATTN, :], preferred_element_type=F32)
         + jnp.dot(convn_ref[...], wout_ref[D_ATTN:D_MODEL, :], preferred_element_type=F32))
    x1 = x_ref[...] + _rms(h, gpost_ref[...])
    o_ref[...] = _ffn(x1, gpre2_ref[...], wg_ref, wu_ref, wd_ref, gpost2_ref[...])


def _out_call(x2d, oc, osl, ow, convn, ga, wout, gpost, gpre2, wg, wu, wd, gpost2, tm):
    t = x2d.shape[0]
    row = lambda width: pl.BlockSpec((tm, width), lambda i: (i, 0))
    return pl.pallas_call(
        _out_kernel,
        grid=(t // tm,),
        in_specs=[row(D_MODEL), row(D_ATTN), row(D_ATTN), row(D_ATTN), row(D_CONV),
                  _const_spec((1, D_ATTN)), _const_spec(wout.shape), _const_spec((1, D_MODEL)),
                  _const_spec((1, D_MODEL)), _const_spec(wg.shape), _const_spec(wu.shape),
                  _const_spec(wd.shape), _const_spec((1, D_MODEL))],
        out_specs=row(D_MODEL),
        out_shape=jax.ShapeDtypeStruct(x2d.shape, F32),
        compiler_params=pltpu.CompilerParams(dimension_semantics=("arbitrary",),
                                             vmem_limit_bytes=VMEM_LIMIT),
        name="outproj_ffn2",
    )(x2d, oc, osl, ow, convn, ga, wout, gpost, gpre2, wg, wu, wd, gpost2)


def _prep_w_in(w_in):
    sizes = [D_ATTN] + [D_KV] * 6 + [N_GATES] + [D_CONV] * 3
    cuts = np.cumsum([0] + sizes)
    q, kc, vc, ks, vs, kw, vw, gt, bg, cg, xc = [w_in[:, cuts[i]:cuts[i + 1]] for i in range(len(sizes))]
    gt = gt.reshape(D_MODEL, N_GROUPS, HPG, 3).transpose(0, 1, 3, 2).reshape(D_MODEL, N_GROUPS, 3 * HPG)
    gt = jnp.pad(gt, ((0, 0), (0, 0), (0, LANES - 3 * HPG))).reshape(D_MODEL, GATE_COLS)
    return jnp.concatenate([q, kc, vc, ks, vs, kw, vw, gt, bg, cg, xc], axis=1).astype(BF16)


def _prep_compress(pe, w1, w2):
    half = CMP_BLOCK // 2
    eye = jnp.eye(N_GROUPS, dtype=F32)
    w1r = w1.reshape(2, half, HEAD_DIM, CMP_HIDDEN)
    w1big = jnp.einsum("ptdc,gh->ptgdhc", w1r, eye).reshape(2, half * D_KV, N_GROUPS * CMP_HIDDEN)
    w2big = jnp.einsum("cd,gh->gchd", w2, eye).reshape(N_GROUPS * CMP_HIDDEN, D_KV)
    pe_rows = jnp.broadcast_to(pe.reshape(2, half, 1, HEAD_DIM), (2, half, N_GROUPS, HEAD_DIM))
    return pe_rows.reshape(2, half * D_KV), w1big.astype(BF16), w2big.astype(BF16)


def _agg_matrix(nch, nsel, cw):
    agg_w = np.convolve(np.ones(SEL_RATIO), np.ones(CMP_BLOCK // CMP_STRIDE))
    a = np.zeros((nch, nsel), np.float32)
    for j in range(nsel):
        for o, wgt in enumerate(agg_w):
            c = SEL_RATIO * j + o - (CMP_BLOCK // CMP_STRIDE - 1)
            if 0 <= c < nch - 1:
                a[c, j] = wgt
    a = a.T.reshape(nsel, nch // cw, cw).transpose(1, 0, 2)
    return jnp.asarray(a, BF16)


def _rope_inv_freq_row():
    inv = ROPE_THETA ** (-np.arange(ROPE_HALF, dtype=np.float32) * 2.0 / ROPE_DIM)
    lane = np.arange(LANES) % HEAD_DIM
    row = np.where(lane < ROPE_DIM, inv[lane % ROPE_HALF], 0.0).astype(np.float32)
    return jnp.asarray(row.reshape(1, LANES))


def _forward(x, positions, p, *, tm, tq, tc, nbh, tqs, tks, tqc, cw, tbw):
    b, s, _ = x.shape
    depth = p["w_in"].shape[0]
    nch = s // CMP_STRIDE
    nsel = s // SEL_BLOCK
    cos, sin = _rope_call(positions.reshape(b, s, 1), _rope_inv_freq_row(), tm)
    agg = _agg_matrix(nch, nsel, cw)
    row = lambda v: v.reshape(1, -1)
    for l in range(depth):
        x2d = x.reshape(b * s, D_MODEL)
        x2d = _ffn_call(x2d, row(p["ffn1_norm_pre"][l]), p["ffn1_w_gate"][l].astype(BF16),
                        p["ffn1_w_up"][l].astype(BF16), p["ffn1_w_down"][l].astype(BF16),
                        row(p["ffn1_norm_post"][l]), tm)
        (q_raw, q_rot, kc_in, vc_in, ksaug, vs, kw, vw, gates, convn) = _inproj_call(
            x2d.reshape(b, s, D_MODEL), cos, sin, row(p["mix_norm_pre"][l]), _prep_w_in(p["w_in"][l]),
            p["conv_w"][l], row(p["conv_out_norm"][l]), tm, nbh)
        pek, w1k, w2k = _prep_compress(p["cmp_pe_k"][l], p["cmp_w1_k"][l], p["cmp_w2_k"][l])
        pev, w1v, w2v = _prep_compress(p["cmp_pe_v"][l], p["cmp_w1_v"][l], p["cmp_w2_v"][l])
        kcmp, vcmp = _compress_call(kc_in.reshape(b, nch, CMP_STRIDE * D_KV),
                                    vc_in.reshape(b, nch, CMP_STRIDE * D_KV),
                                    pek, w1k, w2k, pev, w1v, w2v, tc)
        o_cmp, bias = _cmp_call(q_raw, kcmp, vcmp, gates, agg, tqc, cw)
        o_slc = _slc_call(q_rot, bias, ksaug, vs, gates, tqs, tks, nbh)
        o_win = _win_call(q_rot, kw, vw, gates, tbw, tq)
        flat = lambda a: a.reshape(b * s, a.shape[-1])
        x2d = _out_call(x2d, flat(o_cmp), flat(o_slc), flat(o_win), flat(convn),
                        row(p["attn_out_norm"][l]), p["w_out"][l].astype(BF16), row(p["mix_norm_post"][l]),
                        row(p["ffn2_norm_pre"][l]), p["ffn2_w_gate"][l].astype(BF16),
                        p["ffn2_w_up"][l].astype(BF16), p["ffn2_w_down"][l].astype(BF16),
                        row(p["ffn2_norm_post"][l]), tm)
        x = x2d.reshape(b, s, D_MODEL)
    return x


def kernel(x, positions, ffn1_norm_pre, ffn1_w_gate, ffn1_w_up, ffn1_w_down, ffn1_norm_post, mix_norm_pre, w_in, cmp_pe_k, cmp_w1_k, cmp_w2_k, cmp_pe_v, cmp_w1_v, cmp_w2_v, conv_w, attn_out_norm, conv_out_norm, w_out, mix_norm_post, ffn2_norm_pre, ffn2_w_gate, ffn2_w_up, ffn2_w_down, ffn2_norm_post):
    params = dict(
        ffn1_norm_pre=ffn1_norm_pre, ffn1_w_gate=ffn1_w_gate, ffn1_w_up=ffn1_w_up, ffn1_w_down=ffn1_w_down,
        ffn1_norm_post=ffn1_norm_post, mix_norm_pre=mix_norm_pre, w_in=w_in,
        cmp_pe_k=cmp_pe_k, cmp_w1_k=cmp_w1_k, cmp_w2_k=cmp_w2_k,
        cmp_pe_v=cmp_pe_v, cmp_w1_v=cmp_w1_v, cmp_w2_v=cmp_w2_v,
        conv_w=conv_w, attn_out_norm=attn_out_norm, conv_out_norm=conv_out_norm, w_out=w_out,
        mix_norm_post=mix_norm_post, ffn2_norm_pre=ffn2_norm_pre, ffn2_w_gate=ffn2_w_gate,
        ffn2_w_up=ffn2_w_up, ffn2_w_down=ffn2_w_down, ffn2_norm_post=ffn2_norm_post)
    return _forward(x, positions, params, tm=512, tq=256, tc=256, nbh=128, tqs=512, tks=1024, tqc=1024, cw=256, tbw=1024)
```

```python
import functools
import math

import numpy as np
import jax
import jax.numpy as jnp
from jax import lax
from jax.experimental import pallas as pl
from jax.experimental.pallas import tpu as pltpu

D_MODEL = 1024
N_HEADS = 8
HEAD_DIM = 64
N_GROUPS = 2
HPG = N_HEADS // N_GROUPS
D_ATTN = N_HEADS * HEAD_DIM
D_KV = N_GROUPS * HEAD_DIM
D_CONV = D_MODEL - D_ATTN
CONV_WIDTH = 3
CMP_BLOCK = 32
CMP_STRIDE = 16
CMP_SHIFT = 4
CMP_HIDDEN = 256
SEL_BLOCK = 64
SEL_SHIFT = 6
SEL_RATIO = SEL_BLOCK // CMP_STRIDE
N_SELECT = 16
N_FORCED = 3
WINDOW = 512
ROPE_THETA = 500000.0
ROPE_DIM = HEAD_DIM // 4
ROPE_HALF = ROPE_DIM // 2
D_FF = 2816
N_GATES = 3 * N_HEADS
EPS = 1e-6
NEG_INF = -1e30
FORCE_SCORE = 1e9
SCALE = 1.0 / math.sqrt(HEAD_DIM)

LANES = 128
SUBLANES = 8
GATE_COLS = N_GROUPS * LANES
VMEM_LIMIT = 56 * 1024 * 1024
FF_CHUNKS = ((0, 768), (768, 1536), (1536, 2304), (2304, 2816))
N_SHIFT_COLS = 2
SAFE_SCORE_BOUND = 40.0
HEAD_SHIFT = 6
BF16_ROUND_UP = 1.0 + 2.0 ** -8

F32 = jnp.float32
BF16 = jnp.bfloat16


def _const_spec(shape):
    nd = len(shape)
    return pl.BlockSpec(shape, lambda *_: (0,) * nd, pipeline_mode=pl.Buffered(1))


def _rms(x, g):
    ms = jnp.mean(x * x, axis=-1, keepdims=True)
    return x * lax.rsqrt(ms + EPS) * g


def _ffn(x, g_pre, wg_ref, wu_ref, wd_ref, g_post):
    h = _rms(x, g_pre).astype(BF16)
    d = None
    for c0, c1 in FF_CHUNKS:
        gate = jnp.dot(h, wg_ref[:, c0:c1], preferred_element_type=F32)
        up = jnp.dot(h, wu_ref[:, c0:c1], preferred_element_type=F32)
        a = (gate * jax.nn.sigmoid(gate) * up).astype(BF16)
        part = jnp.dot(a, wd_ref[c0:c1, :], preferred_element_type=F32)
        d = part if d is None else d + part
    return x + 0.5 * _rms(d, g_post)


def _ffn_kernel(x_ref, gpre_ref, wg_ref, wu_ref, wd_ref, gpost_ref, o_ref):
    o_ref[0] = _ffn(x_ref[0], gpre_ref[...], wg_ref, wu_ref, wd_ref, gpost_ref[...])


def _ffn_call(x, g_pre, wg, wu, wd, g_post, tm):
    b, s, _ = x.shape
    row = pl.BlockSpec((1, tm, D_MODEL), lambda bi, i: (bi, i, 0))
    return pl.pallas_call(
        _ffn_kernel,
        grid=(b, s // tm),
        in_specs=[row, _const_spec((1, D_MODEL)), _const_spec(wg.shape), _const_spec(wu.shape),
                  _const_spec(wd.shape), _const_spec((1, D_MODEL))],
        out_specs=row,
        out_shape=jax.ShapeDtypeStruct(x.shape, F32),
        compiler_params=pltpu.CompilerParams(dimension_semantics=("arbitrary", "arbitrary"),
                                             vmem_limit_bytes=VMEM_LIMIT),
        name="ffn1",
    )(x, g_pre, wg, wu, wd, g_post)


_C_Q = 0
_C_KC = _C_Q + D_ATTN
_C_VC = _C_KC + D_KV
_C_KS = _C_VC + D_KV
_C_VS = _C_KS + D_KV
_C_KW = _C_VS + D_KV
_C_VW = _C_KW + D_KV
_C_GATE = _C_VW + D_KV
_C_BG = _C_GATE + GATE_COLS
_C_CG = _C_BG + D_CONV
_C_XC = _C_CG + D_CONV
_C_END = _C_XC + D_CONV


def _rope_kernel(pos_ref, invf_ref, cos_ref, sin_ref):
    ang = pos_ref[0].astype(F32) * invf_ref[...]
    cos_ref[0] = jnp.cos(ang)
    sin_ref[0] = jnp.sin(ang)


def _rope_call(pos3, invf, tm):
    b, s, _ = pos3.shape
    row = lambda width: pl.BlockSpec((1, tm, width), lambda bi, i: (bi, i, 0))
    shape = jax.ShapeDtypeStruct((b, s, LANES), F32)
    return pl.pallas_call(
        _rope_kernel,
        grid=(b, s // tm),
        in_specs=[row(1), _const_spec((1, LANES))],
        out_specs=[row(LANES), row(LANES)],
        out_shape=[shape, shape],
        compiler_params=pltpu.CompilerParams(dimension_semantics=("arbitrary", "arbitrary"),
                                             vmem_limit_bytes=VMEM_LIMIT),
        name="rope_tables",
    )(pos3, invf)


def _inproj_kernel(nbh, x_ref, cos_ref, sin_ref, g_ref, w_ref, convw_ref, convg_ref,
                   qraw_ref, qrot_ref, kc_ref, vc_ref, ksaug_ref, vs_ref, kw_ref, vw_ref,
                   gates_ref, convn_ref, ubuf_ref):
    i = pl.program_id(1)
    tm = x_ref.shape[1]
    h = _rms(x_ref[0], g_ref[...]).astype(BF16)

    def proj(c0, c1):
        return jnp.dot(h, w_ref[:, c0:c1], preferred_element_type=F32)

    cos = cos_ref[0]
    sin = sin_ref[0]
    lane = lax.broadcasted_iota(jnp.int32, (1, LANES), 1) & (HEAD_DIM - 1)
    s_lo = jnp.where(lane < ROPE_HALF, -sin, 0.0)
    s_hi = jnp.where(lane >= ROPE_HALF, sin, 0.0)

    def rope(z):
        return (z * cos + pltpu.roll(z, LANES - ROPE_HALF, 1) * s_lo
                + pltpu.roll(z, ROPE_HALF, 1) * s_hi)

    for c in range(0, D_ATTN // LANES, 2):
        zq2 = proj(_C_Q + c * LANES, _C_Q + (c + 2) * LANES)
        for cc in (c, c + 1):
            zq = zq2[:, (cc - c) * LANES:(cc - c + 1) * LANES]
            qraw_ref[0, :, cc * LANES:(cc + 1) * LANES] = zq.astype(BF16)
            qrot_ref[0, :, cc * LANES:(cc + 1) * LANES] = rope(zq).astype(BF16)

    kvc = proj(_C_KC, _C_KS)
    kc_ref[0] = kvc[:, 0:D_KV].reshape(tm // CMP_STRIDE, CMP_STRIDE, D_KV)
    vc_ref[0] = kvc[:, D_KV:2 * D_KV].reshape(tm // CMP_STRIDE, CMP_STRIDE, D_KV)
    kvs = proj(_C_KS, _C_KW)
    ks = rope(kvs[:, 0:D_KV]).astype(BF16)
    vs = kvs[:, D_KV:2 * D_KV].astype(BF16)
    kvw = proj(_C_KW, _C_GATE)
    kw = rope(kvw[:, 0:D_KV]).astype(BF16)
    vw = kvw[:, D_KV:2 * D_KV].astype(BF16)
    row_blk = ((i * tm + lax.broadcasted_iota(jnp.int32, (tm, nbh), 0)) >> SEL_SHIFT) & (nbh - 1)
    onehot = jnp.where(row_blk == lax.broadcasted_iota(jnp.int32, (tm, nbh), 1), 1.0, 0.0).astype(BF16)
    tail = jnp.where(lax.broadcasted_iota(jnp.int32, (tm, HEAD_DIM), 1) < N_SHIFT_COLS, 1.0, 0.0).astype(BF16)
    for g in range(N_GROUPS):
        sl = slice(g * HEAD_DIM, (g + 1) * HEAD_DIM)
        ksaug_ref[0, g, :, 0:nbh] = onehot
        ksaug_ref[0, g, :, nbh:nbh + HEAD_DIM] = ks[:, sl]
        ksaug_ref[0, g, :, nbh + HEAD_DIM:nbh + 2 * HEAD_DIM] = tail
        vs_ref[0, g, :, 0:HEAD_DIM] = vs[:, sl]
        vs_ref[0, g, :, HEAD_DIM:2 * HEAD_DIM] = tail
        kw_ref[0, g, :, 0:HEAD_DIM] = kw[:, sl]
        kw_ref[0, g, :, HEAD_DIM:2 * HEAD_DIM] = tail
        vw_ref[0, g, :, 0:HEAD_DIM] = vw[:, sl]
        vw_ref[0, g, :, HEAD_DIM:2 * HEAD_DIM] = tail

    gates_ref[0] = jax.nn.sigmoid(proj(_C_GATE, _C_BG))

    u = proj(_C_CG, _C_XC) * proj(_C_XC, _C_END)

    @pl.when(i == 0)
    def _():
        ubuf_ref[0:8, :] = jnp.zeros((8, D_CONV), F32)

    @pl.when(i > 0)
    def _():
        ubuf_ref[0:8, :] = ubuf_ref[tm:tm + 8, :]

    ubuf_ref[8:tm + 8, :] = u
    w = convw_ref[...]
    y = (w[2:3, :] * u + w[1:2, :] * ubuf_ref[7:tm + 7, :] + w[0:1, :] * ubuf_ref[6:tm + 6, :])
    conv = proj(_C_BG, _C_CG) * y
    convn_ref[0] = _rms(conv, convg_ref[...]).astype(BF16)


def _inproj_call(x, cos, sin, g_pre, w_in, conv_w, conv_g, tm, nbh):
    b, s, _ = x.shape
    grid = (b, s // tm)
    row = lambda width: pl.BlockSpec((1, tm, width), lambda bi, i: (bi, i, 0))
    grp = lambda width: pl.BlockSpec((1, N_GROUPS, tm, width), lambda bi, i: (bi, 0, i, 0))
    chunked = pl.BlockSpec((1, tm // CMP_STRIDE, CMP_STRIDE, D_KV), lambda bi, i: (bi, i, 0, 0))
    kern = functools.partial(_inproj_kernel, nbh)
    return pl.pallas_call(
        kern,
        grid=grid,
        in_specs=[row(D_MODEL), row(LANES), row(LANES), _const_spec((1, D_MODEL)), _const_spec(w_in.shape),
                  _const_spec((CONV_WIDTH, D_CONV)), _const_spec((1, D_CONV))],
        out_specs=[row(D_ATTN), row(D_ATTN), chunked, chunked, grp(nbh + 2 * HEAD_DIM), grp(2 * HEAD_DIM),
                   grp(2 * HEAD_DIM), grp(2 * HEAD_DIM), row(GATE_COLS), row(D_CONV)],
        out_shape=[
            jax.ShapeDtypeStruct((b, s, D_ATTN), BF16),
            jax.ShapeDtypeStruct((b, s, D_ATTN), BF16),
            jax.ShapeDtypeStruct((b, s // CMP_STRIDE, CMP_STRIDE, D_KV), F32),
            jax.ShapeDtypeStruct((b, s // CMP_STRIDE, CMP_STRIDE, D_KV), F32),
            jax.ShapeDtypeStruct((b, N_GROUPS, s, nbh + 2 * HEAD_DIM), BF16),
            jax.ShapeDtypeStruct((b, N_GROUPS, s, 2 * HEAD_DIM), BF16),
            jax.ShapeDtypeStruct((b, N_GROUPS, s, 2 * HEAD_DIM), BF16),
            jax.ShapeDtypeStruct((b, N_GROUPS, s, 2 * HEAD_DIM), BF16),
            jax.ShapeDtypeStruct((b, s, GATE_COLS), F32),
            jax.ShapeDtypeStruct((b, s, D_CONV), BF16),
        ],
        scratch_shapes=[pltpu.VMEM((tm + 8, D_CONV), F32)],
        compiler_params=pltpu.CompilerParams(dimension_semantics=("arbitrary", "arbitrary"),
                                             vmem_limit_bytes=VMEM_LIMIT),
        name="inproj",
    )(x, cos, sin, g_pre, w_in, conv_w, conv_g)


def _compress_kernel(kc_ref, kcn_ref, vc_ref, vcn_ref, pek_ref, w1k_ref, w2k_ref,
                     pev_ref, w1v_ref, w2v_ref, ko_ref, vo_ref):
    tc = kc_ref.shape[1]
    hid_w = N_GROUPS * CMP_HIDDEN
    last = lax.broadcasted_iota(jnp.int32, (tc, 1), 0) == tc - 1

    def one(x_ref, xn_ref, pe_ref, w1_ref, w2_ref, o_ref):
        def half_mlp(ref, half):
            acc = None
            for j in range(0, CMP_STRIDE, 2):
                cols = slice(j * D_KV, (j + 2) * D_KV)
                xj = jnp.concatenate([ref[0, :, j, :], ref[0, :, j + 1, :]], axis=-1)
                part = jnp.dot((xj + pe_ref[half:half + 1, cols]).astype(BF16), w1_ref[half, cols, :],
                               preferred_element_type=F32)
                acc = part if acc is None else acc + part
            return acc

        top = half_mlp(x_ref, 0)
        bot = half_mlp(x_ref, 1)
        botn = half_mlp(xn_ref, 1)
        shifted = jnp.where(last, botn[0:1, :], pltpu.roll(bot, tc - 1, 0))
        hid = jax.nn.gelu(top + shifted).astype(BF16)
        out = jnp.dot(hid, w2_ref[...], preferred_element_type=F32)
        tail = jnp.where(lax.broadcasted_iota(jnp.int32, (tc, HEAD_DIM), 1) < N_SHIFT_COLS, 1.0, 0.0).astype(BF16)
        for g in range(N_GROUPS):
            o_ref[0, g, :, 0:HEAD_DIM] = out[:, g * HEAD_DIM:(g + 1) * HEAD_DIM].astype(BF16)
            o_ref[0, g, :, HEAD_DIM:2 * HEAD_DIM] = tail

    one(kc_ref, kcn_ref, pek_ref, w1k_ref, w2k_ref, ko_ref)
    one(vc_ref, vcn_ref, pev_ref, w1v_ref, w2v_ref, vo_ref)


def _compress_call(kc_in, vc_in, pek, w1k, w2k, pev, w1v, w2v, tc):
    b, nch, _, width = kc_in.shape
    nt = nch // tc
    last8 = nch // 8 - 1
    cur = pl.BlockSpec((1, tc, CMP_STRIDE, width), lambda bi, i: (bi, i, 0, 0))
    nxt = pl.BlockSpec((1, 8, CMP_STRIDE, width),
                       lambda bi, i: (bi, jnp.minimum((i + 1) * (tc // 8), last8), 0, 0))
    out = pl.BlockSpec((1, N_GROUPS, tc, 2 * HEAD_DIM), lambda bi, i: (bi, 0, i, 0))
    oshape = jax.ShapeDtypeStruct((b, N_GROUPS, nch, 2 * HEAD_DIM), BF16)
    return pl.pallas_call(
        _compress_kernel,
        grid=(b, nt),
        in_specs=[cur, nxt, cur, nxt, _const_spec(pek.shape), _const_spec(w1k.shape), _const_spec(w2k.shape),
                  _const_spec(pev.shape), _const_spec(w1v.shape), _const_spec(w2v.shape)],
        out_specs=[out, out],
        out_shape=[oshape, oshape],
        compiler_params=pltpu.CompilerParams(dimension_semantics=("arbitrary", "arbitrary"),
                                             vmem_limit_bytes=VMEM_LIMIT),
        name="compress",
    )(kc_in, kc_in, vc_in, vc_in, pek, w1k, w2k, pev, w1v, w2v)


def _split3(x):
    hi = x.astype(BF16)
    r = x - hi.astype(F32)
    mid = r.astype(BF16)
    lo = (r - mid.astype(F32)).astype(BF16)
    return hi, mid, lo


def _tile_score_bound(qs, kmax):
    qf = qs.astype(F32)
    width = HPG * HEAD_DIM
    heads = jnp.where((lax.broadcasted_iota(jnp.int32, (width, LANES), 0) >> HEAD_SHIFT)
                      == lax.broadcasted_iota(jnp.int32, (width, LANES), 1), 1.0, 0.0).astype(BF16)
    nrm2 = jnp.dot((qf * qf).astype(BF16), heads, preferred_element_type=F32)
    top = jnp.max(jnp.max(nrm2, axis=0, keepdims=True), axis=1, keepdims=True)
    return jnp.sqrt(top * BF16_ROUND_UP) * kmax


def _shift_cols(shift):
    hi = shift.astype(BF16).astype(F32)
    lo = (shift - hi).astype(BF16).astype(F32)
    lane = lax.broadcasted_iota(jnp.int32, (1, HEAD_DIM), 1)
    return jnp.where(lane == 0, -hi, jnp.where(lane == 1, -lo, 0.0)).astype(BF16)


def _cmp_kernel(k_top, cw, q_ref, kaug_ref, vaug_ref, gates_ref, aggt_ref, o_ref, bias_ref,
                qaug_ref, e_ref, acc_ref, kmax_ref, pslc_ref):
    qt = pl.program_id(2)
    tq = q_ref.shape[1]
    nch = kaug_ref.shape[2]
    nsel = bias_ref.shape[3]
    m_rows = HPG * tq
    t0 = qt * tq
    tcol = slice(HEAD_DIM, 2 * HEAD_DIM)
    c_last = ((t0 + tq - CMP_BLOCK) >> CMP_SHIFT) // cw
    c_mask = jnp.maximum(c_last - 1, 0)

    @pl.when(qt == 0)
    def _():
        k = kaug_ref[0, 0, :, 0:HEAD_DIM].astype(F32)
        n_ok = lax.broadcasted_iota(jnp.int32, (nch, 1), 0) < nch - 1
        ksq = jnp.where(n_ok, jnp.sum(k * k, axis=-1, keepdims=True), 0.0)
        kmax_ref[...] = jnp.broadcast_to(jnp.sqrt(jnp.max(ksq, axis=0, keepdims=True)), kmax_ref.shape)

    qs = q_ref[0] * SCALE
    for h in range(HPG):
        qaug_ref[h * tq:(h + 1) * tq, 0:HEAD_DIM] = qs[:, h * HEAD_DIM:(h + 1) * HEAD_DIM]
    bound = _tile_score_bound(qs, kmax_ref[0:1, 0:1])
    safe = jnp.max(bound) <= SAFE_SCORE_BOUND

    def scores(c, masked):
        ka = kaug_ref[0, 0, pl.ds(pl.multiple_of(c * cw, cw), cw), :]
        s = lax.dot_general(qaug_ref[...], ka, (((1,), (1,)), ((), ())), preferred_element_type=F32)
        if masked:
            row_t = t0 + (lax.broadcasted_iota(jnp.int32, (m_rows, 1), 0) & (tq - 1))
            n_vis = (row_t - (CMP_BLOCK - 1)) >> CMP_SHIFT
            n = c * cw + lax.broadcasted_iota(jnp.int32, (m_rows, cw), 1)
            s = jnp.where(n <= n_vis, s, NEG_INF)
        return s

    def sweep(fn_full, fn_masked, init):
        carry = lax.fori_loop(0, c_mask, fn_full, init)
        return lax.fori_loop(c_mask, c_last + 1, fn_masked, carry)

    @pl.when(safe)
    def _():
        qaug_ref[:, tcol] = jnp.broadcast_to(_shift_cols(bound), (m_rows, HEAD_DIM))

    @pl.when(jnp.logical_not(safe))
    def _():
        qaug_ref[:, tcol] = jnp.zeros((m_rows, HEAD_DIM), BF16)
        step = lambda masked: (lambda c, m: jnp.maximum(m, jnp.max(scores(c, masked), axis=-1, keepdims=True)))
        mx = sweep(step(False), step(True), jnp.full((m_rows, 1), NEG_INF, F32))
        qaug_ref[:, tcol] = _shift_cols(jnp.where(mx > 0.5 * NEG_INF, mx, 0.0))

    acc_ref[...] = jnp.zeros(acc_ref.shape, F32)

    def chunk(masked):
        def body(c, carry):
            e = jnp.exp(scores(c, masked))
            start = pl.multiple_of(c * cw, cw)
            e_ref[c] = e
            acc_ref[...] += jnp.dot(e.astype(BF16), vaug_ref[0, 0, pl.ds(start, cw), :],
                                    preferred_element_type=F32)
            return carry
        return body

    sweep(chunk(False), chunk(True), 0)
    acc = acc_ref[...]
    l = acc[:, HEAD_DIM:HEAD_DIM + 1]
    rinv = jnp.where(l > 0.0, 1.0 / l, 0.0)
    o = acc[:, 0:HEAD_DIM] * rinv
    gates = gates_ref[0]
    o_ref[0] = jnp.concatenate(
        [o[h * tq:(h + 1) * tq] * gates[:, h:h + 1] for h in range(HPG)], axis=-1)

    pslc_ref[...] = jnp.zeros(pslc_ref.shape, F32)

    def agg_body(c, carry):
        imp = sum(e_ref[c, h * tq:(h + 1) * tq, :] * rinv[h * tq:(h + 1) * tq] for h in range(HPG))
        at = aggt_ref[c]
        pslc_ref[...] += sum(lax.dot_general(at, part, (((1,), (1,)), ((), ())), preferred_element_type=F32)
                             for part in _split3(imp))
        return carry

    lax.fori_loop(0, c_last + 1, agg_body, 0)

    n_slabs = nsel // SUBLANES
    sub = lax.broadcasted_iota(jnp.int32, (SUBLANES, LANES), 0)
    sub_f = sub.astype(F32)
    no_slab = float(n_slabs)
    n_cols = tq // LANES

    def tree(fn, xs):
        while len(xs) > 1:
            xs = [fn(xs[i], xs[i + 1]) if i + 1 < len(xs) else xs[i] for i in range(0, len(xs), 2)]
        return xs[0]

    def first_slab(score, m):
        firsts = []
        for g0 in range(0, len(score), SUBLANES):
            slab = jnp.full((SUBLANES, LANES), no_slab, F32)
            for r in reversed(range(g0, min(g0 + SUBLANES, len(score)))):
                slab = jnp.where(score[r] == m, float(r), slab)
            firsts.append(slab)
        return tree(jnp.minimum, firsts)

    def select_blocks(n_act, group):
        for c0 in range(0, n_cols, group):
            cols = range(c0, min(c0 + group, n_cols))
            lanes = {c: slice(c * LANES, (c + 1) * LANES) for c in cols}
            cur = {c: (t0 + c * LANES + lax.broadcasted_iota(jnp.int32, (1, LANES), 1)) >> SEL_SHIFT
                   for c in cols}
            score = {}
            for c in cols:
                score[c] = []
                for r in range(n_act):
                    j = sub + r * SUBLANES
                    free = (j <= cur[c] - 2) & (j > 0)
                    score[c].append(jnp.where(
                        free, pslc_ref[r * SUBLANES:(r + 1) * SUBLANES, lanes[c]], -jnp.inf))
            for _ in range(max(k_top - N_FORCED, 0)):
                for c in cols:
                    m = jnp.max(tree(jnp.maximum, score[c]), axis=0, keepdims=True)
                    slab = first_slab(score[c], m)
                    block = slab * SUBLANES + sub_f
                    first = jnp.min(block, axis=0, keepdims=True)
                    taken = jnp.where(block == first, slab, -1.0)
                    score[c] = [jnp.where(taken == float(r), -jnp.inf, score[c][r]) for r in range(n_act)]
            for c in cols:
                for r in range(n_act):
                    pslc_ref[r * SUBLANES:(r + 1) * SUBLANES, lanes[c]] = jnp.where(
                        (sub + r * SUBLANES <= cur[c]) & (score[c][r] == -jnp.inf), 0.0, NEG_INF)
        if n_act < n_slabs:
            pslc_ref[n_act * SUBLANES:, :] = jnp.full(((n_slabs - n_act) * SUBLANES, tq), NEG_INF, F32)

    if n_slabs % 2 == 0:
        half = n_slabs // 2
        few = t0 + tq <= half * SUBLANES * SEL_BLOCK
        pl.when(few)(lambda: select_blocks(half, 2))
        pl.when(jnp.logical_not(few))(lambda: select_blocks(n_slabs, 1))
    else:
        select_blocks(n_slabs, 1)
    bias_ref[0, 0] = pslc_ref[...].T.astype(BF16)


def _cmp_call(q_raw, kaug, vaug, gates, aggt, tq, cw):
    b, s, _ = q_raw.shape
    nch = kaug.shape[2]
    nsel = aggt.shape[1]
    assert nch % cw == 0 and tq // CMP_STRIDE < cw
    k_top = min(N_SELECT, nsel)
    gw = HPG * HEAD_DIM
    m_rows = HPG * tq
    qspec = pl.BlockSpec((1, tq, gw), lambda bi, g, i: (bi, i, g))
    kvspec = pl.BlockSpec((1, 1, nch, 2 * HEAD_DIM), lambda bi, g, i: (bi, g, 0, 0))
    gspec = pl.BlockSpec((1, tq, LANES), lambda bi, g, i: (bi, i, g))
    return pl.pallas_call(
        functools.partial(_cmp_kernel, k_top, cw),
        grid=(b, N_GROUPS, s // tq),
        in_specs=[qspec, kvspec, kvspec, gspec, _const_spec(aggt.shape)],
        out_specs=[qspec, pl.BlockSpec((1, 1, tq, nsel), lambda bi, g, i: (bi, g, i, 0))],
        out_shape=[jax.ShapeDtypeStruct((b, s, D_ATTN), F32),
                   jax.ShapeDtypeStruct((b, N_GROUPS, s, nsel), BF16)],
        scratch_shapes=[pltpu.VMEM((m_rows, 2 * HEAD_DIM), BF16),
                        pltpu.VMEM((nch // cw, m_rows, cw), F32),
                        pltpu.VMEM((m_rows, 2 * HEAD_DIM), F32),
                        pltpu.VMEM((8, LANES), F32),
                        pltpu.VMEM((nsel, tq), F32)],
        compiler_params=pltpu.CompilerParams(dimension_semantics=("arbitrary",) * 3,
                                             vmem_limit_bytes=VMEM_LIMIT),
        name="cmp_attn_topk",
    )(q_raw, kaug, vaug, gates, aggt)


def _win_partials(tw, qt, qs, kaug_ref, vaug_ref, qaug_ref, kmax_ref):
    tq = qs.shape[0]
    n_sub = tq // tw
    span = WINDOW + tw
    m_rows = HPG * tw
    tcol = slice(HEAD_DIM, 2 * HEAD_DIM)

    @pl.when(qt == 0)
    def _():
        def body(c, mx):
            k = kaug_ref[0, 0, pl.ds(pl.multiple_of(c * tw, tw), tw), 0:HEAD_DIM].astype(F32)
            return jnp.maximum(mx, jnp.sum(k * k, axis=-1, keepdims=True))
        mx = lax.fori_loop(0, kaug_ref.shape[2] // tw, body, jnp.zeros((tw, 1), F32))
        kmax_ref[...] = jnp.broadcast_to(jnp.sqrt(jnp.max(mx, axis=0, keepdims=True)), kmax_ref.shape)

    for j in range(n_sub):
        for h in range(HPG):
            qaug_ref[j, h * tw:(h + 1) * tw, 0:HEAD_DIM] = qs[j * tw:(j + 1) * tw, h * HEAD_DIM:(h + 1) * HEAD_DIM]
    bound = _tile_score_bound(qs, kmax_ref[0:1, 0:1])
    safe = jnp.max(bound) <= SAFE_SCORE_BOUND

    def band_start(j):
        return pl.multiple_of(jnp.maximum(qt * tq + j * tw - WINDOW, 0), tw)

    def scores(j):
        t0 = qt * tq + j * tw
        start = band_start(j)
        ka = kaug_ref[0, 0, pl.ds(start, span), :]
        s = lax.dot_general(qaug_ref[j], ka, (((1,), (1,)), ((), ())), preferred_element_type=F32)
        d = (lax.broadcasted_iota(jnp.int32, (m_rows, tw), 1)
             - (lax.broadcasted_iota(jnp.int32, (m_rows, tw), 0) & (tw - 1)))
        blocks = []
        for blk in range(span // tw):
            off = t0 - start - blk * tw
            ok = lax.bitcast_convert_type(off - d, jnp.uint32) < jnp.uint32(WINDOW)
            blocks.append(jnp.where(ok, s[:, blk * tw:(blk + 1) * tw], NEG_INF))
        return jnp.concatenate(blocks, axis=-1)

    @pl.when(safe)
    def _():
        for j in range(n_sub):
            qaug_ref[j, :, tcol] = jnp.broadcast_to(_shift_cols(bound), (m_rows, HEAD_DIM))

    @pl.when(jnp.logical_not(safe))
    def _():
        for j in range(n_sub):
            qaug_ref[j, :, tcol] = jnp.zeros((m_rows, HEAD_DIM), BF16)
            qaug_ref[j, :, tcol] = _shift_cols(jnp.max(scores(j), axis=-1, keepdims=True))

    out = []
    for j in range(n_sub):
        p = jnp.exp(scores(j)).astype(BF16)
        out.append(jnp.dot(p, vaug_ref[0, 0, pl.ds(band_start(j), span), :], preferred_element_type=F32))
    return out


def _slc_kernel(nbh, tk, tw, q_ref, bias_ref, kaug_ref, vaug_ref, kwaug_ref, vwaug_ref, gates_ref, o_ref,
                qaug_ref, acc_ref, kmax_ref, qaugw_ref, kmaxw_ref):
    qt = pl.program_id(2)
    tq = q_ref.shape[1]
    m_rows = HPG * tq
    t0 = qt * tq
    n_wide = t0 // tk
    n_narrow = (t0 - n_wide * tk) // tq
    n_half = qaug_ref.shape[0]
    kcol = slice(nbh, nbh + HEAD_DIM)
    tcol = slice(nbh + HEAD_DIM, nbh + 2 * HEAD_DIM)

    @pl.when(qt == 0)
    def _():
        def body(c, mx):
            k = kaug_ref[0, 0, pl.ds(pl.multiple_of(c * tk, tk), tk), kcol].astype(F32)
            return jnp.maximum(mx, jnp.sum(k * k, axis=-1, keepdims=True))
        mx = lax.fori_loop(0, kaug_ref.shape[2] // tk, body, jnp.zeros((tk, 1), F32))
        kmax_ref[...] = jnp.broadcast_to(jnp.sqrt(jnp.max(mx, axis=0, keepdims=True)), kmax_ref.shape)

    qs = q_ref[0] * SCALE
    bias = bias_ref[0, 0]
    for h in range(HPG):
        rows = slice(h * tq, (h + 1) * tq)
        for hf in range(n_half):
            qaug_ref[hf, rows, 0:nbh] = bias[:, hf * nbh:(hf + 1) * nbh]
            qaug_ref[hf, rows, kcol] = qs[:, h * HEAD_DIM:(h + 1) * HEAD_DIM]
    bound = _tile_score_bound(qs, kmax_ref[0:1, 0:1])
    safe = jnp.max(bound) <= SAFE_SCORE_BOUND

    def scores(start, width, diagonal):
        ka = kaug_ref[0, 0, pl.ds(start, width), :]
        qa = qaug_ref[start // (nbh * SEL_BLOCK)]
        s = lax.dot_general(qa, ka, (((1,), (1,)), ((), ())), preferred_element_type=F32)
        if diagonal:
            r = lax.broadcasted_iota(jnp.int32, (m_rows, width), 0) & (tq - 1)
            c = lax.broadcasted_iota(jnp.int32, (m_rows, width), 1)
            s = jnp.where(c <= r, s, NEG_INF)
        return s

    def sweep(fn, carry):
        carry = lax.fori_loop(
            0, n_wide, lambda i, c: fn(pl.multiple_of(i * tk, tk), tk, False, c), carry)
        carry = lax.fori_loop(
            0, n_narrow, lambda i, c: fn(pl.multiple_of(n_wide * tk + i * tq, tq), tq, False, c), carry)
        return fn(pl.multiple_of(t0, tq), tq, True, carry)

    @pl.when(safe)
    def _():
        tail = jnp.broadcast_to(_shift_cols(bound), (m_rows, HEAD_DIM))
        for hf in range(n_half):
            qaug_ref[hf, :, tcol] = tail

    @pl.when(jnp.logical_not(safe))
    def _():
        for hf in range(n_half):
            qaug_ref[hf, :, tcol] = jnp.zeros((m_rows, HEAD_DIM), BF16)
        mx = sweep(lambda st, w, dg, m: jnp.maximum(m, jnp.max(scores(st, w, dg), axis=-1, keepdims=True)),
                   jnp.full((m_rows, 1), NEG_INF, F32))
        tail = _shift_cols(mx)
        for hf in range(n_half):
            qaug_ref[hf, :, tcol] = tail

    acc_ref[...] = jnp.zeros(acc_ref.shape, F32)

    def pv(start, width, diagonal):
        p = jnp.exp(scores(start, width, diagonal)).astype(BF16)
        return jnp.dot(p, vaug_ref[0, 0, pl.ds(start, width), :], preferred_element_type=F32)

    def tile(start, width, diagonal, carry):
        acc_ref[...] += pv(start, width, diagonal)
        return carry

    def wide_pair(i, carry):
        first = pl.multiple_of(2 * i * tk, tk)
        acc_ref[...] += pv(first, tk, False) + pv(pl.multiple_of(first + tk, tk), tk, False)
        return carry

    lax.fori_loop(0, n_wide // 2, wide_pair, 0)
    lax.fori_loop(2 * (n_wide // 2), n_wide, lambda i, c: tile(pl.multiple_of(i * tk, tk), tk, False, c), 0)
    lax.fori_loop(
        0, n_narrow, lambda i, c: tile(pl.multiple_of(n_wide * tk + i * tq, tq), tq, False, c), 0)
    tile(pl.multiple_of(t0, tq), tq, True, 0)
    wins = _win_partials(tw, qt, qs, kwaug_ref, vwaug_ref, qaugw_ref, kmaxw_ref)
    acc = acc_ref[...]
    o = acc[:, 0:HEAD_DIM] / acc[:, HEAD_DIM:HEAD_DIM + 1]
    gates = gates_ref[0]
    for j, accw in enumerate(wins):
        ow = accw[:, 0:HEAD_DIM] / accw[:, HEAD_DIM:HEAD_DIM + 1]
        gj = gates[j * tw:(j + 1) * tw]
        o_ref[0, j * tw:(j + 1) * tw, :] = jnp.concatenate(
            [o[h * tq + j * tw:h * tq + (j + 1) * tw] * gj[:, HPG + h:HPG + h + 1]
             + ow[h * tw:(h + 1) * tw] * gj[:, 2 * HPG + h:2 * HPG + h + 1] for h in range(HPG)], axis=-1)


def _slc_call(q_rot, bias, kaug, v, kw, vw, gates, tq, tk, tw, nbh):
    b, s, _ = q_rot.shape
    assert tk % tq == 0 and (nbh * SEL_BLOCK) % tk == 0 and s % tk == 0
    assert tq % tw == 0 and WINDOW % tw == 0
    nsel = bias.shape[3]
    gw = HPG * HEAD_DIM
    aug = nbh + 2 * HEAD_DIM
    qspec = pl.BlockSpec((1, tq, gw), lambda bi, g, i: (bi, i, g))
    kv = lambda width: pl.BlockSpec((1, 1, s, width), lambda bi, g, i: (bi, g, 0, 0),
                                    pipeline_mode=pl.Buffered(1))
    return pl.pallas_call(
        functools.partial(_slc_kernel, nbh, tk, tw),
        grid=(b, N_GROUPS, s // tq),
        in_specs=[qspec,
                  pl.BlockSpec((1, 1, tq, nsel), lambda bi, g, i: (bi, g, i, 0)),
                  kv(aug), kv(2 * HEAD_DIM), kv(2 * HEAD_DIM), kv(2 * HEAD_DIM),
                  pl.BlockSpec((1, tq, LANES), lambda bi, g, i: (bi, i, g))],
        out_specs=qspec,
        out_shape=jax.ShapeDtypeStruct((b, s, D_ATTN), F32),
        scratch_shapes=[pltpu.VMEM((nsel // nbh, HPG * tq, aug), BF16),
                        pltpu.VMEM((HPG * tq, 2 * HEAD_DIM), F32),
                        pltpu.VMEM((8, LANES), F32),
                        pltpu.VMEM((tq // tw, HPG * tw, 2 * HEAD_DIM), BF16),
                        pltpu.VMEM((8, LANES), F32)],
        compiler_params=pltpu.CompilerParams(dimension_semantics=("arbitrary",) * 3,
                                             vmem_limit_bytes=VMEM_LIMIT),
        name="slc_win_attn",
    )(q_rot, bias, kaug, v, kw, vw, gates)


def _win_kernel(tq, q_ref, kaug_ref, vaug_ref, gates_ref, o_ref, qaug_ref, kmax_ref):
    qb = pl.program_id(2)
    tb = q_ref.shape[1]
    n_sub = tb // tq
    span = WINDOW + tq
    m_rows = HPG * tq
    tcol = slice(HEAD_DIM, 2 * HEAD_DIM)

    @pl.when(qb == 0)
    def _():
        def body(c, mx):
            k = kaug_ref[0, 0, pl.ds(pl.multiple_of(c * tq, tq), tq), 0:HEAD_DIM].astype(F32)
            return jnp.maximum(mx, jnp.sum(k * k, axis=-1, keepdims=True))
        mx = lax.fori_loop(0, kaug_ref.shape[2] // tq, body, jnp.zeros((tq, 1), F32))
        kmax_ref[...] = jnp.broadcast_to(jnp.sqrt(jnp.max(mx, axis=0, keepdims=True)), kmax_ref.shape)

    qs = q_ref[0] * SCALE
    for j in range(n_sub):
        for h in range(HPG):
            qaug_ref[j, h * tq:(h + 1) * tq, 0:HEAD_DIM] = qs[j * tq:(j + 1) * tq, h * HEAD_DIM:(h + 1) * HEAD_DIM]
    bound = _tile_score_bound(qs, kmax_ref[0:1, 0:1])
    safe = jnp.max(bound) <= SAFE_SCORE_BOUND

    def band_start(j):
        return pl.multiple_of(jnp.maximum(qb * tb + j * tq - WINDOW, 0), tq)

    def scores(j):
        t0 = qb * tb + j * tq
        start = band_start(j)
        ka = kaug_ref[0, 0, pl.ds(start, span), :]
        s = lax.dot_general(qaug_ref[j], ka, (((1,), (1,)), ((), ())), preferred_element_type=F32)
        d = (lax.broadcasted_iota(jnp.int32, (m_rows, tq), 1)
             - (lax.broadcasted_iota(jnp.int32, (m_rows, tq), 0) & (tq - 1)))
        blocks = []
        for blk in range(span // tq):
            off = t0 - start - blk * tq
            ok = lax.bitcast_convert_type(off - d, jnp.uint32) < jnp.uint32(WINDOW)
            blocks.append(jnp.where(ok, s[:, blk * tq:(blk + 1) * tq], NEG_INF))
        return jnp.concatenate(blocks, axis=-1)

    @pl.when(safe)
    def _():
        for j in range(n_sub):
            qaug_ref[j, :, tcol] = jnp.broadcast_to(_shift_cols(bound), (m_rows, HEAD_DIM))

    @pl.when(jnp.logical_not(safe))
    def _():
        for j in range(n_sub):
            qaug_ref[j, :, tcol] = jnp.zeros((m_rows, HEAD_DIM), BF16)
            qaug_ref[j, :, tcol] = _shift_cols(jnp.max(scores(j), axis=-1, keepdims=True))

    gates = gates_ref[0]
    for j in range(n_sub):
        p = jnp.exp(scores(j)).astype(BF16)
        acc = jnp.dot(p, vaug_ref[0, 0, pl.ds(band_start(j), span), :], preferred_element_type=F32)
        o = acc[:, 0:HEAD_DIM] / acc[:, HEAD_DIM:HEAD_DIM + 1]
        gj = gates[j * tq:(j + 1) * tq]
        o_ref[0, j * tq:(j + 1) * tq, :] = jnp.concatenate(
            [o[h * tq:(h + 1) * tq] * gj[:, 2 * HPG + h:2 * HPG + h + 1] for h in range(HPG)], axis=-1)


def _win_call(q_rot, k, v, gates, tb, tq):
    b, s, _ = q_rot.shape
    assert tb % tq == 0 and WINDOW % tq == 0
    gw = HPG * HEAD_DIM
    qspec = pl.BlockSpec((1, tb, gw), lambda bi, g, i: (bi, i, g))
    kvspec = pl.BlockSpec((1, 1, s, 2 * HEAD_DIM), lambda bi, g, i: (bi, g, 0, 0))
    return pl.pallas_call(
        functools.partial(_win_kernel, tq),
        grid=(b, N_GROUPS, s // tb),
        in_specs=[qspec, kvspec, kvspec, pl.BlockSpec((1, tb, LANES), lambda bi, g, i: (bi, i, g))],
        out_specs=qspec,
        out_shape=jax.ShapeDtypeStruct((b, s, D_ATTN), F32),
        scratch_shapes=[pltpu.VMEM((tb // tq, HPG * tq, 2 * HEAD_DIM), BF16),
                        pltpu.VMEM((8, LANES), F32)],
        compiler_params=pltpu.CompilerParams(dimension_semantics=("arbitrary",) * 3,
                                             vmem_limit_bytes=VMEM_LIMIT),
        name="win_attn",
    )(q_rot, k, v, gates)


def _out_kernel(x_ref, oc_ref, osw_ref, convn_ref, ga_ref, wout_ref, gpost_ref,
                gpre2_ref, wg_ref, wu_ref, wd_ref, gpost2_ref, o_ref):
    attn = oc_ref[0] + osw_ref[0]
    an = _rms(attn, ga_ref[...]).astype(BF16)
    h = (jnp.dot(an, wout_ref[0:D_ATTN, :], preferred_element_type=F32)
         + jnp.dot(convn_ref[0], wout_ref[D_ATTN:D_MODEL, :], preferred_element_type=F32))
    x1 = x_ref[0] + _rms(h, gpost_ref[...])
    o_ref[0] = _ffn(x1, gpre2_ref[...], wg_ref, wu_ref, wd_ref, gpost2_ref[...])


def _out_call(x, oc, osw, convn, ga, wout, gpost, gpre2, wg, wu, wd, gpost2, tm):
    b, s, _ = x.shape
    row = lambda width: pl.BlockSpec((1, tm, width), lambda bi, i: (bi, i, 0))
    return pl.pallas_call(
        _out_kernel,
        grid=(b, s // tm),
        in_specs=[row(D_MODEL), row(D_ATTN), row(D_ATTN), row(D_CONV),
                  _const_spec((1, D_ATTN)), _const_spec(wout.shape), _const_spec((1, D_MODEL)),
                  _const_spec((1, D_MODEL)), _const_spec(wg.shape), _const_spec(wu.shape),
                  _const_spec(wd.shape), _const_spec((1, D_MODEL))],
        out_specs=row(D_MODEL),
        out_shape=jax.ShapeDtypeStruct(x.shape, F32),
        compiler_params=pltpu.CompilerParams(dimension_semantics=("arbitrary", "arbitrary"),
                                             vmem_limit_bytes=VMEM_LIMIT),
        name="outproj_ffn2",
    )(x, oc, osw, convn, ga, wout, gpost, gpre2, wg, wu, wd, gpost2)


def _prep_w_in(w_in):
    sizes = [D_ATTN] + [D_KV] * 6 + [N_GATES] + [D_CONV] * 3
    cuts = np.cumsum([0] + sizes)
    q, kc, vc, ks, vs, kw, vw, gt, bg, cg, xc = [w_in[:, cuts[i]:cuts[i + 1]] for i in range(len(sizes))]
    gt = gt.reshape(D_MODEL, N_GROUPS, HPG, 3).transpose(0, 1, 3, 2).reshape(D_MODEL, N_GROUPS, 3 * HPG)
    gt = jnp.pad(gt, ((0, 0), (0, 0), (0, LANES - 3 * HPG))).reshape(D_MODEL, GATE_COLS)
    return jnp.concatenate([q, kc, vc, ks, vs, kw, vw, gt, bg, cg, xc], axis=1).astype(BF16)


def _prep_compress(pe, w1, w2):
    half = CMP_BLOCK // 2
    eye = jnp.eye(N_GROUPS, dtype=F32)
    w1r = w1.reshape(2, half, HEAD_DIM, CMP_HIDDEN)
    w1big = jnp.einsum("ptdc,gh->ptgdhc", w1r, eye).reshape(2, half * D_KV, N_GROUPS * CMP_HIDDEN)
    w2big = jnp.einsum("cd,gh->gchd", w2, eye).reshape(N_GROUPS * CMP_HIDDEN, D_KV)
    pe_rows = jnp.broadcast_to(pe.reshape(2, half, 1, HEAD_DIM), (2, half, N_GROUPS, HEAD_DIM))
    return pe_rows.reshape(2, half * D_KV), w1big.astype(BF16), w2big.astype(BF16)


def _agg_matrix(nch, nsel, cw):
    agg_w = np.convolve(np.ones(SEL_RATIO), np.ones(CMP_BLOCK // CMP_STRIDE))
    a = np.zeros((nch, nsel), np.float32)
    for j in range(nsel):
        for o, wgt in enumerate(agg_w):
            c = SEL_RATIO * j + o - (CMP_BLOCK // CMP_STRIDE - 1)
            if 0 <= c < nch - 1:
                a[c, j] = wgt
    a = a.T.reshape(nsel, nch // cw, cw).transpose(1, 0, 2)
    return jnp.asarray(a, BF16)


def _rope_inv_freq_row():
    inv = ROPE_THETA ** (-np.arange(ROPE_HALF, dtype=np.float32) * 2.0 / ROPE_DIM)
    lane = np.arange(LANES) % HEAD_DIM
    row = np.where(lane < ROPE_DIM, inv[lane % ROPE_HALF], 0.0).astype(np.float32)
    return jnp.asarray(row.reshape(1, LANES))


def _forward(x, positions, p, *, tm, tq, tc, nbh, tqs, tks, tqc, cw, tbw):
    b, s, _ = x.shape
    depth = p["w_in"].shape[0]
    nch = s // CMP_STRIDE
    nsel = s // SEL_BLOCK
    cos, sin = _rope_call(positions.reshape(b, s, 1), _rope_inv_freq_row(), tm)
    agg = _agg_matrix(nch, nsel, cw)
    row = lambda v: v.reshape(1, -1)
    for l in range(depth):
        x = _ffn_call(x, row(p["ffn1_norm_pre"][l]), p["ffn1_w_gate"][l].astype(BF16),
                      p["ffn1_w_up"][l].astype(BF16), p["ffn1_w_down"][l].astype(BF16),
                      row(p["ffn1_norm_post"][l]), tm)
        (q_raw, q_rot, kc_in, vc_in, ksaug, vs, kw, vw, gates, convn) = _inproj_call(
            x, cos, sin, row(p["mix_norm_pre"][l]), _prep_w_in(p["w_in"][l]),
            p["conv_w"][l], row(p["conv_out_norm"][l]), tm, nbh)
        pek, w1k, w2k = _prep_compress(p["cmp_pe_k"][l], p["cmp_w1_k"][l], p["cmp_w2_k"][l])
        pev, w1v, w2v = _prep_compress(p["cmp_pe_v"][l], p["cmp_w1_v"][l], p["cmp_w2_v"][l])
        kcmp, vcmp = _compress_call(kc_in, vc_in, pek, w1k, w2k, pev, w1v, w2v, tc)
        o_cmp, bias = _cmp_call(q_raw, kcmp, vcmp, gates, agg, tqc, cw)
        o_sw = _slc_call(q_rot, bias, ksaug, vs, kw, vw, gates, tqs, tks, tq, nbh)
        x = _out_call(x, o_cmp, o_sw, convn,
                      row(p["attn_out_norm"][l]), p["w_out"][l].astype(BF16), row(p["mix_norm_post"][l]),
                      row(p["ffn2_norm_pre"][l]), p["ffn2_w_gate"][l].astype(BF16),
                      p["ffn2_w_up"][l].astype(BF16), p["ffn2_w_down"][l].astype(BF16),
                      row(p["ffn2_norm_post"][l]), tm)
    return x


def kernel(x, positions, ffn1_norm_pre, ffn1_w_gate, ffn1_w_up, ffn1_w_down, ffn1_norm_post, mix_norm_pre, w_in, cmp_pe_k, cmp_w1_k, cmp_w2_k, cmp_pe_v, cmp_w1_v, cmp_w2_v, conv_w, attn_out_norm, conv_out_norm, w_out, mix_norm_post, ffn2_norm_pre, ffn2_w_gate, ffn2_w_up, ffn2_w_down, ffn2_norm_post):
    params = dict(
        ffn1_norm_pre=ffn1_norm_pre, ffn1_w_gate=ffn1_w_gate, ffn1_w_up=ffn1_w_up, ffn1_w_down=ffn1_w_down,
        ffn1_norm_post=ffn1_norm_post, mix_norm_pre=mix_norm_pre, w_in=w_in,
        cmp_pe_k=cmp_pe_k, cmp_w1_k=cmp_w1_k, cmp_w2_k=cmp_w2_k,
        cmp_pe_v=cmp_pe_v, cmp_w1_v=cmp_w1_v, cmp_w2_v=cmp_w2_v,
        conv_w=conv_w, attn_out_norm=attn_out_norm, conv_out_norm=conv_out_norm, w_out=w_out,
        mix_norm_post=mix_norm_post, ffn2_norm_pre=ffn2_norm_pre, ffn2_w_gate=ffn2_w_gate,
        ffn2_w_up=ffn2_w_up, ffn2_w_down=ffn2_w_down, ffn2_norm_post=ffn2_norm_post)
    return _forward(x, positions, params, tm=512, tq=256, tc=256, nbh=128, tqs=512, tks=1024, tqc=1024, cw=256, tbw=1024)
```

```python
import functools
import math

import numpy as np
import jax
import jax.numpy as jnp
from jax import lax
from jax.experimental import pallas as pl
from jax.experimental.pallas import tpu as pltpu

D_MODEL = 1024
N_HEADS = 8
HEAD_DIM = 64
N_GROUPS = 2
HPG = N_HEADS // N_GROUPS
D_ATTN = N_HEADS * HEAD_DIM
D_KV = N_GROUPS * HEAD_DIM
D_CONV = D_MODEL - D_ATTN
CONV_WIDTH = 3
CMP_BLOCK = 32
CMP_STRIDE = 16
CMP_SHIFT = 4
CMP_HIDDEN = 256
SEL_BLOCK = 64
SEL_SHIFT = 6
SEL_RATIO = SEL_BLOCK // CMP_STRIDE
N_SELECT = 16
N_FORCED = 3
WINDOW = 512
ROPE_THETA = 500000.0
ROPE_DIM = HEAD_DIM // 4
ROPE_HALF = ROPE_DIM // 2
D_FF = 2816
N_GATES = 3 * N_HEADS
EPS = 1e-6
NEG_INF = -1e30
FORCE_SCORE = 1e9
SCALE = 1.0 / math.sqrt(HEAD_DIM)

LANES = 128
SUBLANES = 8
GATE_COLS = N_GROUPS * LANES
VMEM_LIMIT = 56 * 1024 * 1024
FF_CHUNKS = ((0, 768), (768, 1536), (1536, 2304), (2304, 2816))
N_SHIFT_COLS = 2
SAFE_SCORE_BOUND = 40.0
HEAD_SHIFT = 6
BF16_ROUND_UP = 1.0 + 2.0 ** -8

F32 = jnp.float32
BF16 = jnp.bfloat16


def _const_spec(shape):
    nd = len(shape)
    return pl.BlockSpec(shape, lambda *_: (0,) * nd, pipeline_mode=pl.Buffered(1))


def _rms(x, g):
    ms = jnp.mean(x * x, axis=-1, keepdims=True)
    return x * lax.rsqrt(ms + EPS) * g


def _ffn(x, g_pre, wg_ref, wu_ref, wd_ref, g_post):
    h = _rms(x, g_pre).astype(BF16)
    d = None
    for c0, c1 in FF_CHUNKS:
        gate = jnp.dot(h, wg_ref[:, c0:c1], preferred_element_type=F32)
        up = jnp.dot(h, wu_ref[:, c0:c1], preferred_element_type=F32)
        a = (gate * jax.nn.sigmoid(gate) * up).astype(BF16)
        part = jnp.dot(a, wd_ref[c0:c1, :], preferred_element_type=F32)
        d = part if d is None else d + part
    return x + 0.5 * _rms(d, g_post)


def _ffn_kernel(x_ref, gpre_ref, wg_ref, wu_ref, wd_ref, gpost_ref, o_ref):
    o_ref[0] = _ffn(x_ref[0], gpre_ref[...], wg_ref, wu_ref, wd_ref, gpost_ref[...])


def _ffn_call(x, g_pre, wg, wu, wd, g_post, tm):
    b, s, _ = x.shape
    row = pl.BlockSpec((1, tm, D_MODEL), lambda bi, i: (bi, i, 0))
    return pl.pallas_call(
        _ffn_kernel,
        grid=(b, s // tm),
        in_specs=[row, _const_spec((1, D_MODEL)), _const_spec(wg.shape), _const_spec(wu.shape),
                  _const_spec(wd.shape), _const_spec((1, D_MODEL))],
        out_specs=row,
        out_shape=jax.ShapeDtypeStruct(x.shape, F32),
        compiler_params=pltpu.CompilerParams(dimension_semantics=("arbitrary", "arbitrary"),
                                             vmem_limit_bytes=VMEM_LIMIT),
        name="ffn1",
    )(x, g_pre, wg, wu, wd, g_post)


_C_Q = 0
_C_KC = _C_Q + D_ATTN
_C_VC = _C_KC + D_KV
_C_KS = _C_VC + D_KV
_C_VS = _C_KS + D_KV
_C_KW = _C_VS + D_KV
_C_VW = _C_KW + D_KV
_C_GATE = _C_VW + D_KV
_C_BG = _C_GATE + GATE_COLS
_C_CG = _C_BG + D_CONV
_C_XC = _C_CG + D_CONV
_C_END = _C_XC + D_CONV


def _rope_kernel(pos_ref, invf_ref, cos_ref, sin_ref):
    ang = pos_ref[0].astype(F32) * invf_ref[...]
    cos_ref[0] = jnp.cos(ang)
    sin_ref[0] = jnp.sin(ang)


def _rope_call(pos3, invf, tm):
    b, s, _ = pos3.shape
    row = lambda width: pl.BlockSpec((1, tm, width), lambda bi, i: (bi, i, 0))
    shape = jax.ShapeDtypeStruct((b, s, LANES), F32)
    return pl.pallas_call(
        _rope_kernel,
        grid=(b, s // tm),
        in_specs=[row(1), _const_spec((1, LANES))],
        out_specs=[row(LANES), row(LANES)],
        out_shape=[shape, shape],
        compiler_params=pltpu.CompilerParams(dimension_semantics=("arbitrary", "arbitrary"),
                                             vmem_limit_bytes=VMEM_LIMIT),
        name="rope_tables",
    )(pos3, invf)


def _inproj_kernel(nbh, x_ref, cos_ref, sin_ref, g_ref, w_ref, convw_ref, convg_ref,
                   qraw_ref, qrot_ref, kc_ref, vc_ref, ksaug_ref, vs_ref, kw_ref, vw_ref,
                   gates_ref, convn_ref, ubuf_ref):
    i = pl.program_id(1)
    tm = x_ref.shape[1]
    h = _rms(x_ref[0], g_ref[...]).astype(BF16)

    def proj(c0, c1):
        return jnp.dot(h, w_ref[:, c0:c1], preferred_element_type=F32)

    cos = cos_ref[0]
    sin = sin_ref[0]
    lane = lax.broadcasted_iota(jnp.int32, (1, LANES), 1) & (HEAD_DIM - 1)
    s_lo = jnp.where(lane < ROPE_HALF, -sin, 0.0)
    s_hi = jnp.where(lane >= ROPE_HALF, sin, 0.0)

    def rope(z):
        return (z * cos + pltpu.roll(z, LANES - ROPE_HALF, 1) * s_lo
                + pltpu.roll(z, ROPE_HALF, 1) * s_hi)

    for c in range(0, D_ATTN // LANES, 2):
        zq2 = proj(_C_Q + c * LANES, _C_Q + (c + 2) * LANES)
        for cc in (c, c + 1):
            zq = zq2[:, (cc - c) * LANES:(cc - c + 1) * LANES]
            qraw_ref[0, :, cc * LANES:(cc + 1) * LANES] = zq.astype(BF16)
            qrot_ref[0, :, cc * LANES:(cc + 1) * LANES] = rope(zq).astype(BF16)

    kvc = proj(_C_KC, _C_KS)
    kc_ref[0] = kvc[:, 0:D_KV].reshape(tm // CMP_STRIDE, CMP_STRIDE, D_KV)
    vc_ref[0] = kvc[:, D_KV:2 * D_KV].reshape(tm // CMP_STRIDE, CMP_STRIDE, D_KV)
    kvs = proj(_C_KS, _C_KW)
    ks = rope(kvs[:, 0:D_KV]).astype(BF16)
    vs = kvs[:, D_KV:2 * D_KV].astype(BF16)
    kvw = proj(_C_KW, _C_GATE)
    kw = rope(kvw[:, 0:D_KV]).astype(BF16)
    vw = kvw[:, D_KV:2 * D_KV].astype(BF16)
    row_blk = ((i * tm + lax.broadcasted_iota(jnp.int32, (tm, nbh), 0)) >> SEL_SHIFT) & (nbh - 1)
    onehot = jnp.where(row_blk == lax.broadcasted_iota(jnp.int32, (tm, nbh), 1), 1.0, 0.0).astype(BF16)
    tail = jnp.where(lax.broadcasted_iota(jnp.int32, (tm, HEAD_DIM), 1) < N_SHIFT_COLS, 1.0, 0.0).astype(BF16)
    for g in range(N_GROUPS):
        sl = slice(g * HEAD_DIM, (g + 1) * HEAD_DIM)
        ksaug_ref[0, g, :, 0:nbh] = onehot
        ksaug_ref[0, g, :, nbh:nbh + HEAD_DIM] = ks[:, sl]
        ksaug_ref[0, g, :, nbh + HEAD_DIM:nbh + 2 * HEAD_DIM] = tail
        vs_ref[0, g, :, 0:HEAD_DIM] = vs[:, sl]
        vs_ref[0, g, :, HEAD_DIM:2 * HEAD_DIM] = tail
        kw_ref[0, g, :, 0:HEAD_DIM] = kw[:, sl]
        kw_ref[0, g, :, HEAD_DIM:2 * HEAD_DIM] = tail
        vw_ref[0, g, :, 0:HEAD_DIM] = vw[:, sl]
        vw_ref[0, g, :, HEAD_DIM:2 * HEAD_DIM] = tail

    gates_ref[0] = jax.nn.sigmoid(proj(_C_GATE, _C_BG))

    u = proj(_C_CG, _C_XC) * proj(_C_XC, _C_END)

    @pl.when(i == 0)
    def _():
        ubuf_ref[0:8, :] = jnp.zeros((8, D_CONV), F32)

    @pl.when(i > 0)
    def _():
        ubuf_ref[0:8, :] = ubuf_ref[tm:tm + 8, :]

    ubuf_ref[8:tm + 8, :] = u
    w = convw_ref[...]
    y = (w[2:3, :] * u + w[1:2, :] * ubuf_ref[7:tm + 7, :] + w[0:1, :] * ubuf_ref[6:tm + 6, :])
    conv = proj(_C_BG, _C_CG) * y
    convn_ref[0] = _rms(conv, convg_ref[...]).astype(BF16)


def _inproj_call(x, cos, sin, g_pre, w_in, conv_w, conv_g, tm, nbh):
    b, s, _ = x.shape
    grid = (b, s // tm)
    row = lambda width: pl.BlockSpec((1, tm, width), lambda bi, i: (bi, i, 0))
    grp = lambda width: pl.BlockSpec((1, N_GROUPS, tm, width), lambda bi, i: (bi, 0, i, 0))
    chunked = pl.BlockSpec((1, tm // CMP_STRIDE, CMP_STRIDE, D_KV), lambda bi, i: (bi, i, 0, 0))
    kern = functools.partial(_inproj_kernel, nbh)
    return pl.pallas_call(
        kern,
        grid=grid,
        in_specs=[row(D_MODEL), row(LANES), row(LANES), _const_spec((1, D_MODEL)), _const_spec(w_in.shape),
                  _const_spec((CONV_WIDTH, D_CONV)), _const_spec((1, D_CONV))],
        out_specs=[row(D_ATTN), row(D_ATTN), chunked, chunked, grp(nbh + 2 * HEAD_DIM), grp(2 * HEAD_DIM),
                   grp(2 * HEAD_DIM), grp(2 * HEAD_DIM), row(GATE_COLS), row(D_CONV)],
        out_shape=[
            jax.ShapeDtypeStruct((b, s, D_ATTN), BF16),
            jax.ShapeDtypeStruct((b, s, D_ATTN), BF16),
            jax.ShapeDtypeStruct((b, s // CMP_STRIDE, CMP_STRIDE, D_KV), F32),
            jax.ShapeDtypeStruct((b, s // CMP_STRIDE, CMP_STRIDE, D_KV), F32),
            jax.ShapeDtypeStruct((b, N_GROUPS, s, nbh + 2 * HEAD_DIM), BF16),
            jax.ShapeDtypeStruct((b, N_GROUPS, s, 2 * HEAD_DIM), BF16),
            jax.ShapeDtypeStruct((b, N_GROUPS, s, 2 * HEAD_DIM), BF16),
            jax.ShapeDtypeStruct((b, N_GROUPS, s, 2 * HEAD_DIM), BF16),
            jax.ShapeDtypeStruct((b, s, GATE_COLS), F32),
            jax.ShapeDtypeStruct((b, s, D_CONV), BF16),
        ],
        scratch_shapes=[pltpu.VMEM((tm + 8, D_CONV), F32)],
        compiler_params=pltpu.CompilerParams(dimension_semantics=("arbitrary", "arbitrary"),
                                             vmem_limit_bytes=VMEM_LIMIT),
        name="inproj",
    )(x, cos, sin, g_pre, w_in, conv_w, conv_g)


def _compress_kernel(kc_ref, kcn_ref, vc_ref, vcn_ref, pek_ref, w1k_ref, w2k_ref,
                     pev_ref, w1v_ref, w2v_ref, ko_ref, vo_ref):
    tc = kc_ref.shape[1]
    hid_w = N_GROUPS * CMP_HIDDEN
    last = lax.broadcasted_iota(jnp.int32, (tc, 1), 0) == tc - 1

    def one(x_ref, xn_ref, pe_ref, w1_ref, w2_ref, o_ref):
        def half_mlp(ref, half):
            acc = None
            for j in range(0, CMP_STRIDE, 2):
                cols = slice(j * D_KV, (j + 2) * D_KV)
                xj = jnp.concatenate([ref[0, :, j, :], ref[0, :, j + 1, :]], axis=-1)
                part = jnp.dot((xj + pe_ref[half:half + 1, cols]).astype(BF16), w1_ref[half, cols, :],
                               preferred_element_type=F32)
                acc = part if acc is None else acc + part
            return acc

        top = half_mlp(x_ref, 0)
        bot = half_mlp(x_ref, 1)
        botn = half_mlp(xn_ref, 1)
        shifted = jnp.where(last, botn[0:1, :], pltpu.roll(bot, tc - 1, 0))
        hid = jax.nn.gelu(top + shifted).astype(BF16)
        out = jnp.dot(hid, w2_ref[...], preferred_element_type=F32)
        tail = jnp.where(lax.broadcasted_iota(jnp.int32, (tc, HEAD_DIM), 1) < N_SHIFT_COLS, 1.0, 0.0).astype(BF16)
        for g in range(N_GROUPS):
            o_ref[0, g, :, 0:HEAD_DIM] = out[:, g * HEAD_DIM:(g + 1) * HEAD_DIM].astype(BF16)
            o_ref[0, g, :, HEAD_DIM:2 * HEAD_DIM] = tail

    one(kc_ref, kcn_ref, pek_ref, w1k_ref, w2k_ref, ko_ref)
    one(vc_ref, vcn_ref, pev_ref, w1v_ref, w2v_ref, vo_ref)


def _compress_call(kc_in, vc_in, pek, w1k, w2k, pev, w1v, w2v, tc):
    b, nch, _, width = kc_in.shape
    nt = nch // tc
    last8 = nch // 8 - 1
    cur = pl.BlockSpec((1, tc, CMP_STRIDE, width), lambda bi, i: (bi, i, 0, 0))
    nxt = pl.BlockSpec((1, 8, CMP_STRIDE, width),
                       lambda bi, i: (bi, jnp.minimum((i + 1) * (tc // 8), last8), 0, 0))
    out = pl.BlockSpec((1, N_GROUPS, tc, 2 * HEAD_DIM), lambda bi, i: (bi, 0, i, 0))
    oshape = jax.ShapeDtypeStruct((b, N_GROUPS, nch, 2 * HEAD_DIM), BF16)
    return pl.pallas_call(
        _compress_kernel,
        grid=(b, nt),
        in_specs=[cur, nxt, cur, nxt, _const_spec(pek.shape), _const_spec(w1k.shape), _const_spec(w2k.shape),
                  _const_spec(pev.shape), _const_spec(w1v.shape), _const_spec(w2v.shape)],
        out_specs=[out, out],
        out_shape=[oshape, oshape],
        compiler_params=pltpu.CompilerParams(dimension_semantics=("arbitrary", "arbitrary"),
                                             vmem_limit_bytes=VMEM_LIMIT),
        name="compress",
    )(kc_in, kc_in, vc_in, vc_in, pek, w1k, w2k, pev, w1v, w2v)


def _split3(x):
    hi = x.astype(BF16)
    r = x - hi.astype(F32)
    mid = r.astype(BF16)
    lo = (r - mid.astype(F32)).astype(BF16)
    return hi, mid, lo


def _tile_score_bound(qs, kmax):
    qf = qs.astype(F32)
    width = HPG * HEAD_DIM
    heads = jnp.where((lax.broadcasted_iota(jnp.int32, (width, LANES), 0) >> HEAD_SHIFT)
                      == lax.broadcasted_iota(jnp.int32, (width, LANES), 1), 1.0, 0.0).astype(BF16)
    nrm2 = jnp.dot((qf * qf).astype(BF16), heads, preferred_element_type=F32)
    top = jnp.max(jnp.max(nrm2, axis=0, keepdims=True), axis=1, keepdims=True)
    return jnp.sqrt(top * BF16_ROUND_UP) * kmax


def _shift_cols(shift):
    hi = shift.astype(BF16).astype(F32)
    lo = (shift - hi).astype(BF16).astype(F32)
    lane = lax.broadcasted_iota(jnp.int32, (1, HEAD_DIM), 1)
    return jnp.where(lane == 0, -hi, jnp.where(lane == 1, -lo, 0.0)).astype(BF16)


def _cmp_kernel(k_top, cw, q_ref, kaug_ref, vaug_ref, gates_ref, aggt_ref, o_ref, bias_ref,
                qaug_ref, e_ref, acc_ref, kmax_ref, pslc_ref):
    qt = pl.program_id(2)
    tq = q_ref.shape[1]
    nch = kaug_ref.shape[2]
    nsel = bias_ref.shape[3]
    m_rows = HPG * tq
    t0 = qt * tq
    tcol = slice(HEAD_DIM, 2 * HEAD_DIM)
    c_last = ((t0 + tq - CMP_BLOCK) >> CMP_SHIFT) // cw
    c_mask = jnp.maximum(c_last - 1, 0)

    @pl.when(qt == 0)
    def _():
        k = kaug_ref[0, 0, :, 0:HEAD_DIM].astype(F32)
        n_ok = lax.broadcasted_iota(jnp.int32, (nch, 1), 0) < nch - 1
        ksq = jnp.where(n_ok, jnp.sum(k * k, axis=-1, keepdims=True), 0.0)
        kmax_ref[...] = jnp.broadcast_to(jnp.sqrt(jnp.max(ksq, axis=0, keepdims=True)), kmax_ref.shape)

    qs = q_ref[0] * SCALE
    for h in range(HPG):
        qaug_ref[h * tq:(h + 1) * tq, 0:HEAD_DIM] = qs[:, h * HEAD_DIM:(h + 1) * HEAD_DIM]
    bound = _tile_score_bound(qs, kmax_ref[0:1, 0:1])
    safe = jnp.max(bound) <= SAFE_SCORE_BOUND

    def scores(c, masked):
        ka = kaug_ref[0, 0, pl.ds(pl.multiple_of(c * cw, cw), cw), :]
        s = lax.dot_general(qaug_ref[...], ka, (((1,), (1,)), ((), ())), preferred_element_type=F32)
        if masked:
            row_t = t0 + (lax.broadcasted_iota(jnp.int32, (m_rows, 1), 0) & (tq - 1))
            n_vis = (row_t - (CMP_BLOCK - 1)) >> CMP_SHIFT
            n = c * cw + lax.broadcasted_iota(jnp.int32, (m_rows, cw), 1)
            s = jnp.where(n <= n_vis, s, NEG_INF)
        return s

    def sweep(fn_full, fn_masked, init):
        carry = lax.fori_loop(0, c_mask, fn_full, init)
        return lax.fori_loop(c_mask, c_last + 1, fn_masked, carry)

    @pl.when(safe)
    def _():
        qaug_ref[:, tcol] = jnp.broadcast_to(_shift_cols(bound), (m_rows, HEAD_DIM))

    @pl.when(jnp.logical_not(safe))
    def _():
        qaug_ref[:, tcol] = jnp.zeros((m_rows, HEAD_DIM), BF16)
        step = lambda masked: (lambda c, m: jnp.maximum(m, jnp.max(scores(c, masked), axis=-1, keepdims=True)))
        mx = sweep(step(False), step(True), jnp.full((m_rows, 1), NEG_INF, F32))
        qaug_ref[:, tcol] = _shift_cols(jnp.where(mx > 0.5 * NEG_INF, mx, 0.0))

    acc_ref[...] = jnp.zeros(acc_ref.shape, F32)

    def chunk(masked):
        def body(c, carry):
            e = jnp.exp(scores(c, masked))
            start = pl.multiple_of(c * cw, cw)
            e_ref[c] = e
            acc_ref[...] += jnp.dot(e.astype(BF16), vaug_ref[0, 0, pl.ds(start, cw), :],
                                    preferred_element_type=F32)
            return carry
        return body

    sweep(chunk(False), chunk(True), 0)
    acc = acc_ref[...]
    l = acc[:, HEAD_DIM:HEAD_DIM + 1]
    rinv = jnp.where(l > 0.0, 1.0 / l, 0.0)
    o = acc[:, 0:HEAD_DIM] * rinv
    gates = gates_ref[0]
    o_ref[0] = jnp.concatenate(
        [o[h * tq:(h + 1) * tq] * gates[:, h:h + 1] for h in range(HPG)], axis=-1)

    pslc_ref[...] = jnp.zeros(pslc_ref.shape, F32)

    def agg_body(c, carry):
        imp = sum(e_ref[c, h * tq:(h + 1) * tq, :] * rinv[h * tq:(h + 1) * tq] for h in range(HPG))
        at = aggt_ref[c]
        pslc_ref[...] += sum(lax.dot_general(at, part, (((1,), (1,)), ((), ())), preferred_element_type=F32)
                             for part in _split3(imp))
        return carry

    lax.fori_loop(0, c_last + 1, agg_body, 0)

    n_slabs = nsel // SUBLANES
    sub = lax.broadcasted_iota(jnp.int32, (SUBLANES, LANES), 0)
    sub_f = sub.astype(F32)
    no_slab = float(n_slabs)
    n_cols = tq // LANES

    def tree(fn, xs):
        while len(xs) > 1:
            xs = [fn(xs[i], xs[i + 1]) if i + 1 < len(xs) else xs[i] for i in range(0, len(xs), 2)]
        return xs[0]

    def first_slab(score, m):
        firsts = []
        for g0 in range(0, len(score), SUBLANES):
            slab = jnp.full((SUBLANES, LANES), no_slab, F32)
            for r in reversed(range(g0, min(g0 + SUBLANES, len(score)))):
                slab = jnp.where(score[r] == m, float(r), slab)
            firsts.append(slab)
        return tree(jnp.minimum, firsts)

    def select_blocks(n_act, group):
        for c0 in range(0, n_cols, group):
            cols = range(c0, min(c0 + group, n_cols))
            lanes = {c: slice(c * LANES, (c + 1) * LANES) for c in cols}
            cur = {c: (t0 + c * LANES + lax.broadcasted_iota(jnp.int32, (1, LANES), 1)) >> SEL_SHIFT
                   for c in cols}
            score = {}
            for c in cols:
                score[c] = []
                for r in range(n_act):
                    j = sub + r * SUBLANES
                    free = (j <= cur[c] - 2) & (j > 0)
                    score[c].append(jnp.where(
                        free, pslc_ref[r * SUBLANES:(r + 1) * SUBLANES, lanes[c]], -jnp.inf))
            for _ in range(max(k_top - N_FORCED, 0)):
                for c in cols:
                    m = jnp.max(tree(jnp.maximum, score[c]), axis=0, keepdims=True)
                    slab = first_slab(score[c], m)
                    block = slab * SUBLANES + sub_f
                    first = jnp.min(block, axis=0, keepdims=True)
                    taken = jnp.where(block == first, slab, -1.0)
                    score[c] = [jnp.where(taken == float(r), -jnp.inf, score[c][r]) for r in range(n_act)]
            for c in cols:
                for r in range(n_act):
                    pslc_ref[r * SUBLANES:(r + 1) * SUBLANES, lanes[c]] = jnp.where(
                        (sub + r * SUBLANES <= cur[c]) & (score[c][r] == -jnp.inf), 0.0, NEG_INF)
        if n_act < n_slabs:
            pslc_ref[n_act * SUBLANES:, :] = jnp.full(((n_slabs - n_act) * SUBLANES, tq), NEG_INF, F32)

    if n_slabs % 2 == 0:
        half = n_slabs // 2
        few = t0 + tq <= half * SUBLANES * SEL_BLOCK
        pl.when(few)(lambda: select_blocks(half, 2))
        pl.when(jnp.logical_not(few))(lambda: select_blocks(n_slabs, 1))
    else:
        select_blocks(n_slabs, 1)
    bias_ref[0, 0] = pslc_ref[...].T.astype(BF16)


def _cmp_call(q_raw, kaug, vaug, gates, aggt, tq, cw):
    b, s, _ = q_raw.shape
    nch = kaug.shape[2]
    nsel = aggt.shape[1]
    assert nch % cw == 0 and tq // CMP_STRIDE < cw
    k_top = min(N_SELECT, nsel)
    gw = HPG * HEAD_DIM
    m_rows = HPG * tq
    qspec = pl.BlockSpec((1, tq, gw), lambda bi, g, i: (bi, i, g))
    kvspec = pl.BlockSpec((1, 1, nch, 2 * HEAD_DIM), lambda bi, g, i: (bi, g, 0, 0))
    gspec = pl.BlockSpec((1, tq, LANES), lambda bi, g, i: (bi, i, g))
    return pl.pallas_call(
        functools.partial(_cmp_kernel, k_top, cw),
        grid=(b, N_GROUPS, s // tq),
        in_specs=[qspec, kvspec, kvspec, gspec, _const_spec(aggt.shape)],
        out_specs=[qspec, pl.BlockSpec((1, 1, tq, nsel), lambda bi, g, i: (bi, g, i, 0))],
        out_shape=[jax.ShapeDtypeStruct((b, s, D_ATTN), F32),
                   jax.ShapeDtypeStruct((b, N_GROUPS, s, nsel), BF16)],
        scratch_shapes=[pltpu.VMEM((m_rows, 2 * HEAD_DIM), BF16),
                        pltpu.VMEM((nch // cw, m_rows, cw), F32),
                        pltpu.VMEM((m_rows, 2 * HEAD_DIM), F32),
                        pltpu.VMEM((8, LANES), F32),
                        pltpu.VMEM((nsel, tq), F32)],
        compiler_params=pltpu.CompilerParams(dimension_semantics=("arbitrary",) * 3,
                                             vmem_limit_bytes=VMEM_LIMIT),
        name="cmp_attn_topk",
    )(q_raw, kaug, vaug, gates, aggt)


def _slc_kernel(nbh, tk, q_ref, bias_ref, kaug_ref, vaug_ref, gates_ref, o_ref,
                qaug_ref, acc_ref, kmax_ref):
    qt = pl.program_id(2)
    tq = q_ref.shape[1]
    m_rows = HPG * tq
    t0 = qt * tq
    n_wide = t0 // tk
    n_narrow = (t0 - n_wide * tk) // tq
    n_half = qaug_ref.shape[0]
    kcol = slice(nbh, nbh + HEAD_DIM)
    tcol = slice(nbh + HEAD_DIM, nbh + 2 * HEAD_DIM)

    @pl.when(qt == 0)
    def _():
        def body(c, mx):
            k = kaug_ref[0, 0, pl.ds(pl.multiple_of(c * tk, tk), tk), kcol].astype(F32)
            return jnp.maximum(mx, jnp.sum(k * k, axis=-1, keepdims=True))
        mx = lax.fori_loop(0, kaug_ref.shape[2] // tk, body, jnp.zeros((tk, 1), F32))
        kmax_ref[...] = jnp.broadcast_to(jnp.sqrt(jnp.max(mx, axis=0, keepdims=True)), kmax_ref.shape)

    qs = q_ref[0] * SCALE
    bias = bias_ref[0, 0]
    for h in range(HPG):
        rows = slice(h * tq, (h + 1) * tq)
        for hf in range(n_half):
            qaug_ref[hf, rows, 0:nbh] = bias[:, hf * nbh:(hf + 1) * nbh]
            qaug_ref[hf, rows, kcol] = qs[:, h * HEAD_DIM:(h + 1) * HEAD_DIM]
    bound = _tile_score_bound(qs, kmax_ref[0:1, 0:1])
    safe = jnp.max(bound) <= SAFE_SCORE_BOUND

    def scores(start, width, diagonal):
        ka = kaug_ref[0, 0, pl.ds(start, width), :]
        qa = qaug_ref[start // (nbh * SEL_BLOCK)]
        s = lax.dot_general(qa, ka, (((1,), (1,)), ((), ())), preferred_element_type=F32)
        if diagonal:
            r = lax.broadcasted_iota(jnp.int32, (m_rows, width), 0) & (tq - 1)
            c = lax.broadcasted_iota(jnp.int32, (m_rows, width), 1)
            s = jnp.where(c <= r, s, NEG_INF)
        return s

    def sweep(fn, carry):
        carry = lax.fori_loop(
            0, n_wide, lambda i, c: fn(pl.multiple_of(i * tk, tk), tk, False, c), carry)
        carry = lax.fori_loop(
            0, n_narrow, lambda i, c: fn(pl.multiple_of(n_wide * tk + i * tq, tq), tq, False, c), carry)
        return fn(pl.multiple_of(t0, tq), tq, True, carry)

    @pl.when(safe)
    def _():
        tail = jnp.broadcast_to(_shift_cols(bound), (m_rows, HEAD_DIM))
        for hf in range(n_half):
            qaug_ref[hf, :, tcol] = tail

    @pl.when(jnp.logical_not(safe))
    def _():
        for hf in range(n_half):
            qaug_ref[hf, :, tcol] = jnp.zeros((m_rows, HEAD_DIM), BF16)
        mx = sweep(lambda st, w, dg, m: jnp.maximum(m, jnp.max(scores(st, w, dg), axis=-1, keepdims=True)),
                   jnp.full((m_rows, 1), NEG_INF, F32))
        tail = _shift_cols(mx)
        for hf in range(n_half):
            qaug_ref[hf, :, tcol] = tail

    def pv(start, width, diagonal):
        p = jnp.exp(scores(start, width, diagonal)).astype(BF16)
        return jnp.dot(p, vaug_ref[0, 0, pl.ds(start, width), :], preferred_element_type=F32)

    def tile(start, width, diagonal, carry):
        acc_ref[...] += pv(start, width, diagonal)
        return carry

    def wide_pair(i, carry):
        first = pl.multiple_of(2 * i * tk, tk)
        acc_ref[...] += pv(first, tk, False) + pv(pl.multiple_of(first + tk, tk), tk, False)
        return carry

    acc_ref[...] = pv(pl.multiple_of(t0, tq), tq, True)
    lax.fori_loop(0, n_wide // 2, wide_pair, 0)
    lax.fori_loop(2 * (n_wide // 2), n_wide, lambda i, c: tile(pl.multiple_of(i * tk, tk), tk, False, c), 0)
    lax.fori_loop(
        0, n_narrow, lambda i, c: tile(pl.multiple_of(n_wide * tk + i * tq, tq), tq, False, c), 0)
    acc = acc_ref[...]
    o = acc[:, 0:HEAD_DIM] / acc[:, HEAD_DIM:HEAD_DIM + 1]
    gates = gates_ref[0]
    o_ref[0] = jnp.concatenate(
        [o[h * tq:(h + 1) * tq] * gates[:, HPG + h:HPG + h + 1] for h in range(HPG)], axis=-1)


def _slc_call(q_rot, bias, kaug, v, gates, tq, tk, nbh):
    b, s, _ = q_rot.shape
    assert tk % tq == 0 and (nbh * SEL_BLOCK) % tk == 0 and s % tk == 0
    nsel = bias.shape[3]
    gw = HPG * HEAD_DIM
    aug = nbh + 2 * HEAD_DIM
    qspec = pl.BlockSpec((1, tq, gw), lambda bi, g, i: (bi, i, g))
    return pl.pallas_call(
        functools.partial(_slc_kernel, nbh, tk),
        grid=(b, N_GROUPS, s // tq),
        in_specs=[qspec,
                  pl.BlockSpec((1, 1, tq, nsel), lambda bi, g, i: (bi, g, i, 0)),
                  pl.BlockSpec((1, 1, s, aug), lambda bi, g, i: (bi, g, 0, 0)),
                  pl.BlockSpec((1, 1, s, 2 * HEAD_DIM), lambda bi, g, i: (bi, g, 0, 0)),
                  pl.BlockSpec((1, tq, LANES), lambda bi, g, i: (bi, i, g))],
        out_specs=qspec,
        out_shape=jax.ShapeDtypeStruct((b, s, D_ATTN), F32),
        scratch_shapes=[pltpu.VMEM((nsel // nbh, HPG * tq, aug), BF16),
                        pltpu.VMEM((HPG * tq, 2 * HEAD_DIM), F32),
                        pltpu.VMEM((8, LANES), F32)],
        compiler_params=pltpu.CompilerParams(dimension_semantics=("arbitrary",) * 3,
                                             vmem_limit_bytes=VMEM_LIMIT),
        name="slc_attn",
    )(q_rot, bias, kaug, v, gates)


def _win_kernel(tq, q_ref, kaug_ref, vaug_ref, gates_ref, o_ref, qaug_ref, kmax_ref):
    qb = pl.program_id(2)
    tb = q_ref.shape[1]
    n_sub = tb // tq
    span = WINDOW + tq
    m_rows = HPG * tq
    tcol = slice(HEAD_DIM, 2 * HEAD_DIM)

    @pl.when(qb == 0)
    def _():
        def body(c, mx):
            k = kaug_ref[0, 0, pl.ds(pl.multiple_of(c * tq, tq), tq), 0:HEAD_DIM].astype(F32)
            return jnp.maximum(mx, jnp.sum(k * k, axis=-1, keepdims=True))
        mx = lax.fori_loop(0, kaug_ref.shape[2] // tq, body, jnp.zeros((tq, 1), F32))
        kmax_ref[...] = jnp.broadcast_to(jnp.sqrt(jnp.max(mx, axis=0, keepdims=True)), kmax_ref.shape)

    qs = q_ref[0] * SCALE
    for j in range(n_sub):
        for h in range(HPG):
            qaug_ref[j, h * tq:(h + 1) * tq, 0:HEAD_DIM] = qs[j * tq:(j + 1) * tq, h * HEAD_DIM:(h + 1) * HEAD_DIM]
    bound = _tile_score_bound(qs, kmax_ref[0:1, 0:1])
    safe = jnp.max(bound) <= SAFE_SCORE_BOUND

    def band_start(j):
        return pl.multiple_of(jnp.maximum(qb * tb + j * tq - WINDOW, 0), tq)

    def scores(j):
        t0 = qb * tb + j * tq
        start = band_start(j)
        ka = kaug_ref[0, 0, pl.ds(start, span), :]
        s = lax.dot_general(qaug_ref[j], ka, (((1,), (1,)), ((), ())), preferred_element_type=F32)
        d = (lax.broadcasted_iota(jnp.int32, (m_rows, tq), 1)
             - (lax.broadcasted_iota(jnp.int32, (m_rows, tq), 0) & (tq - 1)))
        blocks = []
        for blk in range(span // tq):
            off = t0 - start - blk * tq
            ok = lax.bitcast_convert_type(off - d, jnp.uint32) < jnp.uint32(WINDOW)
            blocks.append(jnp.where(ok, s[:, blk * tq:(blk + 1) * tq], NEG_INF))
        return jnp.concatenate(blocks, axis=-1)

    @pl.when(safe)
    def _():
        for j in range(n_sub):
            qaug_ref[j, :, tcol] = jnp.broadcast_to(_shift_cols(bound), (m_rows, HEAD_DIM))

    @pl.when(jnp.logical_not(safe))
    def _():
        for j in range(n_sub):
            qaug_ref[j, :, tcol] = jnp.zeros((m_rows, HEAD_DIM), BF16)
            qaug_ref[j, :, tcol] = _shift_cols(jnp.max(scores(j), axis=-1, keepdims=True))

    gates = gates_ref[0]
    for j in range(n_sub):
        p = jnp.exp(scores(j)).astype(BF16)
        acc = jnp.dot(p, vaug_ref[0, 0, pl.ds(band_start(j), span), :], preferred_element_type=F32)
        o = acc[:, 0:HEAD_DIM] / acc[:, HEAD_DIM:HEAD_DIM + 1]
        gj = gates[j * tq:(j + 1) * tq]
        o_ref[0, j * tq:(j + 1) * tq, :] = jnp.concatenate(
            [o[h * tq:(h + 1) * tq] * gj[:, 2 * HPG + h:2 * HPG + h + 1] for h in range(HPG)], axis=-1)


def _win_call(q_rot, k, v, gates, tb, tq):
    b, s, _ = q_rot.shape
    assert tb % tq == 0 and WINDOW % tq == 0
    gw = HPG * HEAD_DIM
    qspec = pl.BlockSpec((1, tb, gw), lambda bi, g, i: (bi, i, g))
    kvspec = pl.BlockSpec((1, 1, s, 2 * HEAD_DIM), lambda bi, g, i: (bi, g, 0, 0))
    return pl.pallas_call(
        functools.partial(_win_kernel, tq),
        grid=(b, N_GROUPS, s // tb),
        in_specs=[qspec, kvspec, kvspec, pl.BlockSpec((1, tb, LANES), lambda bi, g, i: (bi, i, g))],
        out_specs=qspec,
        out_shape=jax.ShapeDtypeStruct((b, s, D_ATTN), F32),
        scratch_shapes=[pltpu.VMEM((tb // tq, HPG * tq, 2 * HEAD_DIM), BF16),
                        pltpu.VMEM((8, LANES), F32)],
        compiler_params=pltpu.CompilerParams(dimension_semantics=("arbitrary",) * 3,
                                             vmem_limit_bytes=VMEM_LIMIT),
        name="win_attn",
    )(q_rot, k, v, gates)


def _out_kernel(x_ref, oc_ref, os_ref, ow_ref, convn_ref, ga_ref, wout_ref, gpost_ref,
                gpre2_ref, wg_ref, wu_ref, wd_ref, gpost2_ref, o_ref):
    attn = oc_ref[0] + os_ref[0] + ow_ref[0]
    an = _rms(attn, ga_ref[...]).astype(BF16)
    h = (jnp.dot(an, wout_ref[0:D_ATTN, :], preferred_element_type=F32)
         + jnp.dot(convn_ref[0], wout_ref[D_ATTN:D_MODEL, :], preferred_element_type=F32))
    x1 = x_ref[0] + _rms(h, gpost_ref[...])
    o_ref[0] = _ffn(x1, gpre2_ref[...], wg_ref, wu_ref, wd_ref, gpost2_ref[...])


def _out_call(x, oc, osl, ow, convn, ga, wout, gpost, gpre2, wg, wu, wd, gpost2, tm):
    b, s, _ = x.shape
    row = lambda width: pl.BlockSpec((1, tm, width), lambda bi, i: (bi, i, 0))
    return pl.pallas_call(
        _out_kernel,
        grid=(b, s // tm),
        in_specs=[row(D_MODEL), row(D_ATTN), row(D_ATTN), row(D_ATTN), row(D_CONV),
                  _const_spec((1, D_ATTN)), _const_spec(wout.shape), _const_spec((1, D_MODEL)),
                  _const_spec((1, D_MODEL)), _const_spec(wg.shape), _const_spec(wu.shape),
                  _const_spec(wd.shape), _const_spec((1, D_MODEL))],
        out_specs=row(D_MODEL),
        out_shape=jax.ShapeDtypeStruct(x.shape, F32),
        compiler_params=pltpu.CompilerParams(dimension_semantics=("arbitrary", "arbitrary"),
                                             vmem_limit_bytes=VMEM_LIMIT),
        name="outproj_ffn2",
    )(x, oc, osl, ow, convn, ga, wout, gpost, gpre2, wg, wu, wd, gpost2)


def _prep_w_in(w_in):
    sizes = [D_ATTN] + [D_KV] * 6 + [N_GATES] + [D_CONV] * 3
    cuts = np.cumsum([0] + sizes)
    q, kc, vc, ks, vs, kw, vw, gt, bg, cg, xc = [w_in[:, cuts[i]:cuts[i + 1]] for i in range(len(sizes))]
    gt = gt.reshape(D_MODEL, N_GROUPS, HPG, 3).transpose(0, 1, 3, 2).reshape(D_MODEL, N_GROUPS, 3 * HPG)
    gt = jnp.pad(gt, ((0, 0), (0, 0), (0, LANES - 3 * HPG))).reshape(D_MODEL, GATE_COLS)
    return jnp.concatenate([q, kc, vc, ks, vs, kw, vw, gt, bg, cg, xc], axis=1).astype(BF16)


def _prep_compress(pe, w1, w2):
    half = CMP_BLOCK // 2
    eye = jnp.eye(N_GROUPS, dtype=F32)
    w1r = w1.reshape(2, half, HEAD_DIM, CMP_HIDDEN)
    w1big = jnp.einsum("ptdc,gh->ptgdhc", w1r, eye).reshape(2, half * D_KV, N_GROUPS * CMP_HIDDEN)
    w2big = jnp.einsum("cd,gh->gchd", w2, eye).reshape(N_GROUPS * CMP_HIDDEN, D_KV)
    pe_rows = jnp.broadcast_to(pe.reshape(2, half, 1, HEAD_DIM), (2, half, N_GROUPS, HEAD_DIM))
    return pe_rows.reshape(2, half * D_KV), w1big.astype(BF16), w2big.astype(BF16)


def _agg_matrix(nch, nsel, cw):
    agg_w = np.convolve(np.ones(SEL_RATIO), np.ones(CMP_BLOCK // CMP_STRIDE))
    a = np.zeros((nch, nsel), np.float32)
    for j in range(nsel):
        for o, wgt in enumerate(agg_w):
            c = SEL_RATIO * j + o - (CMP_BLOCK // CMP_STRIDE - 1)
            if 0 <= c < nch - 1:
                a[c, j] = wgt
    a = a.T.reshape(nsel, nch // cw, cw).transpose(1, 0, 2)
    return jnp.asarray(a, BF16)


def _rope_inv_freq_row():
    inv = ROPE_THETA ** (-np.arange(ROPE_HALF, dtype=np.float32) * 2.0 / ROPE_DIM)
    lane = np.arange(LANES) % HEAD_DIM
    row = np.where(lane < ROPE_DIM, inv[lane % ROPE_HALF], 0.0).astype(np.float32)
    return jnp.asarray(row.reshape(1, LANES))


def _forward(x, positions, p, *, tm, tq, tc, nbh, tqs, tks, tqc, cw, tbw):
    b, s, _ = x.shape
    depth = p["w_in"].shape[0]
    nch = s // CMP_STRIDE
    nsel = s // SEL_BLOCK
    cos, sin = _rope_call(positions.reshape(b, s, 1), _rope_inv_freq_row(), tm)
    agg = _agg_matrix(nch, nsel, cw)
    row = lambda v: v.reshape(1, -1)
    for l in range(depth):
        x = _ffn_call(x, row(p["ffn1_norm_pre"][l]), p["ffn1_w_gate"][l].astype(BF16),
                      p["ffn1_w_up"][l].astype(BF16), p["ffn1_w_down"][l].astype(BF16),
                      row(p["ffn1_norm_post"][l]), tm)
        (q_raw, q_rot, kc_in, vc_in, ksaug, vs, kw, vw, gates, convn) = _inproj_call(
            x, cos, sin, row(p["mix_norm_pre"][l]), _prep_w_in(p["w_in"][l]),
            p["conv_w"][l], row(p["conv_out_norm"][l]), tm, nbh)
        pek, w1k, w2k = _prep_compress(p["cmp_pe_k"][l], p["cmp_w1_k"][l], p["cmp_w2_k"][l])
        pev, w1v, w2v = _prep_compress(p["cmp_pe_v"][l], p["cmp_w1_v"][l], p["cmp_w2_v"][l])
        kcmp, vcmp = _compress_call(kc_in, vc_in, pek, w1k, w2k, pev, w1v, w2v, tc)
        o_cmp, bias = _cmp_call(q_raw, kcmp, vcmp, gates, agg, tqc, cw)
        o_slc = _slc_call(q_rot, bias, ksaug, vs, gates, tqs, tks, nbh)
        o_win = _win_call(q_rot, kw, vw, gates, tbw, tq)
        x = _out_call(x, o_cmp, o_slc, o_win, convn,
                      row(p["attn_out_norm"][l]), p["w_out"][l].astype(BF16), row(p["mix_norm_post"][l]),
                      row(p["ffn2_norm_pre"][l]), p["ffn2_w_gate"][l].astype(BF16),
                      p["ffn2_w_up"][l].astype(BF16), p["ffn2_w_down"][l].astype(BF16),
                      row(p["ffn2_norm_post"][l]), tm)
    return x


def kernel(x, positions, ffn1_norm_pre, ffn1_w_gate, ffn1_w_up, ffn1_w_down, ffn1_norm_post, mix_norm_pre, w_in, cmp_pe_k, cmp_w1_k, cmp_w2_k, cmp_pe_v, cmp_w1_v, cmp_w2_v, conv_w, attn_out_norm, conv_out_norm, w_out, mix_norm_post, ffn2_norm_pre, ffn2_w_gate, ffn2_w_up, ffn2_w_down, ffn2_norm_post):
    params = dict(
        ffn1_norm_pre=ffn1_norm_pre, ffn1_w_gate=ffn1_w_gate, ffn1_w_up=ffn1_w_up, ffn1_w_down=ffn1_w_down,
        ffn1_norm_post=ffn1_norm_post, mix_norm_pre=mix_norm_pre, w_in=w_in,
        cmp_pe_k=cmp_pe_k, cmp_w1_k=cmp_w1_k, cmp_w2_k=cmp_w2_k,
        cmp_pe_v=cmp_pe_v, cmp_w1_v=cmp_w1_v, cmp_w2_v=cmp_w2_v,
        conv_w=conv_w, attn_out_norm=attn_out_norm, conv_out_norm=conv_out_norm, w_out=w_out,
        mix_norm_post=mix_norm_post, ffn2_norm_pre=ffn2_norm_pre, ffn2_w_gate=ffn2_w_gate,
        ffn2_w_up=ffn2_w_up, ffn2_w_down=ffn2_w_down, ffn2_norm_post=ffn2_norm_post)
    return _forward(x, positions, params, tm=512, tq=256, tc=256, nbh=128, tqs=512, tks=1024, tqc=1024, cw=256, tbw=1024)
```
